```python
import jax, jax.numpy as jnp
from jax import lax
import numpy as np

D_MODEL = 4096
BATCH = 4
SEQ = 2048
DEPTH = 1
DEC_BATCH = 128
DEC_SEQ = 1
PAST_LEN = 16384
PAGE_SIZE = 128

E_A = D_MODEL // 2
CHUNK_A = 128
A_GROUPS = 16
A_GROUP_DIM = E_A // A_GROUPS
E_B = D_MODEL // 2
B_HEAD_DIM = 128
B_HEADS = E_B // B_HEAD_DIM
CHUNK_B = 64
COL_U = 0
COL_V = COL_U + E_A
COL_ZA = COL_V + E_A
COL_Q = COL_ZA + E_A
COL_F = COL_Q + E_B
COL_I = COL_F + E_B
COL_ZB = COL_I + E_B
COL_GA = COL_ZB + E_B
COL_GB = COL_GA + D_MODEL
N_COLS = COL_GB + D_MODEL
EPS = 1e-6

kernel_name = 'hybrid_gmlp_hgrn2_step'


def _rmsnorm(x, g):
    xf = x.astype(jnp.float32)
    y = xf * lax.rsqrt(jnp.mean(xf * xf, axis=-1, keepdims=True) + EPS)
    return (y * g.astype(jnp.float32)).astype(x.dtype)


def _layernorm(x, g, b):
    xf = x.astype(jnp.float32)
    mu = jnp.mean(xf, axis=-1, keepdims=True)
    xc = xf - mu
    y = xc * lax.rsqrt(jnp.mean(xc * xc, axis=-1, keepdims=True) + EPS)
    return (y * g.astype(jnp.float32) + b.astype(jnp.float32)).astype(x.dtype)


def _chunk_spatial_gate(v, w_s, b_s):
    bn, t, _ = v.shape
    pad = (-t) % CHUNK_A
    vp = jnp.pad(v, ((0, 0), (0, pad), (0, 0)))
    n = (t + pad) // CHUNK_A
    vc = vp.reshape(bn, n, CHUNK_A, A_GROUPS, A_GROUP_DIM)
    mask = jnp.tril(jnp.ones((CHUNK_A, CHUNK_A), dtype=bool))
    wm = jnp.where(mask[None], w_s, jnp.zeros((), w_s.dtype))
    mixed = jnp.einsum('gts,bnsgc->bntgc', wm, vc) + b_s.T[None, None, :, :, None]
    return mixed.reshape(bn, n * CHUNK_A, E_A)[:, :t]


def _hgrn2(q, log_f, k, i, s0):
    bn, t, h, _ = q.shape
    c = CHUNK_B if t >= CHUNK_B else t
    pad = (-t) % c
    if pad:
        pw = ((0, 0), (0, pad), (0, 0), (0, 0))
        q, log_f, k, i = [jnp.pad(a, pw) for a in (q, log_f, k, i)]
    n = (t + pad) // c

    def to_chunks(a):
        return a.reshape(bn, n, c, h, a.shape[-1]).transpose(1, 0, 2, 3, 4)

    mask = jnp.tril(jnp.ones((c, c), dtype=bool))[None, :, :, None, None]

    def step(s, inp):
        qc, lfc, kc, ic = inp
        b = jnp.cumsum(lfc, axis=1)
        o_inter = jnp.einsum('bthd,bhde->bthe', qc * jnp.exp(b), s)
        dec = jnp.exp(jnp.where(mask, b[:, :, None] - b[:, None, :], -jnp.inf))
        att = jnp.einsum('bthd,bshd,btshd->bhts', qc, kc, dec)
        o_intra = jnp.einsum('bhts,bshe->bthe', att, ic)
        b_last = b[:, -1]
        s_new = jnp.exp(b_last)[..., None] * s + jnp.einsum(
            'bshd,bshe->bhde', kc * jnp.exp(b_last[:, None] - b), ic)
        return s_new, o_inter + o_intra

    s_t, o = lax.scan(step, s0, (to_chunks(q), to_chunks(log_f), to_chunks(k), to_chunks(i)))
    o = o.transpose(1, 0, 2, 3, 4).reshape(bn, n * c, h, i.shape[-1])[:, :t]
    return o, s_t


def _layer(x, s0, layer_idx, lb_logits, g_pre, w_in, ln_g, ln_b, w_s, b_s,
           g_onorm, w_pa, w_pb, w_o, g_post):
    bn, t, _ = x.shape
    xn = _rmsnorm(x, g_pre)
    hcat = xn @ w_in
    hu = hcat[..., COL_U:COL_V]
    hv = hcat[..., COL_V:COL_ZA]
    za = hcat[..., COL_ZA:COL_Q]
    hq = hcat[..., COL_Q:COL_F]
    hf = hcat[..., COL_F:COL_I]
    hi = hcat[..., COL_I:COL_ZB]
    zb = hcat[..., COL_ZB:COL_GA]
    hga = hcat[..., COL_GA:COL_GB]
    hgb = hcat[..., COL_GB:N_COLS]

    u = jax.nn.gelu(hu)
    vn = _layernorm(jax.nn.gelu(hv), ln_g, ln_b)
    y_a = u * _chunk_spatial_gate(vn, w_s, b_s) * jax.nn.silu(za)

    lb = jnp.cumsum(jax.nn.softmax(lb_logits.astype(jnp.float32), axis=0), axis=0)[layer_idx]
    zf = hf.astype(jnp.float32)
    log_f = jnp.log(lb + (1.0 - lb) * jax.nn.sigmoid(zf))
    k = (1.0 - lb) * jax.nn.sigmoid(-zf)
    q = jax.nn.silu(hq.astype(jnp.float32))

    def heads(a):
        return a.reshape(bn, t, B_HEADS, B_HEAD_DIM)

    o, s_t = _hgrn2(heads(q), heads(log_f), heads(k), heads(hi.astype(jnp.float32)),
                    s0.astype(jnp.float32))
    o = o * lax.rsqrt(jnp.mean(o * o, axis=-1, keepdims=True) + EPS)
    o = o.reshape(bn, t, E_B) * g_onorm.astype(jnp.float32)
    y_b = o.astype(x.dtype) * jax.nn.silu(zb)

    merged = jax.nn.sigmoid(hga) * (y_a @ w_pa) + jax.nn.sigmoid(hgb) * (y_b @ w_pb)
    out = x + _rmsnorm(merged @ w_o, g_post)
    n_last = (t - 1) % CHUNK_A + 1
    return out, s_t.astype(x.dtype), vn[:, t - n_last:]


def setup_inputs(seed: int = 0) -> dict:
    key = jax.random.key(seed)
    ks = jax.random.split(key, 20)
    f32 = jnp.float32
    nrm = lambda k, shape, s: jax.random.normal(k, shape, f32) * s
    return {
        'x_prompt': nrm(ks[0], (BATCH, SEQ, D_MODEL), 1.0),
        'x_sample': nrm(ks[1], (DEC_BATCH, DEC_SEQ, D_MODEL), 1.0),
        'state_hgrn': nrm(ks[2], (DEPTH, DEC_BATCH, B_HEADS, B_HEAD_DIM, B_HEAD_DIM), 0.5),
        'lb_logits': nrm(ks[3], (DEPTH + 1, E_B), 1.0),
        'g_pre': 1.0 + nrm(ks[4], (DEPTH, D_MODEL), 0.05),
        'w_in': nrm(ks[5], (DEPTH, D_MODEL, N_COLS), D_MODEL ** -0.5),
        'ln_g': 1.0 + nrm(ks[6], (DEPTH, E_A), 0.05),
        'ln_b': nrm(ks[7], (DEPTH, E_A), 0.02),
        'w_s': nrm(ks[8], (DEPTH, A_GROUPS, CHUNK_A, CHUNK_A), CHUNK_A ** -0.5),
        'b_s': 1.0 + nrm(ks[9], (DEPTH, A_GROUPS, CHUNK_A), 0.1),
        'g_onorm': 1.0 + nrm(ks[10], (DEPTH, E_B), 0.05),
        'w_pa': nrm(ks[11], (DEPTH, E_A, D_MODEL), E_A ** -0.5),
        'w_pb': nrm(ks[12], (DEPTH, E_B, D_MODEL), E_B ** -0.5),
        'w_o': nrm(ks[13], (DEPTH, D_MODEL, D_MODEL), D_MODEL ** -0.5),
        'g_post': 1.0 + nrm(ks[14], (DEPTH, D_MODEL), 0.05),
    }


def reference(x_prompt, x_sample, state_hgrn, lb_logits, g_pre, w_in, ln_g, ln_b, w_s, b_s,
              g_onorm, w_pa, w_pb, w_o, g_post):
    yp, ys = x_prompt, x_sample
    sp_list, ss_list, vp_list, vs_list = [], [], [], []
    s0_prompt = jnp.zeros((x_prompt.shape[0], B_HEADS, B_HEAD_DIM, B_HEAD_DIM), x_prompt.dtype)
    for l in range(DEPTH):
        params = (g_pre[l], w_in[l], ln_g[l], ln_b[l], w_s[l], b_s[l],
                  g_onorm[l], w_pa[l], w_pb[l], w_o[l], g_post[l])
        yp, sp, vp = _layer(yp, s0_prompt, l, lb_logits, *params)
        ys, ss, vs = _layer(ys, state_hgrn[l], l, lb_logits, *params)
        sp_list.append(sp)
        ss_list.append(ss)
        vp_list.append(vp)
        vs_list.append(vs)
    state_hgrn_prompt = jnp.stack(sp_list)
    state_hgrn_sample = jnp.stack(ss_list)
    vrows_prompt = jnp.stack(vp_list)
    vrows_sample = jnp.stack(vs_list)
    return (yp, ys, state_hgrn_prompt, state_hgrn_sample, vrows_prompt, vrows_sample)
```

```python
import functools

import jax
import jax.numpy as jnp
from jax import lax
from jax.experimental import pallas as pl
from jax.experimental.pallas import tpu as pltpu

F32 = jnp.float32
BF16 = jnp.bfloat16

EPS = 1e-6
LANES = 128
SUBLANES = 8
CHUNK_A = 128
HEAD_DIM = 128
CHUNK_B = 128
VMEM_LIMIT_BYTES = 56 * 1024 * 1024
ROW_TILE = 1024
COL_TILE = 512
NORM_ROWS = 256
SAMPLE_SEQ_BLOCK = 8


def _params(sem):
    return pltpu.CompilerParams(dimension_semantics=sem,
                                vmem_limit_bytes=VMEM_LIMIT_BYTES)


def _rmsnorm_kernel(x_ref, g_ref, o_ref):
    x = x_ref[...]
    ms = jnp.mean(x * x, axis=-1, keepdims=True)
    o_ref[...] = (x * lax.rsqrt(ms + EPS) * g_ref[...]).astype(o_ref.dtype)


def _rmsnorm(x, g, rows):
    m, d = x.shape
    return pl.pallas_call(
        _rmsnorm_kernel,
        grid=(m // rows,),
        in_specs=[pl.BlockSpec((rows, d), lambda i: (i, 0)),
                  pl.BlockSpec((1, d), lambda i: (0, 0))],
        out_specs=pl.BlockSpec((rows, d), lambda i: (i, 0)),
        out_shape=jax.ShapeDtypeStruct((m, d), BF16),
        compiler_params=_params(("arbitrary",)),
        name="pre_rmsnorm",
    )(x, g)


def _proj_kernel(*refs, epilogue, out_head_major, n_aux, n_prompt_tiles):
    n_out = len(out_head_major)
    xp_ref, xs_ref, w_ref = refs[:3]
    aux_refs = refs[3:3 + n_aux]
    out_refs = refs[3 + n_aux:3 + n_aux + 2 * n_out]
    wbf_ref = refs[-1]
    i = pl.program_id(1)

    @pl.when(i == 0)
    def _():
        wbf_ref[...] = w_ref[...].astype(BF16)

    def run(x_ref, outs):
        acc = jnp.dot(x_ref[...].astype(BF16), wbf_ref[...],
                      preferred_element_type=F32)
        res = epilogue(acc, *[a[...] for a in aux_refs])
        for r, o_ref, hm in zip(res, outs, out_head_major):
            if hm:
                for hh in range(o_ref.shape[0]):
                    o_ref[hh] = r[:, hh * LANES:(hh + 1) * LANES].astype(o_ref.dtype)
            else:
                o_ref[...] = r.astype(o_ref.dtype)

    @pl.when(i < n_prompt_tiles)
    def _():
        run(xp_ref, out_refs[0::2])

    @pl.when(i == n_prompt_tiles)
    def _():
        run(xs_ref, out_refs[1::2])


def _proj(xp, xs, w3, col0, ncols, epilogue, outs, aux=(), name="proj"):
    mp, kdim = xp.shape
    ms = xs.shape[0]
    tm, tn = ROW_TILE, COL_TILE
    npt = mp // tm
    nj = ncols // tn
    c0 = col0 // tn
    hpt = tn // LANES

    def row_i(i):
        return jnp.minimum(i, npt - 1)

    in_specs = [
        pl.BlockSpec((tm, kdim), lambda j, i: (row_i(i), 0)),
        pl.BlockSpec((ms, kdim), lambda j, i: (0, 0)),
        pl.BlockSpec((None, kdim, tn), lambda j, i: (0, 0, c0 + j)),
    ]
    for a in aux:
        in_specs.append(pl.BlockSpec((a.shape[0], tn), lambda j, i: (0, j)))
    out_specs, out_shapes = [], []
    for pdt, sdt, hm in outs:
        if hm:
            out_specs.append(pl.BlockSpec((hpt, tm, LANES), lambda j, i: (j, row_i(i), 0)))
            out_shapes.append(jax.ShapeDtypeStruct((ncols // LANES, mp, LANES), pdt))
            out_specs.append(pl.BlockSpec((hpt, ms, LANES), lambda j, i: (j, 0, 0)))
            out_shapes.append(jax.ShapeDtypeStruct((ncols // LANES, ms, LANES), sdt))
        else:
            out_specs.append(pl.BlockSpec((tm, tn), lambda j, i: (row_i(i), j)))
            out_shapes.append(jax.ShapeDtypeStruct((mp, ncols), pdt))
            out_specs.append(pl.BlockSpec((ms, tn), lambda j, i: (0, j)))
            out_shapes.append(jax.ShapeDtypeStruct((ms, ncols), sdt))
    kern = functools.partial(
        _proj_kernel, epilogue=epilogue,
        out_head_major=tuple(hm for _, _, hm in outs),
        n_aux=len(aux), n_prompt_tiles=npt)
    return pl.pallas_call(
        kern,
        grid=(nj, npt + 1),
        in_specs=in_specs,
        out_specs=out_specs,
        out_shape=out_shapes,
        scratch_shapes=[pltpu.VMEM((kdim, tn), BF16)],
        compiler_params=_params(("arbitrary", "arbitrary")),
        name=name,
    )(xp, xs, w3, *aux)


def _ep_gelu(acc):
    return (jax.nn.gelu(acc, approximate=True),)


def _ep_silu(acc):
    return (acc * jax.nn.sigmoid(acc),)


def _ep_sigmoid(acc):
    return (jax.nn.sigmoid(acc),)


def _ep_identity(acc):
    return (acc,)


def _ep_forget(acc, lbl):
    m = jnp.max(lbl, axis=0, keepdims=True)
    e = jnp.exp(lbl - m)
    lb = e[0:1, :] / jnp.sum(e, axis=0, keepdims=True)
    log_f = jnp.log(lb + (1.0 - lb) * jax.nn.sigmoid(acc))
    k = (1.0 - lb) * jax.nn.sigmoid(-acc)
    return log_f, k


def _layernorm(gv, g, b):
    mu = jnp.mean(gv, axis=-1, keepdims=True)
    xc = gv - mu
    var = jnp.mean(xc * xc, axis=-1, keepdims=True)
    return xc * lax.rsqrt(var + EPS) * g + b


def _gate_kernel(u_ref, gv_ref, sza_ref, lng_ref, lnb_ref, ws_ref, bs_ref,
                 ya_ref, vr_ref, wm_ref, *, n_chunks, n_groups):
    n = pl.program_id(1)

    @pl.when((pl.program_id(0) == 0) & (n == 0))
    def _():
        r = lax.broadcasted_iota(jnp.int32, (CHUNK_A, CHUNK_A), 0)
        c = lax.broadcasted_iota(jnp.int32, (CHUNK_A, CHUNK_A), 1)
        for g in range(n_groups):
            wm_ref[g] = jnp.where(c <= r, ws_ref[g], 0.0).astype(BF16)

    vn = _layernorm(gv_ref[...].astype(F32), lng_ref[...], lnb_ref[...])

    @pl.when(n == n_chunks - 1)
    def _():
        vr_ref[...] = vn

    vnb = vn.astype(BF16)
    for g in range(n_groups):
        sl = slice(g * LANES, (g + 1) * LANES)
        mixed = jnp.dot(wm_ref[g], vnb[:, sl], preferred_element_type=F32) + bs_ref[:, sl]
        ya_ref[:, sl] = (u_ref[:, sl].astype(F32) * mixed
                         * sza_ref[:, sl].astype(F32)).astype(ya_ref.dtype)


def _gate_prompt(uv, sza, ln_g, ln_b, w_s, bs_rows, batch, seq):
    e_a = sza.shape[1]
    n_chunks = seq // CHUNK_A
    n_groups = w_s.shape[0]
    blk = lambda b, n: (b * n_chunks + n, 0)
    kern = functools.partial(_gate_kernel, n_chunks=n_chunks, n_groups=n_groups)
    return pl.pallas_call(
        kern,
        grid=(batch, n_chunks),
        in_specs=[
            pl.BlockSpec((CHUNK_A, e_a), blk),
            pl.BlockSpec((CHUNK_A, e_a), lambda b, n: (b * n_chunks + n, 1)),
            pl.BlockSpec((CHUNK_A, e_a), blk),
            pl.BlockSpec((1, e_a), lambda b, n: (0, 0)),
            pl.BlockSpec((1, e_a), lambda b, n: (0, 0)),
            pl.BlockSpec(w_s.shape, lambda b, n: (0, 0, 0)),
            pl.BlockSpec((CHUNK_A, e_a), lambda b, n: (0, 0)),
        ],
        out_specs=[
            pl.BlockSpec((CHUNK_A, e_a), blk),
            pl.BlockSpec((None, None, CHUNK_A, e_a), lambda b, n: (0, b, 0, 0)),
        ],
        out_shape=[
            jax.ShapeDtypeStruct((batch * seq, e_a), BF16),
            jax.ShapeDtypeStruct((1, batch, CHUNK_A, e_a), F32),
        ],
        scratch_shapes=[pltpu.VMEM((n_groups, CHUNK_A, CHUNK_A), BF16)],
        compiler_params=_params(("arbitrary", "arbitrary")),
        name="spatial_gate_prompt",
    )(uv, uv, sza, ln_g, ln_b, w_s, bs_rows)


def _gate_sample_kernel(u_ref, gv_ref, sza_ref, lng_ref, lnb_ref, wd_ref, b0_ref,
                        ya_ref, vr_ref):
    vn = _layernorm(gv_ref[...].astype(F32), lng_ref[...], lnb_ref[...])
    vr_ref[...] = vn
    mixed = wd_ref[...] * vn + b0_ref[...]
    ya_ref[...] = (u_ref[...].astype(F32) * mixed * sza_ref[...].astype(F32)).astype(ya_ref.dtype)


def _gate_sample(uv, sza, ln_g, ln_b, wd_row, b0_row):
    m, e_a = sza.shape
    full = lambda i: (0, 0)
    return pl.pallas_call(
        _gate_sample_kernel,
        grid=(1,),
        in_specs=[
            pl.BlockSpec((m, e_a), full),
            pl.BlockSpec((m, e_a), lambda i: (0, 1)),
            pl.BlockSpec((m, e_a), full),
            pl.BlockSpec((1, e_a), full),
            pl.BlockSpec((1, e_a), full),
            pl.BlockSpec((1, e_a), full),
            pl.BlockSpec((1, e_a), full),
        ],
        out_specs=[pl.BlockSpec((m, e_a), full), pl.BlockSpec((m, e_a), full)],
        out_shape=[jax.ShapeDtypeStruct((m, e_a), BF16),
                   jax.ShapeDtypeStruct((m, e_a), F32)],
        compiler_params=_params(("arbitrary",)),
        name="spatial_gate_sample",
    )(uv, uv, sza, ln_g, ln_b, wd_row, b0_row)


def _dot_nt(a, b):
    return lax.dot_general(a, b, (((1,), (1,)), ((), ())), preferred_element_type=F32)


def _dot_tn(a, b):
    return lax.dot_general(a, b, (((0,), (0,)), ((), ())), preferred_element_type=F32)


def _hgrn_kernel(q_ref, lf_ref, k_ref, i_ref, szb_ref, gon_ref, yb_ref, st_ref,
                 s_scr, o_scr, p_scr, *, n_chunks, n_heads):
    n = pl.program_id(1)
    c, hd = CHUNK_B, HEAD_DIM

    @pl.when(n == 0)
    def _():
        s_scr[...] = jnp.zeros_like(s_scr)

    rows = lax.broadcasted_iota(jnp.int32, (c, hd), 0)
    t_idx = lax.broadcasted_iota(jnp.int32, (c, c), 0)
    s_idx = lax.broadcasted_iota(jnp.int32, (c, c), 1)
    halves = [1 << l for l in range((c // 2).bit_length())]
    second = {h: (rows & (2 * h - 1)) >= h for h in halves}
    att_mask = {}
    for l, h in enumerate(halves):
        same = (t_idx >> (l + 1)) == (s_idx >> (l + 1))
        if h == 1:
            att_mask[h] = same & (s_idx <= t_idx)
        else:
            att_mask[h] = same & ((t_idx & (2 * h - 1)) >= h) & ((s_idx & (2 * h - 1)) < h)

    def boundary(h):
        if 2 * h >= SUBLANES:
            parts = [jnp.broadcast_to(p_scr[pl.ds(blk * 2 * h + h - 1, 1), :], (2 * h, hd))
                     for blk in range(c // (2 * h))]
            return jnp.concatenate(parts, axis=0)
        r8 = rows[:SUBLANES] & (SUBLANES - 1)
        parts = []
        for v in range(c // SUBLANES):
            acc = None
            for blk in range(SUBLANES // (2 * h)):
                row = jnp.broadcast_to(
                    p_scr[pl.ds(v * SUBLANES + blk * 2 * h + h - 1, 1), :], (SUBLANES, hd))
                acc = row if acc is None else jnp.where(r8 >= blk * 2 * h, row, acc)
            parts.append(acc)
        return jnp.concatenate(parts, axis=0)

    def head_body(h, carry):
        lf = lf_ref[h]
        q = q_ref[h]
        k = k_ref[h]
        ib = i_ref[h].astype(BF16)

        odd = second[1]
        qf = jnp.where(odd, q * jnp.exp(lf), q).astype(BF16)
        kf = jnp.where(odd, k * jnp.exp(-lf), k).astype(BF16)
        att = jnp.where(att_mask[1], _dot_nt(qf, kf), 0.0)
        p = lf + jnp.where(odd, pltpu.roll(lf, 1, 0), 0.0)
        for hh in halves[1:]:
            p_scr[...] = p
            bnd = boundary(hh)
            g = jnp.where(second[hh], p, bnd - p)
            x = (jnp.where(second[hh], q, k) * jnp.exp(g)).astype(BF16)
            att = jnp.where(att_mask[hh], _dot_nt(x, x), att)
            p = p + jnp.where(second[hh], bnd, 0.0)
        b = p
        b_last = b[c - 1:c, :]
        qt = (q * jnp.exp(b)).astype(BF16)
        kt = (k * jnp.exp(b_last - b)).astype(BF16)
        dec = jnp.exp(b_last)
        s_old = s_scr[h]
        o = (jnp.dot(att.astype(BF16), ib, preferred_element_type=F32)
             + jnp.dot(qt, s_old.astype(BF16), preferred_element_type=F32))
        dec_col = jnp.transpose(jnp.broadcast_to(dec, (hd, hd)))
        s_scr[h] = dec_col * s_old + _dot_tn(kt, ib)
        ms = jnp.mean(o * o, axis=-1, keepdims=True)
        o_scr[h] = o * lax.rsqrt(ms + EPS)
        return carry

    lax.fori_loop(0, n_heads, head_body, 0)

    for h in range(n_heads):
        sl = slice(h * hd, (h + 1) * hd)
        yb_ref[:, sl] = (o_scr[h] * gon_ref[:, sl]
                         * szb_ref[:, sl].astype(F32)).astype(yb_ref.dtype)

    @pl.when(n == n_chunks - 1)
    def _():
        st_ref[...] = s_scr[...]


def _hgrn_prompt(q, lf, k, iv, szb, g_onorm, batch, seq):
    n_heads = q.shape[0]
    e_b = n_heads * HEAD_DIM
    n_chunks = seq // CHUNK_B
    hm = pl.BlockSpec((n_heads, CHUNK_B, HEAD_DIM), lambda b, n: (0, b * n_chunks + n, 0))
    rm = pl.BlockSpec((CHUNK_B, e_b), lambda b, n: (b * n_chunks + n, 0))
    kern = functools.partial(_hgrn_kernel, n_chunks=n_chunks, n_heads=n_heads)
    return pl.pallas_call(
        kern,
        grid=(batch, n_chunks),
        in_specs=[hm, hm, hm, hm, rm, pl.BlockSpec((1, e_b), lambda b, n: (0, 0))],
        out_specs=[
            rm,
            pl.BlockSpec((None, None, n_heads, HEAD_DIM, HEAD_DIM),
                         lambda b, n: (0, b, 0, 0, 0)),
        ],
        out_shape=[
            jax.ShapeDtypeStruct((batch * seq, e_b), BF16),
            jax.ShapeDtypeStruct((1, batch, n_heads, HEAD_DIM, HEAD_DIM), F32),
        ],
        scratch_shapes=[
            pltpu.VMEM((n_heads, HEAD_DIM, HEAD_DIM), F32),
            pltpu.VMEM((n_heads, CHUNK_B, HEAD_DIM), F32),
            pltpu.VMEM((CHUNK_B, HEAD_DIM), F32),
        ],
        compiler_params=_params(("arbitrary", "arbitrary")),
        name="hgrn_prompt",
    )(q, lf, k, iv, szb, g_onorm)


def _hgrn_sample_kernel(q_ref, lf_ref, k_ref, i_ref, st_ref, szb_ref, gon_ref,
                        sto_ref, yb_ref, ft_scr, kt_scr, qt_scr, o_scr, *, n_heads, nb):
    s = pl.program_id(0)
    hd = HEAD_DIM

    @pl.when(s == 0)
    def _():
        def tb(h, carry):
            ft_scr[h] = jnp.transpose(jnp.exp(lf_ref[h]))
            kt_scr[h] = jnp.transpose(k_ref[h])
            qt_scr[h] = jnp.transpose(q_ref[h])
            return carry
        lax.fori_loop(0, n_heads, tb, 0)

    n_seq = lf_ref.shape[1]
    shift = jnp.where(s == 0, 0, n_seq - s * nb)

    def head_body(h, carry):
        fr = pltpu.roll(ft_scr[h], shift, 1)
        kr = pltpu.roll(kt_scr[h], shift, 1)
        qr = pltpu.roll(qt_scr[h], shift, 1)
        for j in range(nb):
            fb = jnp.broadcast_to(fr[:, j:j + 1], (hd, hd))
            kb = jnp.broadcast_to(kr[:, j:j + 1], (hd, hd))
            qb = jnp.broadcast_to(qr[:, j:j + 1], (hd, hd))
            irow = i_ref[h, pl.ds(s * nb + j, 1), :]
            s_new = fb * st_ref[j, h] + kb * irow
            sto_ref[j, h] = s_new
            o_scr[h, pl.ds(j, 1), :] = jnp.sum(qb * s_new, axis=0, keepdims=True)
        return carry

    lax.fori_loop(0, n_heads, head_body, 0)

    for h in range(n_heads):
        sl = slice(h * hd, (h + 1) * hd)
        o = o_scr[h]
        ms = jnp.mean(o * o, axis=-1, keepdims=True)
        yb_ref[:, sl] = (o * lax.rsqrt(ms + EPS) * gon_ref[:, sl] * szb_ref[:, sl])


def _hgrn_sample(q, lf, k, iv, state, szb, g_onorm):
    n_heads, n_seq, hd = q.shape
    nb = SAMPLE_SEQ_BLOCK
    e_b = n_heads * hd
    full3 = pl.BlockSpec((n_heads, n_seq, hd), lambda s: (0, 0, 0))
    st_spec = pl.BlockSpec((None, nb, n_heads, hd, hd), lambda s: (0, s, 0, 0, 0))
    kern = functools.partial(_hgrn_sample_kernel, n_heads=n_heads, nb=nb)
    return pl.pallas_call(
        kern,
        grid=(n_seq // nb,),
        in_specs=[full3, full3, full3, full3, st_spec,
                  pl.BlockSpec((nb, e_b), lambda s: (s, 0)),
                  pl.BlockSpec((1, e_b), lambda s: (0, 0))],
        out_specs=[st_spec, pl.BlockSpec((nb, e_b), lambda s: (s, 0))],
        out_shape=[jax.ShapeDtypeStruct(state.shape, F32),
                   jax.ShapeDtypeStruct((n_seq, e_b), F32)],
        scratch_shapes=[
            pltpu.VMEM((n_heads, hd, n_seq), F32),
            pltpu.VMEM((n_heads, hd, n_seq), F32),
            pltpu.VMEM((n_heads, hd, n_seq), F32),
            pltpu.VMEM((n_heads, nb, hd), F32),
        ],
        compiler_params=_params(("arbitrary",)),
        name="hgrn_sample",
    )(q, lf, k, iv, state, szb, g_onorm)


def _merge_kernel(yap_ref, ybp_ref, gap_ref, gbp_ref, yas_ref, ybs_ref, gas_ref, gbs_ref,
                  wpa_ref, wpb_ref, mp_ref, ms_ref, wa_bf, wb_bf, *, n_prompt_tiles):
    i = pl.program_id(1)

    @pl.when(i == 0)
    def _():
        wa_bf[...] = wpa_ref[...].astype(BF16)
        wb_bf[...] = wpb_ref[...].astype(BF16)

    def run(ya, yb, ga, gb, out):
        a = jnp.dot(ya[...].astype(BF16), wa_bf[...], preferred_element_type=F32)
        b = jnp.dot(yb[...].astype(BF16), wb_bf[...], preferred_element_type=F32)
        out[...] = (ga[...].astype(F32) * a + gb[...].astype(F32) * b).astype(out.dtype)

    @pl.when(i < n_prompt_tiles)
    def _():
        run(yap_ref, ybp_ref, gap_ref, gbp_ref, mp_ref)

    @pl.when(i == n_prompt_tiles)
    def _():
        run(yas_ref, ybs_ref, gas_ref, gbs_ref, ms_ref)


def _merge(ya_p, yb_p, g_p, ya_s, yb_s, g_s, w_pa, w_pb):
    mp, kdim = ya_p.shape
    ms = ya_s.shape[0]
    d = w_pa.shape[-1]
    tm, tn = ROW_TILE, COL_TILE
    npt = mp // tm
    nj = d // tn

    def row_i(i):
        return jnp.minimum(i, npt - 1)

    yp = pl.BlockSpec((tm, kdim), lambda j, i: (row_i(i), 0))
    ys = pl.BlockSpec((ms, kdim), lambda j, i: (0, 0))
    w = pl.BlockSpec((None, kdim, tn), lambda j, i: (0, 0, j))
    kern = functools.partial(_merge_kernel, n_prompt_tiles=npt)
    return pl.pallas_call(
        kern,
        grid=(nj, npt + 1),
        in_specs=[
            yp, yp,
            pl.BlockSpec((tm, tn), lambda j, i: (row_i(i), j)),
            pl.BlockSpec((tm, tn), lambda j, i: (row_i(i), nj + j)),
            ys, ys,
            pl.BlockSpec((ms, tn), lambda j, i: (0, j)),
            pl.BlockSpec((ms, tn), lambda j, i: (0, nj + j)),
            w, w,
        ],
        out_specs=[pl.BlockSpec((tm, tn), lambda j, i: (row_i(i), j)),
                   pl.BlockSpec((ms, tn), lambda j, i: (0, j))],
        out_shape=[jax.ShapeDtypeStruct((mp, d), BF16),
                   jax.ShapeDtypeStruct((ms, d), BF16)],
        scratch_shapes=[pltpu.VMEM((kdim, tn), BF16), pltpu.VMEM((kdim, tn), BF16)],
        compiler_params=_params(("arbitrary", "arbitrary")),
        name="gated_merge",
    )(ya_p, yb_p, g_p, g_p, ya_s, yb_s, g_s, g_s, w_pa, w_pb)


def _post_kernel(x_ref, z_ref, g_ref, o_ref):
    z = z_ref[...]
    ms = jnp.mean(z * z, axis=-1, keepdims=True)
    o_ref[...] = x_ref[...] + z * lax.rsqrt(ms + EPS) * g_ref[...]


def _post(x, z, g, rows):
    m, d = x.shape
    row = pl.BlockSpec((rows, d), lambda i: (i, 0))
    return pl.pallas_call(
        _post_kernel,
        grid=(m // rows,),
        in_specs=[row, row, pl.BlockSpec((1, d), lambda i: (0, 0))],
        out_specs=row,
        out_shape=jax.ShapeDtypeStruct((m, d), F32),
        compiler_params=_params(("arbitrary",)),
        name="post_norm_residual",
    )(x, z, g)


def kernel(x_prompt, x_sample, state_hgrn, lb_logits, g_pre, w_in, ln_g, ln_b, w_s, b_s,
           g_onorm, w_pa, w_pb, w_o, g_post):
    batch, seq, d = x_prompt.shape
    n_seq, dec_seq, _ = x_sample.shape
    depth = w_in.shape[0]
    assert depth == 1 and dec_seq == 1
    assert seq % CHUNK_A == 0 and seq % CHUNK_B == 0
    e = w_pa.shape[1]
    n_groups = w_s.shape[1]
    assert n_groups * CHUNK_A == e

    xp = x_prompt.reshape(batch * seq, d)
    xs = x_sample.reshape(n_seq, d)
    xn_p = _rmsnorm(xp, g_pre, NORM_ROWS)
    xn_s = _rmsnorm(xs, g_pre, n_seq)

    col = lambda idx: idx * e
    uv_p, uv_s = _proj(xn_p, xn_s, w_in, col(0), 2 * e, _ep_gelu,
                       [(BF16, BF16, False)], name="proj_uv")
    sza_p, sza_s = _proj(xn_p, xn_s, w_in, col(2), e, _ep_silu,
                         [(BF16, BF16, False)], name="proj_za")
    q_p, q_s = _proj(xn_p, xn_s, w_in, col(3), e, _ep_silu,
                     [(F32, F32, True)], name="proj_q")
    lf_p, lf_s, k_p, k_s = _proj(xn_p, xn_s, w_in, col(4), e, _ep_forget,
                                 [(F32, F32, True), (F32, F32, True)],
                                 aux=(lb_logits,), name="proj_f")
    i_p, i_s = _proj(xn_p, xn_s, w_in, col(5), e, _ep_identity,
                     [(F32, F32, True)], name="proj_i")
    szb_p, szb_s = _proj(xn_p, xn_s, w_in, col(6), e, _ep_silu,
                         [(BF16, F32, False)], name="proj_zb")
    g_p, g_s = _proj(xn_p, xn_s, w_in, col(7), 2 * d, _ep_sigmoid,
                     [(BF16, BF16, False)], name="proj_gates")

    bs_rows = jnp.repeat(b_s[0].T, CHUNK_A, axis=1)
    ya_p, vrows_p = _gate_prompt(uv_p, sza_p, ln_g, ln_b, w_s[0], bs_rows, batch, seq)
    wd_row = jnp.repeat(w_s[0, :, 0, 0], CHUNK_A)[None, :]
    b0_row = jnp.repeat(b_s[0, :, 0], CHUNK_A)[None, :]
    ya_s, vrows_s = _gate_sample(uv_s, sza_s, ln_g, ln_b, wd_row, b0_row)

    yb_p, state_p = _hgrn_prompt(q_p, lf_p, k_p, i_p, szb_p, g_onorm, batch, seq)
    state_s, yb_s = _hgrn_sample(q_s, lf_s, k_s, i_s, state_hgrn, szb_s, g_onorm)

    m_p, m_s = _merge(ya_p, yb_p, g_p, ya_s, yb_s, g_s, w_pa, w_pb)
    z_p, z_s = _proj(m_p, m_s, w_o, 0, d, _ep_identity, [(F32, F32, False)], name="proj_out")
    y_p = _post(xp, z_p, g_post, NORM_ROWS)
    y_s = _post(xs, z_s, g_post, n_seq)

    return (y_p.reshape(batch, seq, d), y_s.reshape(n_seq, 1, d),
            state_p, state_s,
            vrows_p, vrows_s.reshape(1, n_seq, 1, e))
```

```python
import functools

import jax
import jax.numpy as jnp
from jax import lax
from jax.experimental import pallas as pl
from jax.experimental.pallas import tpu as pltpu

F32 = jnp.float32
BF16 = jnp.bfloat16

EPS = 1e-6
LANES = 128
SUBLANES = 8
CHUNK_A = 128
HEAD_DIM = 128
CHUNK_B = 128
HEAD_UNROLL = 4
VMEM_LIMIT_BYTES = 56 * 1024 * 1024
ROW_TILE = 1024
COL_TILE = 512
NORM_ROWS = 256
OUT_ROWS = 128
CAST_ROWS = 512
SAMPLE_SEQ_BLOCK = 8


def _params(sem):
    return pltpu.CompilerParams(dimension_semantics=sem,
                                vmem_limit_bytes=VMEM_LIMIT_BYTES)


def _rmsnorm_kernel(x_ref, g_ref, o_ref):
    x = x_ref[...]
    ms = jnp.mean(x * x, axis=-1, keepdims=True)
    o_ref[...] = (x * lax.rsqrt(ms + EPS) * g_ref[...]).astype(o_ref.dtype)


def _rmsnorm(x, g, rows):
    m, d = x.shape
    return pl.pallas_call(
        _rmsnorm_kernel,
        grid=(m // rows,),
        in_specs=[pl.BlockSpec((rows, d), lambda i: (i, 0)),
                  pl.BlockSpec((1, d), lambda i: (0, 0))],
        out_specs=pl.BlockSpec((rows, d), lambda i: (i, 0)),
        out_shape=jax.ShapeDtypeStruct((m, d), BF16),
        compiler_params=_params(("arbitrary",)),
        name="pre_rmsnorm",
    )(x, g)


def _sample_step(n_prompt_tiles):
    return n_prompt_tiles // 2


def _prompt_tile(j, i, n_prompt_tiles):
    fwd = i - (i > _sample_step(n_prompt_tiles)).astype(jnp.int32)
    return jnp.where(j % 2 == 0, fwd, n_prompt_tiles - 1 - fwd)


def _proj_kernel(*refs, epilogue, out_head_major, n_aux, n_prompt_tiles):
    n_out = len(out_head_major)
    xp_ref, xs_ref, w_ref = refs[:3]
    aux_refs = refs[3:3 + n_aux]
    out_refs = refs[3 + n_aux:3 + n_aux + 2 * n_out]
    wbf_ref = refs[-1]
    i = pl.program_id(1)

    @pl.when(i == 0)
    def _():
        wbf_ref[...] = w_ref[...].astype(BF16)

    def run(x_ref, outs):
        acc = jnp.dot(x_ref[...].astype(BF16), wbf_ref[...],
                      preferred_element_type=F32)
        res = epilogue(acc, *[a[...] for a in aux_refs])
        for r, o_ref, hm in zip(res, outs, out_head_major):
            if hm:
                for hh in range(o_ref.shape[0]):
                    o_ref[hh] = r[:, hh * LANES:(hh + 1) * LANES].astype(o_ref.dtype)
            else:
                o_ref[...] = r.astype(o_ref.dtype)

    @pl.when(i != _sample_step(n_prompt_tiles))
    def _():
        run(xp_ref, out_refs[0::2])

    @pl.when(i == _sample_step(n_prompt_tiles))
    def _():
        run(xs_ref, out_refs[1::2])


def _proj(xp, xs, w3, col0, ncols, epilogue, outs, aux=(), name="proj"):
    mp, kdim = xp.shape
    ms = xs.shape[0]
    tm, tn = ROW_TILE, COL_TILE
    npt = mp // tm
    nj = ncols // tn
    c0 = col0 // tn
    hpt = tn // LANES

    def row_i(j, i):
        return _prompt_tile(j, i, npt)

    in_specs = [
        pl.BlockSpec((tm, kdim), lambda j, i: (row_i(j, i), 0)),
        pl.BlockSpec((ms, kdim), lambda j, i: (0, 0)),
        pl.BlockSpec((None, kdim, tn), lambda j, i: (0, 0, c0 + j)),
    ]
    for a in aux:
        in_specs.append(pl.BlockSpec((a.shape[0], tn), lambda j, i: (0, j)))
    out_specs, out_shapes = [], []
    for pdt, sdt, hm in outs:
        if hm:
            out_specs.append(pl.BlockSpec((hpt, tm, LANES), lambda j, i: (j, row_i(j, i), 0)))
            out_shapes.append(jax.ShapeDtypeStruct((ncols // LANES, mp, LANES), pdt))
            out_specs.append(pl.BlockSpec((hpt, ms, LANES), lambda j, i: (j, 0, 0)))
            out_shapes.append(jax.ShapeDtypeStruct((ncols // LANES, ms, LANES), sdt))
        else:
            out_specs.append(pl.BlockSpec((tm, tn), lambda j, i: (row_i(j, i), j)))
            out_shapes.append(jax.ShapeDtypeStruct((mp, ncols), pdt))
            out_specs.append(pl.BlockSpec((ms, tn), lambda j, i: (0, j)))
            out_shapes.append(jax.ShapeDtypeStruct((ms, ncols), sdt))
    kern = functools.partial(
        _proj_kernel, epilogue=epilogue,
        out_head_major=tuple(hm for _, _, hm in outs),
        n_aux=len(aux), n_prompt_tiles=npt)
    return pl.pallas_call(
        kern,
        grid=(nj, npt + 1),
        in_specs=in_specs,
        out_specs=out_specs,
        out_shape=out_shapes,
        scratch_shapes=[pltpu.VMEM((kdim, tn), BF16)],
        compiler_params=_params(("arbitrary", "arbitrary")),
        name=name,
    )(xp, xs, w3, *aux)


def _ep_gelu(acc):
    return (jax.nn.gelu(acc, approximate=True),)


def _ep_silu(acc):
    return (acc * jax.nn.sigmoid(acc),)


def _ep_sigmoid(acc):
    return (jax.nn.sigmoid(acc),)


def _ep_identity(acc):
    return (acc,)


def _ep_forget(acc, lbl):
    m = jnp.max(lbl, axis=0, keepdims=True)
    e = jnp.exp(lbl - m)
    lb = e[0:1, :] / jnp.sum(e, axis=0, keepdims=True)
    log_f = jnp.log(lb + (1.0 - lb) * jax.nn.sigmoid(acc))
    k = (1.0 - lb) * jax.nn.sigmoid(-acc)
    return log_f, k


def _layernorm(gv, g, b):
    mu = jnp.mean(gv, axis=-1, keepdims=True)
    xc = gv - mu
    var = jnp.mean(xc * xc, axis=-1, keepdims=True)
    return xc * lax.rsqrt(var + EPS) * g + b


def _gate_kernel(u_ref, gv_ref, sza_ref, lng_ref, lnb_ref, ws_ref, bs_ref,
                 ya_ref, vr_ref, wm_ref, *, n_chunks, n_groups):
    n = pl.program_id(1)

    @pl.when((pl.program_id(0) == 0) & (n == 0))
    def _():
        r = lax.broadcasted_iota(jnp.int32, (CHUNK_A, CHUNK_A), 0)
        c = lax.broadcasted_iota(jnp.int32, (CHUNK_A, CHUNK_A), 1)
        for g in range(n_groups):
            wm_ref[g] = jnp.where(c <= r, ws_ref[g], 0.0).astype(BF16)

    vn = _layernorm(gv_ref[...].astype(F32), lng_ref[...], lnb_ref[...])

    @pl.when(n == n_chunks - 1)
    def _():
        vr_ref[...] = vn

    vnb = vn.astype(BF16)
    for g in range(n_groups):
        sl = slice(g * LANES, (g + 1) * LANES)
        mixed = jnp.dot(wm_ref[g], vnb[:, sl], preferred_element_type=F32) + bs_ref[:, sl]
        ya_ref[:, sl] = (u_ref[:, sl].astype(F32) * mixed
                         * sza_ref[:, sl].astype(F32)).astype(ya_ref.dtype)


def _gate_prompt(uv, sza, ln_g, ln_b, w_s, bs_rows, batch, seq):
    e_a = sza.shape[1]
    n_chunks = seq // CHUNK_A
    n_groups = w_s.shape[0]
    blk = lambda b, n: (b * n_chunks + n, 0)
    kern = functools.partial(_gate_kernel, n_chunks=n_chunks, n_groups=n_groups)
    return pl.pallas_call(
        kern,
        grid=(batch, n_chunks),
        in_specs=[
            pl.BlockSpec((CHUNK_A, e_a), blk),
            pl.BlockSpec((CHUNK_A, e_a), lambda b, n: (b * n_chunks + n, 1)),
            pl.BlockSpec((CHUNK_A, e_a), blk),
            pl.BlockSpec((1, e_a), lambda b, n: (0, 0)),
            pl.BlockSpec((1, e_a), lambda b, n: (0, 0)),
            pl.BlockSpec(w_s.shape, lambda b, n: (0, 0, 0)),
            pl.BlockSpec((CHUNK_A, e_a), lambda b, n: (0, 0)),
        ],
        out_specs=[
            pl.BlockSpec((CHUNK_A, e_a), blk),
            pl.BlockSpec((None, None, CHUNK_A, e_a), lambda b, n: (0, b, 0, 0)),
        ],
        out_shape=[
            jax.ShapeDtypeStruct((batch * seq, e_a), BF16),
            jax.ShapeDtypeStruct((1, batch, CHUNK_A, e_a), F32),
        ],
        scratch_shapes=[pltpu.VMEM((n_groups, CHUNK_A, CHUNK_A), BF16)],
        compiler_params=_params(("arbitrary", "arbitrary")),
        name="spatial_gate_prompt",
    )(uv, uv, sza, ln_g, ln_b, w_s, bs_rows)


def _gate_sample_kernel(u_ref, gv_ref, sza_ref, lng_ref, lnb_ref, wd_ref, b0_ref,
                        ya_ref, vr_ref):
    vn = _layernorm(gv_ref[...].astype(F32), lng_ref[...], lnb_ref[...])
    vr_ref[...] = vn
    mixed = wd_ref[...] * vn + b0_ref[...]
    ya_ref[...] = (u_ref[...].astype(F32) * mixed * sza_ref[...].astype(F32)).astype(ya_ref.dtype)


def _gate_sample(uv, sza, ln_g, ln_b, wd_row, b0_row):
    m, e_a = sza.shape
    full = lambda i: (0, 0)
    return pl.pallas_call(
        _gate_sample_kernel,
        grid=(1,),
        in_specs=[
            pl.BlockSpec((m, e_a), full),
            pl.BlockSpec((m, e_a), lambda i: (0, 1)),
            pl.BlockSpec((m, e_a), full),
            pl.BlockSpec((1, e_a), full),
            pl.BlockSpec((1, e_a), full),
            pl.BlockSpec((1, e_a), full),
            pl.BlockSpec((1, e_a), full),
        ],
        out_specs=[pl.BlockSpec((m, e_a), full), pl.BlockSpec((m, e_a), full)],
        out_shape=[jax.ShapeDtypeStruct((m, e_a), BF16),
                   jax.ShapeDtypeStruct((m, e_a), F32)],
        compiler_params=_params(("arbitrary",)),
        name="spatial_gate_sample",
    )(uv, uv, sza, ln_g, ln_b, wd_row, b0_row)


def _dot_nt(a, b):
    return lax.dot_general(a, b, (((1,), (1,)), ((), ())), preferred_element_type=F32)


def _dot_tn(a, b):
    return lax.dot_general(a, b, (((0,), (0,)), ((), ())), preferred_element_type=F32)


def _hgrn_kernel(q_ref, lf_ref, k_ref, i_ref, szb_ref, gon_ref, yb_ref, st_ref,
                 s_scr, o_scr, p_scr, *, n_chunks, n_heads):
    n = pl.program_id(1)
    c, hd = CHUNK_B, HEAD_DIM

    @pl.when(n == 0)
    def _():
        s_scr[...] = jnp.zeros_like(s_scr)

    rows = lax.broadcasted_iota(jnp.int32, (c, hd), 0)
    t_idx = lax.broadcasted_iota(jnp.int32, (c, c), 0)
    s_idx = lax.broadcasted_iota(jnp.int32, (c, c), 1)
    halves = [1 << l for l in range((c // 2).bit_length())]
    second = {h: (rows & (2 * h - 1)) >= h for h in halves}
    att_mask = {}
    for l, h in enumerate(halves):
        same = (t_idx >> (l + 1)) == (s_idx >> (l + 1))
        if h == 1:
            att_mask[h] = same & (s_idx <= t_idx)
        else:
            att_mask[h] = same & ((t_idx & (2 * h - 1)) >= h) & ((s_idx & (2 * h - 1)) < h)

    def boundary(p_ref, h):
        if 2 * h >= SUBLANES:
            parts = [jnp.broadcast_to(p_ref[pl.ds(blk * 2 * h + h - 1, 1), :], (2 * h, hd))
                     for blk in range(c // (2 * h))]
            return jnp.concatenate(parts, axis=0)
        r8 = rows[:SUBLANES] & (SUBLANES - 1)
        parts = []
        for v in range(c // SUBLANES):
            acc = None
            for blk in range(SUBLANES // (2 * h)):
                row = jnp.broadcast_to(
                    p_ref[pl.ds(v * SUBLANES + blk * 2 * h + h - 1, 1), :], (SUBLANES, hd))
                acc = row if acc is None else jnp.where(r8 >= blk * 2 * h, row, acc)
            parts.append(acc)
        return jnp.concatenate(parts, axis=0)

    def one_head(h, p_ref):
        lf = lf_ref[h]
        q = q_ref[h]
        k = k_ref[h]
        ib = i_ref[h].astype(BF16)

        odd = second[1]
        qf = jnp.where(odd, q * jnp.exp(lf), q).astype(BF16)
        kf = jnp.where(odd, k * jnp.exp(-lf), k).astype(BF16)
        att = jnp.where(att_mask[1], _dot_nt(qf, kf), 0.0)
        p = lf + jnp.where(odd, pltpu.roll(lf, 1, 0), 0.0)
        for hh in halves[1:]:
            p_ref[...] = p
            bnd = boundary(p_ref, hh)
            g = jnp.where(second[hh], p, bnd - p)
            x = (jnp.where(second[hh], q, k) * jnp.exp(g)).astype(BF16)
            att = jnp.where(att_mask[hh], _dot_nt(x, x), att)
            p = p + jnp.where(second[hh], bnd, 0.0)
        b = p
        b_last = b[c - 1:c, :]
        qt = (q * jnp.exp(b)).astype(BF16)
        kt = (k * jnp.exp(b_last - b)).astype(BF16)
        dec = jnp.exp(b_last)
        s_old = s_scr[h]
        o = (jnp.dot(att.astype(BF16), ib, preferred_element_type=F32)
             + jnp.dot(qt, s_old.astype(BF16), preferred_element_type=F32))
        dec_col = jnp.transpose(jnp.broadcast_to(dec, (hd, hd)))
        s_scr[h] = dec_col * s_old + _dot_tn(kt, ib)
        ms = jnp.mean(o * o, axis=-1, keepdims=True)
        o_scr[h] = o * lax.rsqrt(ms + EPS)

    def head_group(grp, carry):
        for u in range(HEAD_UNROLL):
            one_head(grp * HEAD_UNROLL + u, p_scr.at[u])
        return carry

    lax.fori_loop(0, n_heads // HEAD_UNROLL, head_group, 0)

    for h in range(n_heads):
        sl = slice(h * hd, (h + 1) * hd)
        yb_ref[:, sl] = (o_scr[h] * gon_ref[:, sl]
                         * szb_ref[:, sl].astype(F32)).astype(yb_ref.dtype)

    @pl.when(n == n_chunks - 1)
    def _():
        st_ref[...] = s_scr[...]


def _hgrn_prompt(q, lf, k, iv, szb, g_onorm, batch, seq):
    n_heads = q.shape[0]
    e_b = n_heads * HEAD_DIM
    n_chunks = seq // CHUNK_B
    hm = pl.BlockSpec((n_heads, CHUNK_B, HEAD_DIM), lambda b, n: (0, b * n_chunks + n, 0))
    rm = pl.BlockSpec((CHUNK_B, e_b), lambda b, n: (b * n_chunks + n, 0))
    kern = functools.partial(_hgrn_kernel, n_chunks=n_chunks, n_heads=n_heads)
    return pl.pallas_call(
        kern,
        grid=(batch, n_chunks),
        in_specs=[hm, hm, hm, hm, rm, pl.BlockSpec((1, e_b), lambda b, n: (0, 0))],
        out_specs=[
            rm,
            pl.BlockSpec((None, None, n_heads, HEAD_DIM, HEAD_DIM),
                         lambda b, n: (0, b, 0, 0, 0)),
        ],
        out_shape=[
            jax.ShapeDtypeStruct((batch * seq, e_b), BF16),
            jax.ShapeDtypeStruct((1, batch, n_heads, HEAD_DIM, HEAD_DIM), F32),
        ],
        scratch_shapes=[
            pltpu.VMEM((n_heads, HEAD_DIM, HEAD_DIM), F32),
            pltpu.VMEM((n_heads, CHUNK_B, HEAD_DIM), F32),
            pltpu.VMEM((HEAD_UNROLL, CHUNK_B, HEAD_DIM), F32),
        ],
        compiler_params=_params(("arbitrary", "arbitrary")),
        name="hgrn_prompt",
    )(q, lf, k, iv, szb, g_onorm)


def _hgrn_sample_kernel(q_ref, lf_ref, k_ref, i_ref, st_ref, szb_ref, gon_ref,
                        sto_ref, yb_ref, ft_scr, kt_scr, qt_scr, o_scr, *, n_heads, nb):
    s = pl.program_id(0)
    hd = HEAD_DIM

    @pl.when(s == 0)
    def _():
        def tb(h, carry):
            ft_scr[h] = jnp.transpose(jnp.exp(lf_ref[h]))
            kt_scr[h] = jnp.transpose(k_ref[h])
            qt_scr[h] = jnp.transpose(q_ref[h])
            return carry
        lax.fori_loop(0, n_heads, tb, 0)

    n_seq = lf_ref.shape[1]
    shift = jnp.where(s == 0, 0, n_seq - s * nb)

    def head_body(h, carry):
        fr = pltpu.roll(ft_scr[h], shift, 1)
        kr = pltpu.roll(kt_scr[h], shift, 1)
        qr = pltpu.roll(qt_scr[h], shift, 1)
        for j in range(nb):
            fb = jnp.broadcast_to(fr[:, j:j + 1], (hd, hd))
            kb = jnp.broadcast_to(kr[:, j:j + 1], (hd, hd))
            qb = jnp.broadcast_to(qr[:, j:j + 1], (hd, hd))
            irow = i_ref[h, pl.ds(s * nb + j, 1), :]
            s_new = fb * st_ref[j, h] + kb * irow
            sto_ref[j, h] = s_new
            o_scr[h, pl.ds(j, 1), :] = jnp.sum(qb * s_new, axis=0, keepdims=True)
        return carry

    lax.fori_loop(0, n_heads, head_body, 0)

    for h in range(n_heads):
        sl = slice(h * hd, (h + 1) * hd)
        o = o_scr[h]
        ms = jnp.mean(o * o, axis=-1, keepdims=True)
        yb_ref[:, sl] = (o * lax.rsqrt(ms + EPS) * gon_ref[:, sl] * szb_ref[:, sl])


def _hgrn_sample(q, lf, k, iv, state, szb, g_onorm):
    n_heads, n_seq, hd = q.shape
    nb = SAMPLE_SEQ_BLOCK
    e_b = n_heads * hd
    full3 = pl.BlockSpec((n_heads, n_seq, hd), lambda s: (0, 0, 0))
    st_spec = pl.BlockSpec((None, nb, n_heads, hd, hd), lambda s: (0, s, 0, 0, 0))
    kern = functools.partial(_hgrn_sample_kernel, n_heads=n_heads, nb=nb)
    return pl.pallas_call(
        kern,
        grid=(n_seq // nb,),
        in_specs=[full3, full3, full3, full3, st_spec,
                  pl.BlockSpec((nb, e_b), lambda s: (s, 0)),
                  pl.BlockSpec((1, e_b), lambda s: (0, 0))],
        out_specs=[st_spec, pl.BlockSpec((nb, e_b), lambda s: (s, 0))],
        out_shape=[jax.ShapeDtypeStruct(state.shape, F32),
                   jax.ShapeDtypeStruct((n_seq, e_b), F32)],
        scratch_shapes=[
            pltpu.VMEM((n_heads, hd, n_seq), F32),
            pltpu.VMEM((n_heads, hd, n_seq), F32),
            pltpu.VMEM((n_heads, hd, n_seq), F32),
            pltpu.VMEM((n_heads, nb, hd), F32),
        ],
        compiler_params=_params(("arbitrary",)),
        name="hgrn_sample",
    )(q, lf, k, iv, state, szb, g_onorm)


def _merge_kernel(yap_ref, ybp_ref, gap_ref, gbp_ref, yas_ref, ybs_ref, gas_ref, gbs_ref,
                  wpa_ref, wpb_ref, mp_ref, ms_ref, wa_bf, wb_bf, *, n_prompt_tiles):
    i = pl.program_id(1)

    @pl.when(i == 0)
    def _():
        wa_bf[...] = wpa_ref[...].astype(BF16)
        wb_bf[...] = wpb_ref[...].astype(BF16)

    def run(ya, yb, ga, gb, out):
        a = jnp.dot(ya[...].astype(BF16), wa_bf[...], preferred_element_type=F32)
        b = jnp.dot(yb[...].astype(BF16), wb_bf[...], preferred_element_type=F32)
        out[...] = (ga[...].astype(F32) * a + gb[...].astype(F32) * b).astype(out.dtype)

    @pl.when(i != _sample_step(n_prompt_tiles))
    def _():
        run(yap_ref, ybp_ref, gap_ref, gbp_ref, mp_ref)

    @pl.when(i == _sample_step(n_prompt_tiles))
    def _():
        run(yas_ref, ybs_ref, gas_ref, gbs_ref, ms_ref)


def _merge(ya_p, yb_p, g_p, ya_s, yb_s, g_s, w_pa, w_pb):
    mp, kdim = ya_p.shape
    ms = ya_s.shape[0]
    d = w_pa.shape[-1]
    tm, tn = ROW_TILE, COL_TILE
    npt = mp // tm
    nj = d // tn

    def row_i(j, i):
        return _prompt_tile(j, i, npt)

    yp = pl.BlockSpec((tm, kdim), lambda j, i: (row_i(j, i), 0))
    ys = pl.BlockSpec((ms, kdim), lambda j, i: (0, 0))
    w = pl.BlockSpec((None, kdim, tn), lambda j, i: (0, 0, j))
    kern = functools.partial(_merge_kernel, n_prompt_tiles=npt)
    return pl.pallas_call(
        kern,
        grid=(nj, npt + 1),
        in_specs=[
            yp, yp,
            pl.BlockSpec((tm, tn), lambda j, i: (row_i(j, i), j)),
            pl.BlockSpec((tm, tn), lambda j, i: (row_i(j, i), nj + j)),
            ys, ys,
            pl.BlockSpec((ms, tn), lambda j, i: (0, j)),
            pl.BlockSpec((ms, tn), lambda j, i: (0, nj + j)),
            w, w,
        ],
        out_specs=[pl.BlockSpec((tm, tn), lambda j, i: (row_i(j, i), j)),
                   pl.BlockSpec((ms, tn), lambda j, i: (0, j))],
        out_shape=[jax.ShapeDtypeStruct((mp, d), BF16),
                   jax.ShapeDtypeStruct((ms, d), BF16)],
        scratch_shapes=[pltpu.VMEM((kdim, tn), BF16), pltpu.VMEM((kdim, tn), BF16)],
        compiler_params=_params(("arbitrary", "arbitrary")),
        name="gated_merge",
    )(ya_p, yb_p, g_p, g_p, ya_s, yb_s, g_s, g_s, w_pa, w_pb)


def _cast_kernel(w_ref, o_ref):
    o_ref[...] = w_ref[...].astype(o_ref.dtype)


def _cast_bf16(w3, rows):
    _, k, n = w3.shape
    return pl.pallas_call(
        _cast_kernel,
        grid=(k // rows,),
        in_specs=[pl.BlockSpec((None, rows, n), lambda i: (0, i, 0))],
        out_specs=pl.BlockSpec((rows, n), lambda i: (i, 0)),
        out_shape=jax.ShapeDtypeStruct((k, n), BF16),
        compiler_params=_params(("arbitrary",)),
        name="cast_w_out",
    )(w3)


def _out_post_kernel(mp_ref, xp_ref, ms_ref, xs_ref, w_ref, g_ref, op_ref, os_ref,
                     *, n_prompt_tiles):
    i = pl.program_id(0)

    def run(m_ref, x_ref, o_ref):
        z = jnp.dot(m_ref[...], w_ref[...], preferred_element_type=F32)
        ms = jnp.mean(z * z, axis=-1, keepdims=True)
        o_ref[...] = x_ref[...] + z * lax.rsqrt(ms + EPS) * g_ref[...]

    @pl.when(i < n_prompt_tiles)
    def _():
        run(mp_ref, xp_ref, op_ref)

    @pl.when(i == n_prompt_tiles)
    def _():
        run(ms_ref, xs_ref, os_ref)


def _out_post(m_p, x_p, m_s, x_s, w_bf, g):
    mp, d = x_p.shape
    ms = x_s.shape[0]
    tm = OUT_ROWS
    npt = mp // tm
    row = pl.BlockSpec((tm, d), lambda i: (jnp.minimum(i, npt - 1), 0))
    full_s = pl.BlockSpec((ms, d), lambda i: (0, 0))
    kern = functools.partial(_out_post_kernel, n_prompt_tiles=npt)
    return pl.pallas_call(
        kern,
        grid=(npt + 1,),
        in_specs=[row, row, full_s, full_s,
                  pl.BlockSpec(w_bf.shape, lambda i: (0, 0)),
                  pl.BlockSpec((1, d), lambda i: (0, 0))],
        out_specs=[row, full_s],
        out_shape=[jax.ShapeDtypeStruct((mp, d), F32), jax.ShapeDtypeStruct((ms, d), F32)],
        compiler_params=_params(("arbitrary",)),
        name="out_proj_post_norm",
    )(m_p, x_p, m_s, x_s, w_bf, g)


def kernel(x_prompt, x_sample, state_hgrn, lb_logits, g_pre, w_in, ln_g, ln_b, w_s, b_s,
           g_onorm, w_pa, w_pb, w_o, g_post):
    batch, seq, d = x_prompt.shape
    n_seq, dec_seq, _ = x_sample.shape
    depth = w_in.shape[0]
    assert depth == 1 and dec_seq == 1
    assert seq % CHUNK_A == 0 and seq % CHUNK_B == 0
    e = w_pa.shape[1]
    n_groups = w_s.shape[1]
    assert n_groups * CHUNK_A == e

    xp = x_prompt.reshape(batch * seq, d)
    xs = x_sample.reshape(n_seq, d)
    xn_p = _rmsnorm(xp, g_pre, NORM_ROWS)
    xn_s = _rmsnorm(xs, g_pre, n_seq)

    col = lambda idx: idx * e
    uv_p, uv_s = _proj(xn_p, xn_s, w_in, col(0), 2 * e, _ep_gelu,
                       [(BF16, BF16, False)], name="proj_uv")
    sza_p, sza_s = _proj(xn_p, xn_s, w_in, col(2), e, _ep_silu,
                         [(BF16, BF16, False)], name="proj_za")
    q_p, q_s = _proj(xn_p, xn_s, w_in, col(3), e, _ep_silu,
                     [(F32, F32, True)], name="proj_q")
    lf_p, lf_s, k_p, k_s = _proj(xn_p, xn_s, w_in, col(4), e, _ep_forget,
                                 [(F32, F32, True), (F32, F32, True)],
                                 aux=(lb_logits,), name="proj_f")
    i_p, i_s = _proj(xn_p, xn_s, w_in, col(5), e, _ep_identity,
                     [(F32, F32, True)], name="proj_i")
    szb_p, szb_s = _proj(xn_p, xn_s, w_in, col(6), e, _ep_silu,
                         [(BF16, F32, False)], name="proj_zb")
    g_p, g_s = _proj(xn_p, xn_s, w_in, col(7), 2 * d, _ep_sigmoid,
                     [(BF16, BF16, False)], name="proj_gates")

    bs_rows = jnp.repeat(b_s[0].T, CHUNK_A, axis=1)
    ya_p, vrows_p = _gate_prompt(uv_p, sza_p, ln_g, ln_b, w_s[0], bs_rows, batch, seq)
    wd_row = jnp.repeat(w_s[0, :, 0, 0], CHUNK_A)[None, :]
    b0_row = jnp.repeat(b_s[0, :, 0], CHUNK_A)[None, :]
    ya_s, vrows_s = _gate_sample(uv_s, sza_s, ln_g, ln_b, wd_row, b0_row)

    yb_p, state_p = _hgrn_prompt(q_p, lf_p, k_p, i_p, szb_p, g_onorm, batch, seq)
    state_s, yb_s = _hgrn_sample(q_s, lf_s, k_s, i_s, state_hgrn, szb_s, g_onorm)

    m_p, m_s = _merge(ya_p, yb_p, g_p, ya_s, yb_s, g_s, w_pa, w_pb)
    w_o_bf = _cast_bf16(w_o, CAST_ROWS)
    y_p, y_s = _out_post(m_p, xp, m_s, xs, w_o_bf, g_post)

    return (y_p.reshape(batch, seq, d), y_s.reshape(n_seq, 1, d),
            state_p, state_s,
            vrows_p, vrows_s.reshape(1, n_seq, 1, e))
```

```python
import functools
import itertools

import jax
import jax.numpy as jnp
from jax import lax
from jax.experimental import pallas as pl
from jax.experimental.pallas import tpu as pltpu

F32 = jnp.float32
BF16 = jnp.bfloat16

EPS = 1e-6
LOG2_E = 1.4426950408889634
LANES = 128
SUBLANES = 8
CHUNK_A = 128
HEAD_DIM = 128
CHUNK_B = 128
HEAD_UNROLL = 8
VMEM_LIMIT_BYTES = 56 * 1024 * 1024
ROW_TILE = 1024
COL_TILE = 512
SUB_ROWS = 256
NORM_ROWS = 256
OUT_ROWS = 128
CAST_ROWS = 512
SAMPLE_SEQ_BLOCK = 8
SAMPLE_HEAD_UNROLL = 4


def _params(sem):
    return pltpu.CompilerParams(dimension_semantics=sem,
                                vmem_limit_bytes=VMEM_LIMIT_BYTES)


def _rmsnorm_kernel(x_ref, g_ref, o_ref):
    x = x_ref[...]
    ms = jnp.mean(x * x, axis=-1, keepdims=True)
    o_ref[...] = (x * lax.rsqrt(ms + EPS) * g_ref[...]).astype(o_ref.dtype)


def _rmsnorm(x, g, rows):
    m, d = x.shape
    return pl.pallas_call(
        _rmsnorm_kernel,
        grid=(m // rows,),
        in_specs=[pl.BlockSpec((rows, d), lambda i: (i, 0)),
                  pl.BlockSpec((1, d), lambda i: (0, 0))],
        out_specs=pl.BlockSpec((rows, d), lambda i: (i, 0)),
        out_shape=jax.ShapeDtypeStruct((m, d), BF16),
        compiler_params=_params(("arbitrary",)),
        name="pre_rmsnorm",
    )(x, g)


def _sample_step(n_prompt_tiles):
    return n_prompt_tiles // 2


def _prompt_tile(j, i, n_prompt_tiles):
    fwd = i - (i > _sample_step(n_prompt_tiles)).astype(jnp.int32)
    return jnp.where(j % 2 == 0, fwd, n_prompt_tiles - 1 - fwd)


def _proj_kernel(*refs, epilogue, out_head_major, n_aux, n_prompt_tiles):
    n_out = len(out_head_major)
    xp_ref, xs_ref, w_ref = refs[:3]
    aux_refs = refs[3:3 + n_aux]
    out_refs = refs[3 + n_aux:3 + n_aux + 2 * n_out]
    wbf_ref = refs[-1]
    i = pl.program_id(1)

    @pl.when(i == 0)
    def _():
        wbf_ref[...] = w_ref[...].astype(BF16)

    def run(x_ref, outs):
        rows = x_ref.shape[0]
        sub = min(rows, SUB_ROWS)
        for m in range(rows // sub):
            rs = slice(m * sub, (m + 1) * sub)
            acc = jnp.dot(x_ref[rs, :].astype(BF16), wbf_ref[...],
                          preferred_element_type=F32)
            res = epilogue(acc, *[a[...] for a in aux_refs])
            for r, o_ref, hm in zip(res, outs, out_head_major):
                if hm:
                    for hh in range(o_ref.shape[0]):
                        o_ref[hh, rs, :] = r[:, hh * LANES:(hh + 1) * LANES].astype(o_ref.dtype)
                else:
                    o_ref[rs, :] = r.astype(o_ref.dtype)

    @pl.when(i != _sample_step(n_prompt_tiles))
    def _():
        run(xp_ref, out_refs[0::2])

    @pl.when(i == _sample_step(n_prompt_tiles))
    def _():
        run(xs_ref, out_refs[1::2])


def _proj(xp, xs, w3, col0, ncols, epilogue, outs, aux=(), name="proj"):
    mp, kdim = xp.shape
    ms = xs.shape[0]
    tm, tn = ROW_TILE, COL_TILE
    npt = mp // tm
    nj = ncols // tn
    c0 = col0 // tn
    hpt = tn // LANES

    def row_i(j, i):
        return _prompt_tile(j, i, npt)

    in_specs = [
        pl.BlockSpec((tm, kdim), lambda j, i: (row_i(j, i), 0)),
        pl.BlockSpec((ms, kdim), lambda j, i: (0, 0)),
        pl.BlockSpec((None, kdim, tn), lambda j, i: (0, 0, c0 + j)),
    ]
    for a in aux:
        in_specs.append(pl.BlockSpec((a.shape[0], tn), lambda j, i: (0, j)))
    out_specs, out_shapes = [], []
    for pdt, sdt, hm in outs:
        if hm:
            out_specs.append(pl.BlockSpec((hpt, tm, LANES), lambda j, i: (j, row_i(j, i), 0)))
            out_shapes.append(jax.ShapeDtypeStruct((ncols // LANES, mp, LANES), pdt))
            out_specs.append(pl.BlockSpec((hpt, ms, LANES), lambda j, i: (j, 0, 0)))
            out_shapes.append(jax.ShapeDtypeStruct((ncols // LANES, ms, LANES), sdt))
        else:
            out_specs.append(pl.BlockSpec((tm, tn), lambda j, i: (row_i(j, i), j)))
            out_shapes.append(jax.ShapeDtypeStruct((mp, ncols), pdt))
            out_specs.append(pl.BlockSpec((ms, tn), lambda j, i: (0, j)))
            out_shapes.append(jax.ShapeDtypeStruct((ms, ncols), sdt))
    kern = functools.partial(
        _proj_kernel, epilogue=epilogue,
        out_head_major=tuple(hm for _, _, hm in outs),
        n_aux=len(aux), n_prompt_tiles=npt)
    return pl.pallas_call(
        kern,
        grid=(nj, npt + 1),
        in_specs=in_specs,
        out_specs=out_specs,
        out_shape=out_shapes,
        scratch_shapes=[pltpu.VMEM((kdim, tn), BF16)],
        compiler_params=_params(("arbitrary", "arbitrary")),
        name=name,
    )(xp, xs, w3, *aux)


def _ep_gelu(acc):
    return (jax.nn.gelu(acc, approximate=True),)


def _ep_silu(acc):
    return (acc * jax.nn.sigmoid(acc),)


def _ep_sigmoid(acc):
    return (jax.nn.sigmoid(acc),)


def _ep_identity(acc):
    return (acc,)


def _ep_forget(acc, lbl):
    m = jnp.max(lbl, axis=0, keepdims=True)
    e = jnp.exp(lbl - m)
    lb = e[0:1, :] / jnp.sum(e, axis=0, keepdims=True)
    log2_f = jnp.log(lb + (1.0 - lb) * jax.nn.sigmoid(acc)) * LOG2_E
    k = (1.0 - lb) * jax.nn.sigmoid(-acc)
    return log2_f, k


def _layernorm(gv, g, b):
    mu = jnp.mean(gv, axis=-1, keepdims=True)
    xc = gv - mu
    var = jnp.mean(xc * xc, axis=-1, keepdims=True)
    return xc * lax.rsqrt(var + EPS) * g + b


def _gate_kernel(u_ref, gv_ref, sza_ref, lng_ref, lnb_ref, ws_ref, bs_ref,
                 ya_ref, vr_ref, wm_ref, *, n_chunks, n_groups):
    n = pl.program_id(1)

    @pl.when((pl.program_id(0) == 0) & (n == 0))
    def _():
        r = lax.broadcasted_iota(jnp.int32, (CHUNK_A, CHUNK_A), 0)
        c = lax.broadcasted_iota(jnp.int32, (CHUNK_A, CHUNK_A), 1)
        for g in range(n_groups):
            wm_ref[g] = jnp.where(c <= r, ws_ref[g], 0.0).astype(BF16)

    vn = _layernorm(gv_ref[...].astype(F32), lng_ref[...], lnb_ref[...])

    @pl.when(n == n_chunks - 1)
    def _():
        vr_ref[...] = vn

    vnb = vn.astype(BF16)
    for g in range(n_groups):
        sl = slice(g * LANES, (g + 1) * LANES)
        mixed = jnp.dot(wm_ref[g], vnb[:, sl], preferred_element_type=F32) + bs_ref[:, sl]
        ya_ref[:, sl] = (u_ref[:, sl].astype(F32) * mixed
                         * sza_ref[:, sl].astype(F32)).astype(ya_ref.dtype)


def _gate_prompt(uv, sza, ln_g, ln_b, w_s, bs_rows, batch, seq):
    e_a = sza.shape[1]
    n_chunks = seq // CHUNK_A
    n_groups = w_s.shape[0]
    blk = lambda b, n: (b * n_chunks + n, 0)
    kern = functools.partial(_gate_kernel, n_chunks=n_chunks, n_groups=n_groups)
    return pl.pallas_call(
        kern,
        grid=(batch, n_chunks),
        in_specs=[
            pl.BlockSpec((CHUNK_A, e_a), blk),
            pl.BlockSpec((CHUNK_A, e_a), lambda b, n: (b * n_chunks + n, 1)),
            pl.BlockSpec((CHUNK_A, e_a), blk),
            pl.BlockSpec((1, e_a), lambda b, n: (0, 0)),
            pl.BlockSpec((1, e_a), lambda b, n: (0, 0)),
            pl.BlockSpec(w_s.shape, lambda b, n: (0, 0, 0)),
            pl.BlockSpec((CHUNK_A, e_a), lambda b, n: (0, 0)),
        ],
        out_specs=[
            pl.BlockSpec((CHUNK_A, e_a), blk),
            pl.BlockSpec((None, None, CHUNK_A, e_a), lambda b, n: (0, b, 0, 0)),
        ],
        out_shape=[
            jax.ShapeDtypeStruct((batch * seq, e_a), BF16),
            jax.ShapeDtypeStruct((1, batch, CHUNK_A, e_a), F32),
        ],
        scratch_shapes=[pltpu.VMEM((n_groups, CHUNK_A, CHUNK_A), BF16)],
        compiler_params=_params(("arbitrary", "arbitrary")),
        name="spatial_gate_prompt",
    )(uv, uv, sza, ln_g, ln_b, w_s, bs_rows)


def _gate_sample_kernel(u_ref, gv_ref, sza_ref, lng_ref, lnb_ref, wd_ref, b0_ref,
                        ya_ref, vr_ref):
    vn = _layernorm(gv_ref[...].astype(F32), lng_ref[...], lnb_ref[...])
    vr_ref[...] = vn
    mixed = wd_ref[...] * vn + b0_ref[...]
    ya_ref[...] = (u_ref[...].astype(F32) * mixed * sza_ref[...].astype(F32)).astype(ya_ref.dtype)


def _gate_sample(uv, sza, ln_g, ln_b, wd_row, b0_row):
    m, e_a = sza.shape
    full = lambda i: (0, 0)
    return pl.pallas_call(
        _gate_sample_kernel,
        grid=(1,),
        in_specs=[
            pl.BlockSpec((m, e_a), full),
            pl.BlockSpec((m, e_a), lambda i: (0, 1)),
            pl.BlockSpec((m, e_a), full),
            pl.BlockSpec((1, e_a), full),
            pl.BlockSpec((1, e_a), full),
            pl.BlockSpec((1, e_a), full),
            pl.BlockSpec((1, e_a), full),
        ],
        out_specs=[pl.BlockSpec((m, e_a), full), pl.BlockSpec((m, e_a), full)],
        out_shape=[jax.ShapeDtypeStruct((m, e_a), BF16),
                   jax.ShapeDtypeStruct((m, e_a), F32)],
        compiler_params=_params(("arbitrary",)),
        name="spatial_gate_sample",
    )(uv, uv, sza, ln_g, ln_b, wd_row, b0_row)


def _dot_nt(a, b):
    return lax.dot_general(a, b, (((1,), (1,)), ((), ())), preferred_element_type=F32)


def _dot_tn(a, b):
    return lax.dot_general(a, b, (((0,), (0,)), ((), ())), preferred_element_type=F32)


def _hgrn_kernel(q_ref, lf_ref, k_ref, i_ref, szb_ref, gon_ref, yb_ref, st_ref,
                 s_scr, o_scr, *att_scrs, n_chunks, n_heads):
    n = pl.program_id(1)
    c, hd, sb = CHUNK_B, HEAD_DIM, SUBLANES
    nv = c // sb

    @pl.when((pl.program_id(0) == 0) & (n == 0))
    def _():
        for att_scr in att_scrs:
            att_scr[...] = jnp.zeros_like(att_scr)

    @pl.when(n == 0)
    def _():
        s_scr[...] = jnp.zeros_like(s_scr)

    r8 = lax.broadcasted_iota(jnp.int32, (1, sb, hd), 1)
    t_idx = lax.broadcasted_iota(jnp.int32, (c, c), 0)
    s_idx = lax.broadcasted_iota(jnp.int32, (c, c), 1)
    fine = [h for h in (1, 2, 4) if 2 * h <= sb]
    coarse = [1 << l for l in range(sb.bit_length() - 1, (c // 2).bit_length())]
    second8 = {h: (r8 & (2 * h - 1)) >= h for h in fine}
    fine_mask = {}
    for h in fine:
        same = (t_idx >> h.bit_length()) == (s_idx >> h.bit_length())
        if h == 1:
            fine_mask[h] = same & (s_idx <= t_idx)
        else:
            fine_mask[h] = same & ((t_idx & (2 * h - 1)) >= h) & ((s_idx & (2 * h - 1)) < h)

    def tiles(x):
        return x.reshape(nv, sb, hd)

    def fine_boundary(p3, h):
        acc = None
        for blk in range(sb // (2 * h)):
            r = blk * 2 * h + h - 1
            row = jnp.broadcast_to(p3[:, r:r + 1, :], (nv, sb, hd))
            acc = row if acc is None else jnp.where(r8 >= blk * 2 * h, row, acc)
        return acc

    def one_head(h, att_ref):
        lf = lf_ref[h]
        q = q_ref[h]
        k = k_ref[h]
        ib = i_ref[h].astype(BF16)
        q3, k3, lf3 = tiles(q), tiles(k), tiles(lf)

        odd = second8[1]
        qf = jnp.where(odd, q3 * jnp.exp2(lf3), q3).reshape(c, hd).astype(BF16)
        kf = jnp.where(odd, k3 * jnp.exp2(-lf3), k3).reshape(c, hd).astype(BF16)
        att_f = jnp.where(fine_mask[1], _dot_nt(qf, kf), 0.0)
        p3 = lf3 + jnp.where(odd, tiles(pltpu.roll(lf, 1, 0)), 0.0)
        yield
        for hh in fine[1:]:
            bnd = fine_boundary(p3, hh)
            sec = second8[hh]
            g = jnp.where(sec, p3, bnd - p3)
            x = (jnp.where(sec, q3, k3) * jnp.exp2(g)).reshape(c, hd).astype(BF16)
            att_f = jnp.where(fine_mask[hh], _dot_nt(x, x), att_f)
            p3 = p3 + jnp.where(sec, bnd, 0.0)
            yield
        for v in range(nv):
            blk = slice(v * sb, (v + 1) * sb)
            att_ref[blk, blk] = att_f[blk, blk]

        p = p3.reshape(c, hd)
        for hh in coarse:
            xs, ps = [], []
            for blk in range(c // (2 * hh)):
                lo, mid, hi = blk * 2 * hh, blk * 2 * hh + hh, (blk + 1) * 2 * hh
                bnd = jnp.broadcast_to(p[mid - 1:mid, :], (hh, hd))
                p_lo, p_hi = p[lo:mid], p[mid:hi]
                xs.append(k[lo:mid] * jnp.exp2(bnd - p_lo))
                xs.append(q[mid:hi] * jnp.exp2(p_hi))
                ps.append(p_lo)
                ps.append(p_hi + bnd)
            x = jnp.concatenate(xs, axis=0).astype(BF16)
            a = _dot_nt(x, x)
            for blk in range(c // (2 * hh)):
                lo, mid, hi = blk * 2 * hh, blk * 2 * hh + hh, (blk + 1) * 2 * hh
                att_ref[mid:hi, lo:mid] = a[mid:hi, lo:mid]
            p = jnp.concatenate(ps, axis=0)
            yield

        b_last = p[c - 1:c, :]
        qt = (q * jnp.exp2(p)).astype(BF16)
        kt = (k * jnp.exp2(jnp.broadcast_to(b_last, (c, hd)) - p)).astype(BF16)
        dec = jnp.exp2(b_last)
        s_old = s_scr[h]
        lhs = jnp.concatenate([att_ref[...].astype(BF16), qt], axis=1)
        rhs = jnp.concatenate([ib, s_old.astype(BF16)], axis=0)
        o_scr[h] = jnp.dot(lhs, rhs, preferred_element_type=F32)
        dec_col = jnp.transpose(jnp.broadcast_to(dec, (hd, hd)))
        s_scr[h] = dec_col * s_old + _dot_tn(kt, ib)

    def head_group(grp, carry):
        heads = [one_head(grp * HEAD_UNROLL + u, att_scrs[u]) for u in range(HEAD_UNROLL)]
        for _ in itertools.zip_longest(*heads):
            pass
        return carry

    lax.fori_loop(0, n_heads // HEAD_UNROLL, head_group, 0)

    for h in range(n_heads):
        sl = slice(h * hd, (h + 1) * hd)
        o = o_scr[h]
        ms = jnp.mean(o * o, axis=-1, keepdims=True)
        yb_ref[:, sl] = (o * lax.rsqrt(ms + EPS) * gon_ref[:, sl]
                         * szb_ref[:, sl].astype(F32)).astype(yb_ref.dtype)

    @pl.when(n == n_chunks - 1)
    def _():
        st_ref[...] = s_scr[...]


def _hgrn_prompt(q, lf, k, iv, szb, g_onorm, batch, seq):
    n_heads = q.shape[0]
    e_b = n_heads * HEAD_DIM
    n_chunks = seq // CHUNK_B
    hm = pl.BlockSpec((n_heads, CHUNK_B, HEAD_DIM), lambda b, n: (0, b * n_chunks + n, 0))
    rm = pl.BlockSpec((CHUNK_B, e_b), lambda b, n: (b * n_chunks + n, 0))
    kern = functools.partial(_hgrn_kernel, n_chunks=n_chunks, n_heads=n_heads)
    return pl.pallas_call(
        kern,
        grid=(batch, n_chunks),
        in_specs=[hm, hm, hm, hm, rm, pl.BlockSpec((1, e_b), lambda b, n: (0, 0))],
        out_specs=[
            rm,
            pl.BlockSpec((None, None, n_heads, HEAD_DIM, HEAD_DIM),
                         lambda b, n: (0, b, 0, 0, 0)),
        ],
        out_shape=[
            jax.ShapeDtypeStruct((batch * seq, e_b), BF16),
            jax.ShapeDtypeStruct((1, batch, n_heads, HEAD_DIM, HEAD_DIM), F32),
        ],
        scratch_shapes=[
            pltpu.VMEM((n_heads, HEAD_DIM, HEAD_DIM), F32),
            pltpu.VMEM((n_heads, CHUNK_B, HEAD_DIM), F32),
        ] + [pltpu.VMEM((CHUNK_B, CHUNK_B), F32) for _ in range(HEAD_UNROLL)],
        compiler_params=_params(("arbitrary", "arbitrary")),
        name="hgrn_prompt",
    )(q, lf, k, iv, szb, g_onorm)


def _hgrn_sample_kernel(q_ref, lf_ref, k_ref, i_ref, st_ref, szb_ref, gon_ref,
                        sto_ref, yb_ref, ft_scr, kt_scr, o_scr, *, n_heads, nb):
    s = pl.program_id(0)
    hd = HEAD_DIM

    @pl.when(s == 0)
    def _():
        def tb(h, carry):
            ft_scr[h] = jnp.transpose(jnp.exp2(lf_ref[h]))
            kt_scr[h] = jnp.transpose(k_ref[h])
            return carry
        lax.fori_loop(0, n_heads, tb, 0)

    n_seq = lf_ref.shape[1]
    shift = jnp.where(s == 0, 0, n_seq - s * nb)

    def one_head(h):
        fr = pltpu.roll(ft_scr[h], shift, 1)
        kr = pltpu.roll(kt_scr[h], shift, 1)
        for j in range(nb):
            fb = jnp.broadcast_to(fr[:, j:j + 1], (hd, hd))
            kb = jnp.broadcast_to(kr[:, j:j + 1], (hd, hd))
            irow = i_ref[h, pl.ds(s * nb + j, 1), :]
            s_new = fb * st_ref[j, h] + kb * irow
            sto_ref[j, h] = s_new
            qrow = jnp.broadcast_to(q_ref[h, pl.ds(s * nb + j, 1), :], (SUBLANES, hd))
            o = jnp.dot(qrow.astype(BF16), s_new.astype(BF16), preferred_element_type=F32)
            o_scr[h, pl.ds(j, 1), :] = o[0:1, :]
            yield

    def head_group(grp, carry):
        heads = [one_head(grp * SAMPLE_HEAD_UNROLL + u) for u in range(SAMPLE_HEAD_UNROLL)]
        for _ in itertools.zip_longest(*heads):
            pass
        return carry

    lax.fori_loop(0, n_heads // SAMPLE_HEAD_UNROLL, head_group, 0)

    for h in range(n_heads):
        sl = slice(h * hd, (h + 1) * hd)
        o = o_scr[h]
        ms = jnp.mean(o * o, axis=-1, keepdims=True)
        yb_ref[:, sl] = (o * lax.rsqrt(ms + EPS) * gon_ref[:, sl] * szb_ref[:, sl])


def _hgrn_sample(q, lf, k, iv, state, szb, g_onorm):
    n_heads, n_seq, hd = q.shape
    nb = SAMPLE_SEQ_BLOCK
    e_b = n_heads * hd
    full3 = pl.BlockSpec((n_heads, n_seq, hd), lambda s: (0, 0, 0))
    st_spec = pl.BlockSpec((None, nb, n_heads, hd, hd), lambda s: (0, s, 0, 0, 0))
    kern = functools.partial(_hgrn_sample_kernel, n_heads=n_heads, nb=nb)
    return pl.pallas_call(
        kern,
        grid=(n_seq // nb,),
        in_specs=[full3, full3, full3, full3, st_spec,
                  pl.BlockSpec((nb, e_b), lambda s: (s, 0)),
                  pl.BlockSpec((1, e_b), lambda s: (0, 0))],
        out_specs=[st_spec, pl.BlockSpec((nb, e_b), lambda s: (s, 0))],
        out_shape=[jax.ShapeDtypeStruct(state.shape, F32),
                   jax.ShapeDtypeStruct((n_seq, e_b), F32)],
        scratch_shapes=[
            pltpu.VMEM((n_heads, hd, n_seq), F32),
            pltpu.VMEM((n_heads, hd, n_seq), F32),
            pltpu.VMEM((n_heads, nb, hd), F32),
        ],
        compiler_params=_params(("arbitrary",)),
        name="hgrn_sample",
    )(q, lf, k, iv, state, szb, g_onorm)


def _merge_kernel(yap_ref, ybp_ref, gap_ref, gbp_ref, yas_ref, ybs_ref, gas_ref, gbs_ref,
                  wpa_ref, wpb_ref, mp_ref, ms_ref, wa_bf, wb_bf, *, n_prompt_tiles):
    i = pl.program_id(1)

    @pl.when(i == 0)
    def _():
        wa_bf[...] = wpa_ref[...].astype(BF16)
        wb_bf[...] = wpb_ref[...].astype(BF16)

    def run(ya, yb, ga, gb, out):
        rows = ya.shape[0]
        sub = min(rows, SUB_ROWS)
        for m in range(rows // sub):
            rs = slice(m * sub, (m + 1) * sub)
            a = jnp.dot(ya[rs, :].astype(BF16), wa_bf[...], preferred_element_type=F32)
            b = jnp.dot(yb[rs, :].astype(BF16), wb_bf[...], preferred_element_type=F32)
            out[rs, :] = (ga[rs, :].astype(F32) * a + gb[rs, :].astype(F32) * b).astype(out.dtype)

    @pl.when(i != _sample_step(n_prompt_tiles))
    def _():
        run(yap_ref, ybp_ref, gap_ref, gbp_ref, mp_ref)

    @pl.when(i == _sample_step(n_prompt_tiles))
    def _():
        run(yas_ref, ybs_ref, gas_ref, gbs_ref, ms_ref)


def _merge(ya_p, yb_p, g_p, ya_s, yb_s, g_s, w_pa, w_pb):
    mp, kdim = ya_p.shape
    ms = ya_s.shape[0]
    d = w_pa.shape[-1]
    tm, tn = ROW_TILE, COL_TILE
    npt = mp // tm
    nj = d // tn

    def row_i(j, i):
        return _prompt_tile(j, i, npt)

    yp = pl.BlockSpec((tm, kdim), lambda j, i: (row_i(j, i), 0))
    ys = pl.BlockSpec((ms, kdim), lambda j, i: (0, 0))
    w = pl.BlockSpec((None, kdim, tn), lambda j, i: (0, 0, j))
    kern = functools.partial(_merge_kernel, n_prompt_tiles=npt)
    return pl.pallas_call(
        kern,
        grid=(nj, npt + 1),
        in_specs=[
            yp, yp,
            pl.BlockSpec((tm, tn), lambda j, i: (row_i(j, i), j)),
            pl.BlockSpec((tm, tn), lambda j, i: (row_i(j, i), nj + j)),
            ys, ys,
            pl.BlockSpec((ms, tn), lambda j, i: (0, j)),
            pl.BlockSpec((ms, tn), lambda j, i: (0, nj + j)),
            w, w,
        ],
        out_specs=[pl.BlockSpec((tm, tn), lambda j, i: (row_i(j, i), j)),
                   pl.BlockSpec((ms, tn), lambda j, i: (0, j))],
        out_shape=[jax.ShapeDtypeStruct((mp, d), BF16),
                   jax.ShapeDtypeStruct((ms, d), BF16)],
        scratch_shapes=[pltpu.VMEM((kdim, tn), BF16), pltpu.VMEM((kdim, tn), BF16)],
        compiler_params=_params(("arbitrary", "arbitrary")),
        name="gated_merge",
    )(ya_p, yb_p, g_p, g_p, ya_s, yb_s, g_s, g_s, w_pa, w_pb)


def _cast_kernel(w_ref, o_ref):
    o_ref[...] = w_ref[...].astype(o_ref.dtype)


def _cast_bf16(w3, rows):
    _, k, n = w3.shape
    return pl.pallas_call(
        _cast_kernel,
        grid=(k // rows,),
        in_specs=[pl.BlockSpec((None, rows, n), lambda i: (0, i, 0))],
        out_specs=pl.BlockSpec((rows, n), lambda i: (i, 0)),
        out_shape=jax.ShapeDtypeStruct((k, n), BF16),
        compiler_params=_params(("arbitrary",)),
        name="cast_w_out",
    )(w3)


def _out_post_kernel(mp_ref, xp_ref, ms_ref, xs_ref, w_ref, g_ref, op_ref, os_ref,
                     *, n_prompt_tiles):
    i = pl.program_id(0)

    def run(m_ref, x_ref, o_ref):
        z = jnp.dot(m_ref[...], w_ref[...], preferred_element_type=F32)
        ms = jnp.mean(z * z, axis=-1, keepdims=True)
        o_ref[...] = x_ref[...] + z * lax.rsqrt(ms + EPS) * g_ref[...]

    @pl.when(i < n_prompt_tiles)
    def _():
        run(mp_ref, xp_ref, op_ref)

    @pl.when(i == n_prompt_tiles)
    def _():
        run(ms_ref, xs_ref, os_ref)


def _out_post(m_p, x_p, m_s, x_s, w_bf, g):
    mp, d = x_p.shape
    ms = x_s.shape[0]
    tm = OUT_ROWS
    npt = mp // tm
    row = pl.BlockSpec((tm, d), lambda i: (jnp.minimum(i, npt - 1), 0))
    full_s = pl.BlockSpec((ms, d), lambda i: (0, 0))
    kern = functools.partial(_out_post_kernel, n_prompt_tiles=npt)
    return pl.pallas_call(
        kern,
        grid=(npt + 1,),
        in_specs=[row, row, full_s, full_s,
                  pl.BlockSpec(w_bf.shape, lambda i: (0, 0)),
                  pl.BlockSpec((1, d), lambda i: (0, 0))],
        out_specs=[row, full_s],
        out_shape=[jax.ShapeDtypeStruct((mp, d), F32), jax.ShapeDtypeStruct((ms, d), F32)],
        compiler_params=_params(("arbitrary",)),
        name="out_proj_post_norm",
    )(m_p, x_p, m_s, x_s, w_bf, g)


def kernel(x_prompt, x_sample, state_hgrn, lb_logits, g_pre, w_in, ln_g, ln_b, w_s, b_s,
           g_onorm, w_pa, w_pb, w_o, g_post):
    batch, seq, d = x_prompt.shape
    n_seq, dec_seq, _ = x_sample.shape
    depth = w_in.shape[0]
    assert depth == 1 and dec_seq == 1
    assert seq % CHUNK_A == 0 and seq % CHUNK_B == 0
    e = w_pa.shape[1]
    n_groups = w_s.shape[1]
    assert n_groups * CHUNK_A == e

    xp = x_prompt.reshape(batch * seq, d)
    xs = x_sample.reshape(n_seq, d)
    xn_p = _rmsnorm(xp, g_pre, NORM_ROWS)
    xn_s = _rmsnorm(xs, g_pre, n_seq)

    col = lambda idx: idx * e
    uv_p, uv_s = _proj(xn_p, xn_s, w_in, col(0), 2 * e, _ep_gelu,
                       [(BF16, BF16, False)], name="proj_uv")
    sza_p, sza_s = _proj(xn_p, xn_s, w_in, col(2), e, _ep_silu,
                         [(BF16, BF16, False)], name="proj_za")
    q_p, q_s = _proj(xn_p, xn_s, w_in, col(3), e, _ep_silu,
                     [(F32, F32, True)], name="proj_q")
    lf_p, lf_s, k_p, k_s = _proj(xn_p, xn_s, w_in, col(4), e, _ep_forget,
                                 [(F32, F32, True), (F32, F32, True)],
                                 aux=(lb_logits,), name="proj_f")
    i_p, i_s = _proj(xn_p, xn_s, w_in, col(5), e, _ep_identity,
                     [(F32, F32, True)], name="proj_i")
    szb_p, szb_s = _proj(xn_p, xn_s, w_in, col(6), e, _ep_silu,
                         [(BF16, F32, False)], name="proj_zb")
    g_p, g_s = _proj(xn_p, xn_s, w_in, col(7), 2 * d, _ep_sigmoid,
                     [(BF16, BF16, False)], name="proj_gates")

    bs_rows = jnp.repeat(b_s[0].T, CHUNK_A, axis=1)
    ya_p, vrows_p = _gate_prompt(uv_p, sza_p, ln_g, ln_b, w_s[0], bs_rows, batch, seq)
    wd_row = jnp.repeat(w_s[0, :, 0, 0], CHUNK_A)[None, :]
    b0_row = jnp.repeat(b_s[0, :, 0], CHUNK_A)[None, :]
    ya_s, vrows_s = _gate_sample(uv_s, sza_s, ln_g, ln_b, wd_row, b0_row)

    yb_p, state_p = _hgrn_prompt(q_p, lf_p, k_p, i_p, szb_p, g_onorm, batch, seq)
    state_s, yb_s = _hgrn_sample(q_s, lf_s, k_s, i_s, state_hgrn, szb_s, g_onorm)

    m_p, m_s = _merge(ya_p, yb_p, g_p, ya_s, yb_s, g_s, w_pa, w_pb)
    w_o_bf = _cast_bf16(w_o, CAST_ROWS)
    y_p, y_s = _out_post(m_p, xp, m_s, xs, w_o_bf, g_post)

    return (y_p.reshape(batch, seq, d), y_s.reshape(n_seq, 1, d),
            state_p, state_s,
            vrows_p, vrows_s.reshape(1, n_seq, 1, e))
```

```python
import functools
import itertools

import jax
import jax.numpy as jnp
from jax import lax
from jax.experimental import pallas as pl
from jax.experimental.pallas import tpu as pltpu

F32 = jnp.float32
BF16 = jnp.bfloat16

EPS = 1e-6
LOG2_E = 1.4426950408889634
LANES = 128
SUBLANES = 8
CHUNK_A = 128
GATE_STEP_CHUNKS = 4
HEAD_DIM = 128
CHUNK_B = 128
HEAD_UNROLL = 8
VMEM_LIMIT_BYTES = 56 * 1024 * 1024
ROW_TILE = 1024
COL_TILE = 512
PROJ_COL_TILE = 1024
SUB_ROWS = 256
NORM_ROWS = 256
OUT_ROWS = 128
CAST_ROWS = 512
SAMPLE_SEQ_BLOCK = 8
SAMPLE_HEAD_UNROLL = 4


def _params(sem):
    return pltpu.CompilerParams(dimension_semantics=sem,
                                vmem_limit_bytes=VMEM_LIMIT_BYTES)


def _rmsnorm_kernel(x_ref, g_ref, o_ref):
    x = x_ref[...]
    ms = jnp.mean(x * x, axis=-1, keepdims=True)
    o_ref[...] = (x * lax.rsqrt(ms + EPS) * g_ref[...]).astype(o_ref.dtype)


def _rmsnorm(x, g, rows):
    m, d = x.shape
    return pl.pallas_call(
        _rmsnorm_kernel,
        grid=(m // rows,),
        in_specs=[pl.BlockSpec((rows, d), lambda i: (i, 0)),
                  pl.BlockSpec((1, d), lambda i: (0, 0))],
        out_specs=pl.BlockSpec((rows, d), lambda i: (i, 0)),
        out_shape=jax.ShapeDtypeStruct((m, d), BF16),
        compiler_params=_params(("arbitrary",)),
        name="pre_rmsnorm",
    )(x, g)


def _sample_step(n_prompt_tiles):
    return n_prompt_tiles // 2


def _prompt_tile(j, i, n_prompt_tiles):
    fwd = i - (i > _sample_step(n_prompt_tiles)).astype(jnp.int32)
    return jnp.where(j % 2 == 0, fwd, n_prompt_tiles - 1 - fwd)


def _proj_kernel(*refs, epilogue, out_head_major, n_aux, n_prompt_tiles, col_tile0,
                 n_col_tiles, n_chunks):
    n_out = len(out_head_major)
    xp_ref, xs_ref, w_hbm = refs[:3]
    aux_refs = refs[3:3 + n_aux]
    out_refs = refs[3 + n_aux:3 + n_aux + 2 * n_out]
    wbf_ref, stage_ref, sem = refs[-3:]
    j = pl.program_id(0)
    i = pl.program_id(1)
    cur = j % 2
    kc, tn = stage_ref.shape

    def chunk_copy(col_tile, c):
        return pltpu.make_async_copy(
            w_hbm.at[0, pl.ds(c * kc, kc), pl.ds((col_tile0 + col_tile) * tn, tn)],
            stage_ref, sem.at[0])

    @pl.when((j == 0) & (i == 0))
    def _():
        for c in range(n_chunks):
            cp = chunk_copy(0, c)
            cp.start()
            cp.wait()
            wbf_ref[0, c * kc:(c + 1) * kc, :] = stage_ref[...].astype(BF16)

    prefetch = (j + 1 < n_col_tiles) & (i < n_chunks)

    @pl.when(prefetch)
    def _():
        chunk_copy(j + 1, i).start()

    def run(x_ref, outs):
        rows = x_ref.shape[0]
        sub = min(rows, SUB_ROWS)
        for m in range(rows // sub):
            rs = slice(m * sub, (m + 1) * sub)
            acc = jnp.dot(x_ref[rs, :].astype(BF16), wbf_ref[cur],
                          preferred_element_type=F32)
            res = epilogue(acc, *[a[...] for a in aux_refs])
            for r, o_ref, hm in zip(res, outs, out_head_major):
                if hm:
                    for hh in range(o_ref.shape[0]):
                        o_ref[hh, rs, :] = r[:, hh * LANES:(hh + 1) * LANES].astype(o_ref.dtype)
                else:
                    o_ref[rs, :] = r.astype(o_ref.dtype)

    @pl.when(i != _sample_step(n_prompt_tiles))
    def _():
        run(xp_ref, out_refs[0::2])

    @pl.when(i == _sample_step(n_prompt_tiles))
    def _():
        run(xs_ref, out_refs[1::2])

    @pl.when(prefetch)
    def _():
        chunk_copy(j + 1, i).wait()
        wbf_ref[1 - cur, pl.ds(i * kc, kc), :] = stage_ref[...].astype(BF16)


def _proj(xp, xs, w3, col0, ncols, epilogue, outs, aux=(), name="proj"):
    mp, kdim = xp.shape
    ms = xs.shape[0]
    tm, tn = ROW_TILE, PROJ_COL_TILE
    npt = mp // tm
    nj = ncols // tn
    assert col0 % tn == 0 and ncols % tn == 0 and kdim % npt == 0
    c0 = col0 // tn
    hpt = tn // LANES
    n_chunks = npt
    kc = kdim // n_chunks

    def row_i(j, i):
        return _prompt_tile(j, i, npt)

    in_specs = [
        pl.BlockSpec((tm, kdim), lambda j, i: (row_i(j, i), 0)),
        pl.BlockSpec((ms, kdim), lambda j, i: (0, 0)),
        pl.BlockSpec(memory_space=pl.ANY),
    ]
    for a in aux:
        in_specs.append(pl.BlockSpec((a.shape[0], tn), lambda j, i: (0, j)))
    out_specs, out_shapes = [], []
    for pdt, sdt, hm in outs:
        if hm:
            out_specs.append(pl.BlockSpec((hpt, tm, LANES), lambda j, i: (j, row_i(j, i), 0)))
            out_shapes.append(jax.ShapeDtypeStruct((ncols // LANES, mp, LANES), pdt))
            out_specs.append(pl.BlockSpec((hpt, ms, LANES), lambda j, i: (j, 0, 0)))
            out_shapes.append(jax.ShapeDtypeStruct((ncols // LANES, ms, LANES), sdt))
        else:
            out_specs.append(pl.BlockSpec((tm, tn), lambda j, i: (row_i(j, i), j)))
            out_shapes.append(jax.ShapeDtypeStruct((mp, ncols), pdt))
            out_specs.append(pl.BlockSpec((ms, tn), lambda j, i: (0, j)))
            out_shapes.append(jax.ShapeDtypeStruct((ms, ncols), sdt))
    kern = functools.partial(
        _proj_kernel, epilogue=epilogue,
        out_head_major=tuple(hm for _, _, hm in outs),
        n_aux=len(aux), n_prompt_tiles=npt, col_tile0=c0, n_col_tiles=nj,
        n_chunks=n_chunks)
    return pl.pallas_call(
        kern,
        grid=(nj, npt + 1),
        in_specs=in_specs,
        out_specs=out_specs,
        out_shape=out_shapes,
        scratch_shapes=[pltpu.VMEM((2, kdim, tn), BF16),
                        pltpu.VMEM((kc, tn), F32),
                        pltpu.SemaphoreType.DMA((1,))],
        compiler_params=_params(("arbitrary", "arbitrary")),
        name=name,
    )(xp, xs, w3, *aux)


def _ep_gelu(acc):
    return (jax.nn.gelu(acc, approximate=True),)


def _ep_silu(acc):
    return (acc * jax.nn.sigmoid(acc),)


def _ep_sigmoid(acc):
    return (jax.nn.sigmoid(acc),)


def _ep_identity(acc):
    return (acc,)


def _ep_forget(acc, lbl):
    m = jnp.max(lbl, axis=0, keepdims=True)
    e = jnp.exp(lbl - m)
    lb = e[0:1, :] / jnp.sum(e, axis=0, keepdims=True)
    sig = jax.nn.sigmoid(acc)
    log2_f = jnp.log(lb + (1.0 - lb) * sig) * LOG2_E
    k = (1.0 - lb) * (1.0 - sig)
    return log2_f, k


def _layernorm(gv, g, b):
    mu = jnp.mean(gv, axis=-1, keepdims=True)
    xc = gv - mu
    var = jnp.mean(xc * xc, axis=-1, keepdims=True)
    return xc * lax.rsqrt(var + EPS) * g + b


def _gate_kernel(u_ref, gv_ref, sza_ref, lng_ref, lnb_ref, ws_ref, bs_ref,
                 ya_ref, vr_ref, wm_ref, *, n_steps, n_groups):
    n = pl.program_id(1)

    @pl.when((pl.program_id(0) == 0) & (n == 0))
    def _():
        r = lax.broadcasted_iota(jnp.int32, (CHUNK_A, CHUNK_A), 0)
        c = lax.broadcasted_iota(jnp.int32, (CHUNK_A, CHUNK_A), 1)
        for g in range(n_groups):
            wm_ref[g] = jnp.where(c <= r, ws_ref[g], 0.0).astype(BF16)

    step_chunks = gv_ref.shape[0] // CHUNK_A
    for cc in range(step_chunks):
        rs = slice(cc * CHUNK_A, (cc + 1) * CHUNK_A)
        vn = _layernorm(gv_ref[rs, :].astype(F32), lng_ref[...], lnb_ref[...])

        if cc == step_chunks - 1:
            @pl.when(n == n_steps - 1)
            def _():
                vr_ref[...] = vn

        vnb = vn.astype(BF16)
        for g in range(n_groups):
            sl = slice(g * LANES, (g + 1) * LANES)
            mixed = jnp.dot(wm_ref[g], vnb[:, sl], preferred_element_type=F32) + bs_ref[:, sl]
            ya_ref[rs, sl] = (u_ref[rs, sl].astype(F32) * mixed
                              * sza_ref[rs, sl].astype(F32)).astype(ya_ref.dtype)


def _gate_prompt(uv, sza, ln_g, ln_b, w_s, bs_rows, batch, seq):
    e_a = sza.shape[1]
    rows = GATE_STEP_CHUNKS * CHUNK_A
    assert seq % rows == 0
    n_steps = seq // rows
    n_groups = w_s.shape[0]
    blk = lambda b, n: (b * n_steps + n, 0)
    kern = functools.partial(_gate_kernel, n_steps=n_steps, n_groups=n_groups)
    return pl.pallas_call(
        kern,
        grid=(batch, n_steps),
        in_specs=[
            pl.BlockSpec((rows, e_a), blk),
            pl.BlockSpec((rows, e_a), lambda b, n: (b * n_steps + n, 1)),
            pl.BlockSpec((rows, e_a), blk),
            pl.BlockSpec((1, e_a), lambda b, n: (0, 0)),
            pl.BlockSpec((1, e_a), lambda b, n: (0, 0)),
            pl.BlockSpec(w_s.shape, lambda b, n: (0, 0, 0)),
            pl.BlockSpec((CHUNK_A, e_a), lambda b, n: (0, 0)),
        ],
        out_specs=[
            pl.BlockSpec((rows, e_a), blk),
            pl.BlockSpec((None, None, CHUNK_A, e_a), lambda b, n: (0, b, 0, 0)),
        ],
        out_shape=[
            jax.ShapeDtypeStruct((batch * seq, e_a), BF16),
            jax.ShapeDtypeStruct((1, batch, CHUNK_A, e_a), F32),
        ],
        scratch_shapes=[pltpu.VMEM((n_groups, CHUNK_A, CHUNK_A), BF16)],
        compiler_params=_params(("arbitrary", "arbitrary")),
        name="spatial_gate_prompt",
    )(uv, uv, sza, ln_g, ln_b, w_s, bs_rows)


def _gate_sample_kernel(u_ref, gv_ref, sza_ref, lng_ref, lnb_ref, wd_ref, b0_ref,
                        ya_ref, vr_ref):
    vn = _layernorm(gv_ref[...].astype(F32), lng_ref[...], lnb_ref[...])
    vr_ref[...] = vn
    mixed = wd_ref[...] * vn + b0_ref[...]
    ya_ref[...] = (u_ref[...].astype(F32) * mixed * sza_ref[...].astype(F32)).astype(ya_ref.dtype)


def _gate_sample(uv, sza, ln_g, ln_b, wd_row, b0_row):
    m, e_a = sza.shape
    full = lambda i: (0, 0)
    return pl.pallas_call(
        _gate_sample_kernel,
        grid=(1,),
        in_specs=[
            pl.BlockSpec((m, e_a), full),
            pl.BlockSpec((m, e_a), lambda i: (0, 1)),
            pl.BlockSpec((m, e_a), full),
            pl.BlockSpec((1, e_a), full),
            pl.BlockSpec((1, e_a), full),
            pl.BlockSpec((1, e_a), full),
            pl.BlockSpec((1, e_a), full),
        ],
        out_specs=[pl.BlockSpec((m, e_a), full), pl.BlockSpec((m, e_a), full)],
        out_shape=[jax.ShapeDtypeStruct((m, e_a), BF16),
                   jax.ShapeDtypeStruct((m, e_a), F32)],
        compiler_params=_params(("arbitrary",)),
        name="spatial_gate_sample",
    )(uv, uv, sza, ln_g, ln_b, wd_row, b0_row)


def _dot_nt(a, b):
    return lax.dot_general(a, b, (((1,), (1,)), ((), ())), preferred_element_type=F32)


def _dot_tn(a, b):
    return lax.dot_general(a, b, (((0,), (0,)), ((), ())), preferred_element_type=F32)


def _hgrn_kernel(q_ref, lf_ref, k_ref, i_ref, szb_ref, gon_ref, yb_ref, st_ref,
                 s_scr, o_scr, *att_scrs, n_chunks, n_heads):
    n = pl.program_id(1)
    c, hd, sb = CHUNK_B, HEAD_DIM, SUBLANES
    nv = c // sb

    @pl.when((pl.program_id(0) == 0) & (n == 0))
    def _():
        for att_scr in att_scrs:
            att_scr[...] = jnp.zeros_like(att_scr)

    @pl.when(n == 0)
    def _():
        s_scr[...] = jnp.zeros_like(s_scr)

    r8 = lax.broadcasted_iota(jnp.int32, (1, sb, hd), 1)
    t_idx = lax.broadcasted_iota(jnp.int32, (c, c), 0)
    s_idx = lax.broadcasted_iota(jnp.int32, (c, c), 1)
    fine = [h for h in (1, 2, 4) if 2 * h <= sb]
    coarse = [1 << l for l in range(sb.bit_length() - 1, (c // 2).bit_length())]
    second8 = {h: (r8 & (2 * h - 1)) >= h for h in fine}
    fine_mask = {}
    for h in fine:
        same = (t_idx >> h.bit_length()) == (s_idx >> h.bit_length())
        if h == 1:
            fine_mask[h] = same & (s_idx <= t_idx)
        else:
            fine_mask[h] = same & ((t_idx & (2 * h - 1)) >= h) & ((s_idx & (2 * h - 1)) < h)

    def tiles(x):
        return x.reshape(nv, sb, hd)

    def fine_boundary(p3, h):
        acc = None
        for blk in range(sb // (2 * h)):
            r = blk * 2 * h + h - 1
            row = jnp.broadcast_to(p3[:, r:r + 1, :], (nv, sb, hd))
            acc = row if acc is None else jnp.where(r8 >= blk * 2 * h, row, acc)
        return acc

    def one_head(h, att_ref):
        lf = lf_ref[h]
        q = q_ref[h]
        k = k_ref[h]
        ib = i_ref[h].astype(BF16)
        q3, k3, lf3 = tiles(q), tiles(k), tiles(lf)

        odd = second8[1]
        qf = jnp.where(odd, q3 * jnp.exp2(lf3), q3).reshape(c, hd).astype(BF16)
        kf = jnp.where(odd, k3 * jnp.exp2(-lf3), k3).reshape(c, hd).astype(BF16)
        att_f = jnp.where(fine_mask[1], _dot_nt(qf, kf), 0.0)
        p3 = lf3 + jnp.where(odd, tiles(pltpu.roll(lf, 1, 0)), 0.0)
        yield
        for hh in fine[1:]:
            bnd = fine_boundary(p3, hh)
            sec = second8[hh]
            g = jnp.where(sec, p3, bnd - p3)
            x = (jnp.where(sec, q3, k3) * jnp.exp2(g)).reshape(c, hd).astype(BF16)
            att_f = jnp.where(fine_mask[hh], _dot_nt(x, x), att_f)
            p3 = p3 + jnp.where(sec, bnd, 0.0)
            yield
        for v in range(nv):
            blk = slice(v * sb, (v + 1) * sb)
            att_ref[blk, blk] = att_f[blk, blk]

        p = p3.reshape(c, hd)
        for hh in coarse:
            xs, ps = [], []
            for blk in range(c // (2 * hh)):
                lo, mid, hi = blk * 2 * hh, blk * 2 * hh + hh, (blk + 1) * 2 * hh
                bnd = jnp.broadcast_to(p[mid - 1:mid, :], (hh, hd))
                p_lo, p_hi = p[lo:mid], p[mid:hi]
                xs.append(k[lo:mid] * jnp.exp2(bnd - p_lo))
                xs.append(q[mid:hi] * jnp.exp2(p_hi))
                ps.append(p_lo)
                ps.append(p_hi + bnd)
            x = jnp.concatenate(xs, axis=0).astype(BF16)
            a = _dot_nt(x, x)
            for blk in range(c // (2 * hh)):
                lo, mid, hi = blk * 2 * hh, blk * 2 * hh + hh, (blk + 1) * 2 * hh
                att_ref[mid:hi, lo:mid] = a[mid:hi, lo:mid]
            p = jnp.concatenate(ps, axis=0)
            yield

        b_last = p[c - 1:c, :]
        qt = (q * jnp.exp2(p)).astype(BF16)
        kt = (k * jnp.exp2(jnp.broadcast_to(b_last, (c, hd)) - p)).astype(BF16)
        dec = jnp.exp2(b_last)
        s_old = s_scr[h]
        lhs = jnp.concatenate([att_ref[...].astype(BF16), qt], axis=1)
        rhs = jnp.concatenate([ib, s_old.astype(BF16)], axis=0)
        o_scr[h] = jnp.dot(lhs, rhs, preferred_element_type=F32)
        dec_col = jnp.transpose(jnp.broadcast_to(dec, (hd, hd)))
        s_scr[h] = dec_col * s_old + _dot_tn(kt, ib)

    def head_group(grp, carry):
        heads = [one_head(grp * HEAD_UNROLL + u, att_scrs[u]) for u in range(HEAD_UNROLL)]
        for _ in itertools.zip_longest(*heads):
            pass
        return carry

    lax.fori_loop(0, n_heads // HEAD_UNROLL, head_group, 0)

    for h in range(n_heads):
        sl = slice(h * hd, (h + 1) * hd)
        o = o_scr[h]
        ms = jnp.mean(o * o, axis=-1, keepdims=True)
        yb_ref[:, sl] = (o * lax.rsqrt(ms + EPS) * gon_ref[:, sl]
                         * szb_ref[:, sl].astype(F32)).astype(yb_ref.dtype)

    @pl.when(n == n_chunks - 1)
    def _():
        st_ref[...] = s_scr[...]


def _hgrn_prompt(q, lf, k, iv, szb, g_onorm, batch, seq):
    n_heads = q.shape[0]
    e_b = n_heads * HEAD_DIM
    n_chunks = seq // CHUNK_B
    hm = pl.BlockSpec((n_heads, CHUNK_B, HEAD_DIM), lambda b, n: (0, b * n_chunks + n, 0))
    rm = pl.BlockSpec((CHUNK_B, e_b), lambda b, n: (b * n_chunks + n, 0))
    kern = functools.partial(_hgrn_kernel, n_chunks=n_chunks, n_heads=n_heads)
    return pl.pallas_call(
        kern,
        grid=(batch, n_chunks),
        in_specs=[hm, hm, hm, hm, rm, pl.BlockSpec((1, e_b), lambda b, n: (0, 0))],
        out_specs=[
            rm,
            pl.BlockSpec((None, None, n_heads, HEAD_DIM, HEAD_DIM),
                         lambda b, n: (0, b, 0, 0, 0)),
        ],
        out_shape=[
            jax.ShapeDtypeStruct((batch * seq, e_b), BF16),
            jax.ShapeDtypeStruct((1, batch, n_heads, HEAD_DIM, HEAD_DIM), F32),
        ],
        scratch_shapes=[
            pltpu.VMEM((n_heads, HEAD_DIM, HEAD_DIM), F32),
            pltpu.VMEM((n_heads, CHUNK_B, HEAD_DIM), F32),
        ] + [pltpu.VMEM((CHUNK_B, CHUNK_B), F32) for _ in range(HEAD_UNROLL)],
        compiler_params=_params(("arbitrary", "arbitrary")),
        name="hgrn_prompt",
    )(q, lf, k, iv, szb, g_onorm)


def _hgrn_sample_kernel(q_ref, lf_ref, k_ref, i_ref, st_ref, szb_ref, gon_ref,
                        sto_ref, yb_ref, ft_scr, kt_scr, o_scr, *, n_heads, nb):
    s = pl.program_id(0)
    hd = HEAD_DIM

    @pl.when(s == 0)
    def _():
        def tb(h, carry):
            ft_scr[h] = jnp.transpose(jnp.exp2(lf_ref[h]))
            kt_scr[h] = jnp.transpose(k_ref[h])
            return carry
        lax.fori_loop(0, n_heads, tb, 0)

    n_seq = lf_ref.shape[1]
    shift = jnp.where(s == 0, 0, n_seq - s * nb)

    def one_head(h):
        fr = pltpu.roll(ft_scr[h], shift, 1)
        kr = pltpu.roll(kt_scr[h], shift, 1)
        for j in range(nb):
            fb = jnp.broadcast_to(fr[:, j:j + 1], (hd, hd))
            kb = jnp.broadcast_to(kr[:, j:j + 1], (hd, hd))
            irow = i_ref[h, pl.ds(s * nb + j, 1), :]
            s_new = fb * st_ref[j, h] + kb * irow
            sto_ref[j, h] = s_new
            qrow = jnp.broadcast_to(q_ref[h, pl.ds(s * nb + j, 1), :], (SUBLANES, hd))
            o = jnp.dot(qrow.astype(BF16), s_new.astype(BF16), preferred_element_type=F32)
            o_scr[h, pl.ds(j, 1), :] = o[0:1, :]
            yield

    def head_group(grp, carry):
        heads = [one_head(grp * SAMPLE_HEAD_UNROLL + u) for u in range(SAMPLE_HEAD_UNROLL)]
        for _ in itertools.zip_longest(*heads):
            pass
        return carry

    lax.fori_loop(0, n_heads // SAMPLE_HEAD_UNROLL, head_group, 0)

    for h in range(n_heads):
        sl = slice(h * hd, (h + 1) * hd)
        o = o_scr[h]
        ms = jnp.mean(o * o, axis=-1, keepdims=True)
        yb_ref[:, sl] = (o * lax.rsqrt(ms + EPS) * gon_ref[:, sl] * szb_ref[:, sl])


def _hgrn_sample(q, lf, k, iv, state, szb, g_onorm):
    n_heads, n_seq, hd = q.shape
    nb = SAMPLE_SEQ_BLOCK
    e_b = n_heads * hd
    full3 = pl.BlockSpec((n_heads, n_seq, hd), lambda s: (0, 0, 0))
    st_spec = pl.BlockSpec((None, nb, n_heads, hd, hd), lambda s: (0, s, 0, 0, 0))
    kern = functools.partial(_hgrn_sample_kernel, n_heads=n_heads, nb=nb)
    return pl.pallas_call(
        kern,
        grid=(n_seq // nb,),
        in_specs=[full3, full3, full3, full3, st_spec,
                  pl.BlockSpec((nb, e_b), lambda s: (s, 0)),
                  pl.BlockSpec((1, e_b), lambda s: (0, 0))],
        out_specs=[st_spec, pl.BlockSpec((nb, e_b), lambda s: (s, 0))],
        out_shape=[jax.ShapeDtypeStruct(state.shape, F32),
                   jax.ShapeDtypeStruct((n_seq, e_b), F32)],
        scratch_shapes=[
            pltpu.VMEM((n_heads, hd, n_seq), F32),
            pltpu.VMEM((n_heads, hd, n_seq), F32),
            pltpu.VMEM((n_heads, nb, hd), F32),
        ],
        compiler_params=_params(("arbitrary",)),
        name="hgrn_sample",
    )(q, lf, k, iv, state, szb, g_onorm)


def _merge_kernel(yap_ref, ybp_ref, gap_ref, gbp_ref, yas_ref, ybs_ref, gas_ref, gbs_ref,
                  wpa_ref, wpb_ref, mp_ref, ms_ref, wa_bf, wb_bf, *, n_prompt_tiles):
    i = pl.program_id(1)

    @pl.when(i == 0)
    def _():
        wa_bf[...] = wpa_ref[...].astype(BF16)
        wb_bf[...] = wpb_ref[...].astype(BF16)

    def run(ya, yb, ga, gb, out):
        rows = ya.shape[0]
        sub = min(rows, SUB_ROWS)
        for m in range(rows // sub):
            rs = slice(m * sub, (m + 1) * sub)
            a = jnp.dot(ya[rs, :].astype(BF16), wa_bf[...], preferred_element_type=F32)
            b = jnp.dot(yb[rs, :].astype(BF16), wb_bf[...], preferred_element_type=F32)
            out[rs, :] = (ga[rs, :].astype(F32) * a + gb[rs, :].astype(F32) * b).astype(out.dtype)

    @pl.when(i != _sample_step(n_prompt_tiles))
    def _():
        run(yap_ref, ybp_ref, gap_ref, gbp_ref, mp_ref)

    @pl.when(i == _sample_step(n_prompt_tiles))
    def _():
        run(yas_ref, ybs_ref, gas_ref, gbs_ref, ms_ref)


def _merge(ya_p, yb_p, g_p, ya_s, yb_s, g_s, w_pa, w_pb):
    mp, kdim = ya_p.shape
    ms = ya_s.shape[0]
    d = w_pa.shape[-1]
    tm, tn = ROW_TILE, COL_TILE
    npt = mp // tm
    nj = d // tn

    def row_i(j, i):
        return _prompt_tile(j, i, npt)

    yp = pl.BlockSpec((tm, kdim), lambda j, i: (row_i(j, i), 0))
    ys = pl.BlockSpec((ms, kdim), lambda j, i: (0, 0))
    w = pl.BlockSpec((None, kdim, tn), lambda j, i: (0, 0, j))
    kern = functools.partial(_merge_kernel, n_prompt_tiles=npt)
    return pl.pallas_call(
        kern,
        grid=(nj, npt + 1),
        in_specs=[
            yp, yp,
            pl.BlockSpec((tm, tn), lambda j, i: (row_i(j, i), j)),
            pl.BlockSpec((tm, tn), lambda j, i: (row_i(j, i), nj + j)),
            ys, ys,
            pl.BlockSpec((ms, tn), lambda j, i: (0, j)),
            pl.BlockSpec((ms, tn), lambda j, i: (0, nj + j)),
            w, w,
        ],
        out_specs=[pl.BlockSpec((tm, tn), lambda j, i: (row_i(j, i), j)),
                   pl.BlockSpec((ms, tn), lambda j, i: (0, j))],
        out_shape=[jax.ShapeDtypeStruct((mp, d), BF16),
                   jax.ShapeDtypeStruct((ms, d), BF16)],
        scratch_shapes=[pltpu.VMEM((kdim, tn), BF16), pltpu.VMEM((kdim, tn), BF16)],
        compiler_params=_params(("arbitrary", "arbitrary")),
        name="gated_merge",
    )(ya_p, yb_p, g_p, g_p, ya_s, yb_s, g_s, g_s, w_pa, w_pb)


def _cast_kernel(w_ref, o_ref):
    o_ref[...] = w_ref[...].astype(o_ref.dtype)


def _cast_bf16(w3, rows):
    _, k, n = w3.shape
    return pl.pallas_call(
        _cast_kernel,
        grid=(k // rows,),
        in_specs=[pl.BlockSpec((None, rows, n), lambda i: (0, i, 0))],
        out_specs=pl.BlockSpec((rows, n), lambda i: (i, 0)),
        out_shape=jax.ShapeDtypeStruct((k, n), BF16),
        compiler_params=_params(("arbitrary",)),
        name="cast_w_out",
    )(w3)


def _out_post_kernel(mp_ref, xp_ref, ms_ref, xs_ref, w_ref, g_ref, op_ref, os_ref,
                     *, n_prompt_tiles):
    i = pl.program_id(0)

    def run(m_ref, x_ref, o_ref):
        z = jnp.dot(m_ref[...], w_ref[...], preferred_element_type=F32)
        ms = jnp.mean(z * z, axis=-1, keepdims=True)
        o_ref[...] = x_ref[...] + z * lax.rsqrt(ms + EPS) * g_ref[...]

    @pl.when(i < n_prompt_tiles)
    def _():
        run(mp_ref, xp_ref, op_ref)

    @pl.when(i == n_prompt_tiles)
    def _():
        run(ms_ref, xs_ref, os_ref)


def _out_post(m_p, x_p, m_s, x_s, w_bf, g):
    mp, d = x_p.shape
    ms = x_s.shape[0]
    tm = OUT_ROWS
    npt = mp // tm
    row = pl.BlockSpec((tm, d), lambda i: (jnp.minimum(i, npt - 1), 0))
    full_s = pl.BlockSpec((ms, d), lambda i: (0, 0))
    kern = functools.partial(_out_post_kernel, n_prompt_tiles=npt)
    return pl.pallas_call(
        kern,
        grid=(npt + 1,),
        in_specs=[row, row, full_s, full_s,
                  pl.BlockSpec(w_bf.shape, lambda i: (0, 0)),
                  pl.BlockSpec((1, d), lambda i: (0, 0))],
        out_specs=[row, full_s],
        out_shape=[jax.ShapeDtypeStruct((mp, d), F32), jax.ShapeDtypeStruct((ms, d), F32)],
        compiler_params=_params(("arbitrary",)),
        name="out_proj_post_norm",
    )(m_p, x_p, m_s, x_s, w_bf, g)


def kernel(x_prompt, x_sample, state_hgrn, lb_logits, g_pre, w_in, ln_g, ln_b, w_s, b_s,
           g_onorm, w_pa, w_pb, w_o, g_post):
    batch, seq, d = x_prompt.shape
    n_seq, dec_seq, _ = x_sample.shape
    depth = w_in.shape[0]
    assert depth == 1 and dec_seq == 1
    assert seq % CHUNK_A == 0 and seq % CHUNK_B == 0
    e = w_pa.shape[1]
    n_groups = w_s.shape[1]
    assert n_groups * CHUNK_A == e

    xp = x_prompt.reshape(batch * seq, d)
    xs = x_sample.reshape(n_seq, d)
    xn_p = _rmsnorm(xp, g_pre, NORM_ROWS)
    xn_s = _rmsnorm(xs, g_pre, n_seq)

    col = lambda idx: idx * e
    uv_p, uv_s = _proj(xn_p, xn_s, w_in, col(0), 2 * e, _ep_gelu,
                       [(BF16, BF16, False)], name="proj_uv")
    sza_p, sza_s = _proj(xn_p, xn_s, w_in, col(2), e, _ep_silu,
                         [(BF16, BF16, False)], name="proj_za")
    q_p, q_s = _proj(xn_p, xn_s, w_in, col(3), e, _ep_silu,
                     [(F32, F32, True)], name="proj_q")
    lf_p, lf_s, k_p, k_s = _proj(xn_p, xn_s, w_in, col(4), e, _ep_forget,
                                 [(F32, F32, True), (F32, F32, True)],
                                 aux=(lb_logits,), name="proj_f")
    i_p, i_s = _proj(xn_p, xn_s, w_in, col(5), e, _ep_identity,
                     [(F32, F32, True)], name="proj_i")
    szb_p, szb_s = _proj(xn_p, xn_s, w_in, col(6), e, _ep_silu,
                         [(BF16, F32, False)], name="proj_zb")
    g_p, g_s = _proj(xn_p, xn_s, w_in, col(7), 2 * d, _ep_sigmoid,
                     [(BF16, BF16, False)], name="proj_gates")

    bs_rows = jnp.repeat(b_s[0].T, CHUNK_A, axis=1)
    ya_p, vrows_p = _gate_prompt(uv_p, sza_p, ln_g, ln_b, w_s[0], bs_rows, batch, seq)
    wd_row = jnp.repeat(w_s[0, :, 0, 0], CHUNK_A)[None, :]
    b0_row = jnp.repeat(b_s[0, :, 0], CHUNK_A)[None, :]
    ya_s, vrows_s = _gate_sample(uv_s, sza_s, ln_g, ln_b, wd_row, b0_row)

    yb_p, state_p = _hgrn_prompt(q_p, lf_p, k_p, i_p, szb_p, g_onorm, batch, seq)
    state_s, yb_s = _hgrn_sample(q_s, lf_s, k_s, i_s, state_hgrn, szb_s, g_onorm)

    m_p, m_s = _merge(ya_p, yb_p, g_p, ya_s, yb_s, g_s, w_pa, w_pb)
    w_o_bf = _cast_bf16(w_o, CAST_ROWS)
    y_p, y_s = _out_post(m_p, xp, m_s, xs, w_o_bf, g_post)

    return (y_p.reshape(batch, seq, d), y_s.reshape(n_seq, 1, d),
            state_p, state_s,
            vrows_p, vrows_s.reshape(1, n_seq, 1, e))
```

```python
import functools
import itertools

import jax
import jax.numpy as jnp
from jax import lax
from jax.experimental import pallas as pl
from jax.experimental.pallas import tpu as pltpu

F32 = jnp.float32
BF16 = jnp.bfloat16

EPS = 1e-6
LOG2_E = 1.4426950408889634
LANES = 128
SUBLANES = 8
CHUNK_A = 128
GATE_STEP_CHUNKS = 4
HEAD_DIM = 128
CHUNK_B = 128
HEAD_UNROLL = 8
VMEM_LIMIT_BYTES = 56 * 1024 * 1024
ROW_TILE = 1024
COL_TILE = 512
PROJ_COL_TILE = 1024
SUB_ROWS = 256
NORM_ROWS = 256
OUT_ROWS = 128
CAST_ROWS = 512
SAMPLE_SEQ_BLOCK = 8
SAMPLE_HEAD_UNROLL = 4


def _params(sem):
    return pltpu.CompilerParams(dimension_semantics=sem,
                                vmem_limit_bytes=VMEM_LIMIT_BYTES)


def _rmsnorm_kernel(x_ref, g_ref, o_ref):
    x = x_ref[...]
    ms = jnp.mean(x * x, axis=-1, keepdims=True)
    o_ref[...] = (x * lax.rsqrt(ms + EPS) * g_ref[...]).astype(o_ref.dtype)


def _rmsnorm(x, g, rows):
    m, d = x.shape
    return pl.pallas_call(
        _rmsnorm_kernel,
        grid=(m // rows,),
        in_specs=[pl.BlockSpec((rows, d), lambda i: (i, 0)),
                  pl.BlockSpec((1, d), lambda i: (0, 0))],
        out_specs=pl.BlockSpec((rows, d), lambda i: (i, 0)),
        out_shape=jax.ShapeDtypeStruct((m, d), BF16),
        compiler_params=_params(("arbitrary",)),
        name="pre_rmsnorm",
    )(x, g)


def _sample_step(n_prompt_tiles):
    return n_prompt_tiles // 2


def _prompt_tile(j, i, n_prompt_tiles):
    fwd = i - (i > _sample_step(n_prompt_tiles)).astype(jnp.int32)
    return jnp.where(j % 2 == 0, fwd, n_prompt_tiles - 1 - fwd)


def _proj_kernel(*refs, epilogues, out_head_major, n_aux, n_prompt_tiles, sections,
                 n_chunks):
    n_out = len(out_head_major)
    xp_ref, xs_ref, w_hbm = refs[:3]
    aux_refs = refs[3:3 + n_aux]
    out_refs = refs[3 + n_aux:3 + n_aux + 2 * n_out]
    wbf_ref, stage_ref, sem = refs[-3:]
    j = pl.program_id(0)
    i = pl.program_id(1)
    n_col_tiles = sum(count for _, count in sections)
    cur = j % 2
    _, kc, tn = stage_ref.shape

    def weight_col_tile(jj):
        tile, start = None, 0
        for first, count in sections:
            t = first + (jj - start)
            tile = t if tile is None else jnp.where(jj >= start, t, tile)
            start += count
        return tile

    def chunk_copy(jj, c, slot):
        return pltpu.make_async_copy(
            w_hbm.at[0, pl.ds(c * kc, kc), pl.ds(weight_col_tile(jj) * tn, tn)],
            stage_ref.at[slot], sem.at[slot])

    @pl.when((j == 0) & (i == 0))
    def _():
        chunk_copy(0, 0, 0).start()
        for c in range(n_chunks):
            if c + 1 < n_chunks:
                chunk_copy(0, c + 1, (c + 1) % 2).start()
            chunk_copy(0, c, c % 2).wait()
            wbf_ref[0, c * kc:(c + 1) * kc, :] = stage_ref[c % 2].astype(BF16)

    has_next = j + 1 < n_col_tiles

    @pl.when(has_next & (i >= 1))
    def _():
        chunk_copy(j + 1, i - 1, (i - 1) % 2).wait()

    @pl.when(has_next & (i < n_chunks))
    def _():
        chunk_copy(j + 1, i, i % 2).start()

    def cast_previous_chunk():
        prev = (i + n_chunks - 1) % n_chunks
        wbf_ref[1 - cur, pl.ds(prev * kc, kc), :] = stage_ref[(i + 1) % 2].astype(BF16)

    def run(x_ref, outs, epilogue):
        cast_previous_chunk()
        rows = x_ref.shape[0]
        sub = min(rows, SUB_ROWS)
        for m in range(rows // sub):
            rs = slice(m * sub, (m + 1) * sub)
            acc = jnp.dot(x_ref[rs, :].astype(BF16), wbf_ref[cur],
                          preferred_element_type=F32)
            res = epilogue(acc, *[a[...] for a in aux_refs])
            for r, o_ref, hm in zip(res, outs, out_head_major):
                if hm:
                    for hh in range(o_ref.shape[0]):
                        o_ref[hh, rs, :] = r[:, hh * LANES:(hh + 1) * LANES].astype(o_ref.dtype)
                else:
                    o_ref[rs, :] = r.astype(o_ref.dtype)

    is_sample = i == _sample_step(n_prompt_tiles)
    start = 0
    for (_, count), epilogue in zip(sections, epilogues):
        in_section = (j >= start) & (j < start + count)
        start += count

        @pl.when(in_section & jnp.logical_not(is_sample))
        def _(epilogue=epilogue):
            run(xp_ref, out_refs[0::2], epilogue)

        @pl.when(in_section & is_sample)
        def _(epilogue=epilogue):
            run(xs_ref, out_refs[1::2], epilogue)


def _proj(xp, xs, w3, col_sections, epilogues, outs, aux=(), tn=None, name="proj"):
    mp, kdim = xp.shape
    ms = xs.shape[0]
    tm = ROW_TILE
    tn = PROJ_COL_TILE if tn is None else tn
    npt = mp // tm
    assert all(c0 % tn == 0 and w % tn == 0 for c0, w in col_sections)
    sections = tuple((c0 // tn, w // tn) for c0, w in col_sections)
    ncols = sum(w for _, w in col_sections)
    nj = ncols // tn
    hpt = tn // LANES
    n_chunks = npt
    assert kdim % n_chunks == 0 and n_chunks >= 2
    kc = kdim // n_chunks

    def row_i(j, i):
        return _prompt_tile(j, i, npt)

    in_specs = [
        pl.BlockSpec((tm, kdim), lambda j, i: (row_i(j, i), 0)),
        pl.BlockSpec((ms, kdim), lambda j, i: (0, 0)),
        pl.BlockSpec(memory_space=pl.ANY),
    ]
    for a in aux:
        in_specs.append(pl.BlockSpec((a.shape[0], tn), lambda j, i: (0, j)))
    out_specs, out_shapes = [], []
    for pdt, sdt, hm in outs:
        if hm:
            out_specs.append(pl.BlockSpec((hpt, tm, LANES), lambda j, i: (j, row_i(j, i), 0)))
            out_shapes.append(jax.ShapeDtypeStruct((ncols // LANES, mp, LANES), pdt))
            out_specs.append(pl.BlockSpec((hpt, ms, LANES), lambda j, i: (j, 0, 0)))
            out_shapes.append(jax.ShapeDtypeStruct((ncols // LANES, ms, LANES), sdt))
        else:
            out_specs.append(pl.BlockSpec((tm, tn), lambda j, i: (row_i(j, i), j)))
            out_shapes.append(jax.ShapeDtypeStruct((mp, ncols), pdt))
            out_specs.append(pl.BlockSpec((ms, tn), lambda j, i: (0, j)))
            out_shapes.append(jax.ShapeDtypeStruct((ms, ncols), sdt))
    kern = functools.partial(
        _proj_kernel, epilogues=tuple(epilogues),
        out_head_major=tuple(hm for _, _, hm in outs),
        n_aux=len(aux), n_prompt_tiles=npt, sections=sections, n_chunks=n_chunks)
    return pl.pallas_call(
        kern,
        grid=(nj, npt + 1),
        in_specs=in_specs,
        out_specs=out_specs,
        out_shape=out_shapes,
        scratch_shapes=[pltpu.VMEM((2, kdim, tn), BF16),
                        pltpu.VMEM((2, kc, tn), F32),
                        pltpu.SemaphoreType.DMA((2,))],
        compiler_params=_params(("arbitrary", "arbitrary")),
        name=name,
    )(xp, xs, w3, *aux)


def _ep_gelu(acc):
    return (jax.nn.gelu(acc, approximate=True),)


def _ep_silu(acc):
    return (acc * jax.nn.sigmoid(acc),)


def _ep_sigmoid(acc):
    return (jax.nn.sigmoid(acc),)


def _ep_identity(acc):
    return (acc,)


def _ep_forget(acc, lbl):
    m = jnp.max(lbl, axis=0, keepdims=True)
    e = jnp.exp(lbl - m)
    lb = e[0:1, :] / jnp.sum(e, axis=0, keepdims=True)
    sig = jax.nn.sigmoid(acc)
    log2_f = jnp.log(lb + (1.0 - lb) * sig) * LOG2_E
    k = (1.0 - lb) * (1.0 - sig)
    return log2_f, k


def _layernorm(gv, g, b):
    mu = jnp.mean(gv, axis=-1, keepdims=True)
    xc = gv - mu
    var = jnp.mean(xc * xc, axis=-1, keepdims=True)
    return xc * lax.rsqrt(var + EPS) * g + b


def _gate_kernel(u_ref, gv_ref, sza_ref, lng_ref, lnb_ref, ws_ref, bs_ref,
                 ya_ref, vr_ref, wm_ref, *, n_steps, n_groups):
    n = pl.program_id(1)

    @pl.when((pl.program_id(0) == 0) & (n == 0))
    def _():
        r = lax.broadcasted_iota(jnp.int32, (CHUNK_A, CHUNK_A), 0)
        c = lax.broadcasted_iota(jnp.int32, (CHUNK_A, CHUNK_A), 1)
        for g in range(n_groups):
            wm_ref[g] = jnp.where(c <= r, ws_ref[g], 0.0).astype(BF16)

    step_chunks = gv_ref.shape[0] // CHUNK_A
    for cc in range(step_chunks):
        rs = slice(cc * CHUNK_A, (cc + 1) * CHUNK_A)
        vn = _layernorm(gv_ref[rs, :].astype(F32), lng_ref[...], lnb_ref[...])

        if cc == step_chunks - 1:
            @pl.when(n == n_steps - 1)
            def _():
                vr_ref[...] = vn

        vnb = vn.astype(BF16)
        for g in range(n_groups):
            sl = slice(g * LANES, (g + 1) * LANES)
            mixed = jnp.dot(wm_ref[g], vnb[:, sl], preferred_element_type=F32) + bs_ref[:, sl]
            ya_ref[rs, sl] = (u_ref[rs, sl].astype(F32) * mixed
                              * sza_ref[rs, sl].astype(F32)).astype(ya_ref.dtype)


def _gate_prompt(uv, sza, ln_g, ln_b, w_s, bs_rows, batch, seq, e_a):
    rows = GATE_STEP_CHUNKS * CHUNK_A
    assert seq % rows == 0
    n_steps = seq // rows
    n_groups = w_s.shape[0]
    blk = lambda b, n: (b * n_steps + n, 0)
    kern = functools.partial(_gate_kernel, n_steps=n_steps, n_groups=n_groups)
    return pl.pallas_call(
        kern,
        grid=(batch, n_steps),
        in_specs=[
            pl.BlockSpec((rows, e_a), blk),
            pl.BlockSpec((rows, e_a), lambda b, n: (b * n_steps + n, 1)),
            pl.BlockSpec((rows, e_a), blk),
            pl.BlockSpec((1, e_a), lambda b, n: (0, 0)),
            pl.BlockSpec((1, e_a), lambda b, n: (0, 0)),
            pl.BlockSpec(w_s.shape, lambda b, n: (0, 0, 0)),
            pl.BlockSpec((CHUNK_A, e_a), lambda b, n: (0, 0)),
        ],
        out_specs=[
            pl.BlockSpec((rows, e_a), blk),
            pl.BlockSpec((None, None, CHUNK_A, e_a), lambda b, n: (0, b, 0, 0)),
        ],
        out_shape=[
            jax.ShapeDtypeStruct((batch * seq, e_a), BF16),
            jax.ShapeDtypeStruct((1, batch, CHUNK_A, e_a), F32),
        ],
        scratch_shapes=[pltpu.VMEM((n_groups, CHUNK_A, CHUNK_A), BF16)],
        compiler_params=_params(("arbitrary", "arbitrary")),
        name="spatial_gate_prompt",
    )(uv, uv, sza, ln_g, ln_b, w_s, bs_rows)


def _gate_sample_kernel(u_ref, gv_ref, sza_ref, lng_ref, lnb_ref, wd_ref, b0_ref,
                        ya_ref, vr_ref):
    vn = _layernorm(gv_ref[...].astype(F32), lng_ref[...], lnb_ref[...])
    vr_ref[...] = vn
    mixed = wd_ref[...] * vn + b0_ref[...]
    ya_ref[...] = (u_ref[...].astype(F32) * mixed * sza_ref[...].astype(F32)).astype(ya_ref.dtype)


def _gate_sample(uv, sza, ln_g, ln_b, wd_row, b0_row, e_a):
    m = sza.shape[0]
    full = lambda i: (0, 0)
    return pl.pallas_call(
        _gate_sample_kernel,
        grid=(1,),
        in_specs=[
            pl.BlockSpec((m, e_a), full),
            pl.BlockSpec((m, e_a), lambda i: (0, 1)),
            pl.BlockSpec((m, e_a), full),
            pl.BlockSpec((1, e_a), full),
            pl.BlockSpec((1, e_a), full),
            pl.BlockSpec((1, e_a), full),
            pl.BlockSpec((1, e_a), full),
        ],
        out_specs=[pl.BlockSpec((m, e_a), full), pl.BlockSpec((m, e_a), full)],
        out_shape=[jax.ShapeDtypeStruct((m, e_a), BF16),
                   jax.ShapeDtypeStruct((m, e_a), F32)],
        compiler_params=_params(("arbitrary",)),
        name="spatial_gate_sample",
    )(uv, uv, sza, ln_g, ln_b, wd_row, b0_row)


def _dot_nt(a, b):
    return lax.dot_general(a, b, (((1,), (1,)), ((), ())), preferred_element_type=F32)


def _dot_tn(a, b):
    return lax.dot_general(a, b, (((0,), (0,)), ((), ())), preferred_element_type=F32)


def _hgrn_kernel(q_ref, lf_ref, k_ref, i_ref, szb_ref, gon_ref, yb_ref, st_ref,
                 s_scr, o_scr, *att_scrs, n_chunks, n_heads):
    n = pl.program_id(1)
    c, hd, sb = CHUNK_B, HEAD_DIM, SUBLANES
    nv = c // sb

    @pl.when((pl.program_id(0) == 0) & (n == 0))
    def _():
        for att_scr in att_scrs:
            att_scr[...] = jnp.zeros_like(att_scr)

    @pl.when(n == 0)
    def _():
        s_scr[...] = jnp.zeros_like(s_scr)

    r8 = lax.broadcasted_iota(jnp.int32, (1, sb, hd), 1)
    t_idx = lax.broadcasted_iota(jnp.int32, (c, c), 0)
    s_idx = lax.broadcasted_iota(jnp.int32, (c, c), 1)
    fine = [h for h in (1, 2, 4) if 2 * h <= sb]
    coarse = [1 << l for l in range(sb.bit_length() - 1, (c // 2).bit_length())]
    second8 = {h: (r8 & (2 * h - 1)) >= h for h in fine}
    fine_mask = {}
    for h in fine:
        same = (t_idx >> h.bit_length()) == (s_idx >> h.bit_length())
        if h == 1:
            fine_mask[h] = same & (s_idx <= t_idx)
        else:
            fine_mask[h] = same & ((t_idx & (2 * h - 1)) >= h) & ((s_idx & (2 * h - 1)) < h)

    def tiles(x):
        return x.reshape(nv, sb, hd)

    def fine_boundary(p3, h):
        acc = None
        for blk in range(sb // (2 * h)):
            r = blk * 2 * h + h - 1
            row = jnp.broadcast_to(p3[:, r:r + 1, :], (nv, sb, hd))
            acc = row if acc is None else jnp.where(r8 >= blk * 2 * h, row, acc)
        return acc

    def one_head(h, att_ref):
        lf = lf_ref[h]
        q = q_ref[h]
        k = k_ref[h]
        ib = i_ref[h].astype(BF16)
        q3, k3, lf3 = tiles(q), tiles(k), tiles(lf)

        odd = second8[1]
        qf = jnp.where(odd, q3 * jnp.exp2(lf3), q3).reshape(c, hd).astype(BF16)
        kf = jnp.where(odd, k3 * jnp.exp2(-lf3), k3).reshape(c, hd).astype(BF16)
        att_f = jnp.where(fine_mask[1], _dot_nt(qf, kf), 0.0)
        p3 = lf3 + jnp.where(odd, tiles(pltpu.roll(lf, 1, 0)), 0.0)
        yield
        for hh in fine[1:]:
            bnd = fine_boundary(p3, hh)
            sec = second8[hh]
            g = jnp.where(sec, p3, bnd - p3)
            x = (jnp.where(sec, q3, k3) * jnp.exp2(g)).reshape(c, hd).astype(BF16)
            att_f = jnp.where(fine_mask[hh], _dot_nt(x, x), att_f)
            p3 = p3 + jnp.where(sec, bnd, 0.0)
            yield
        for v in range(nv):
            blk = slice(v * sb, (v + 1) * sb)
            att_ref[blk, blk] = att_f[blk, blk]

        p = p3.reshape(c, hd)
        for hh in coarse:
            xs, ps = [], []
            for blk in range(c // (2 * hh)):
                lo, mid, hi = blk * 2 * hh, blk * 2 * hh + hh, (blk + 1) * 2 * hh
                bnd = jnp.broadcast_to(p[mid - 1:mid, :], (hh, hd))
                p_lo, p_hi = p[lo:mid], p[mid:hi]
                xs.append(k[lo:mid] * jnp.exp2(bnd - p_lo))
                xs.append(q[mid:hi] * jnp.exp2(p_hi))
                ps.append(p_lo)
                ps.append(p_hi + bnd)
            x = jnp.concatenate(xs, axis=0).astype(BF16)
            a = _dot_nt(x, x)
            for blk in range(c // (2 * hh)):
                lo, mid, hi = blk * 2 * hh, blk * 2 * hh + hh, (blk + 1) * 2 * hh
                att_ref[mid:hi, lo:mid] = a[mid:hi, lo:mid]
            p = jnp.concatenate(ps, axis=0)
            yield

        b_last = p[c - 1:c, :]
        qt = (q * jnp.exp2(p)).astype(BF16)
        kt = (k * jnp.exp2(jnp.broadcast_to(b_last, (c, hd)) - p)).astype(BF16)
        dec = jnp.exp2(b_last)
        s_old = s_scr[h]
        lhs = jnp.concatenate([att_ref[...].astype(BF16), qt], axis=1)
        rhs = jnp.concatenate([ib, s_old.astype(BF16)], axis=0)
        o_scr[h] = jnp.dot(lhs, rhs, preferred_element_type=F32)
        dec_col = jnp.transpose(jnp.broadcast_to(dec, (hd, hd)))
        s_scr[h] = dec_col * s_old + _dot_tn(kt, ib)

    def head_group(grp, carry):
        heads = [one_head(grp * HEAD_UNROLL + u, att_scrs[u]) for u in range(HEAD_UNROLL)]
        for _ in itertools.zip_longest(*heads):
            pass
        return carry

    lax.fori_loop(0, n_heads // HEAD_UNROLL, head_group, 0)

    for h in range(n_heads):
        sl = slice(h * hd, (h + 1) * hd)
        o = o_scr[h]
        ms = jnp.mean(o * o, axis=-1, keepdims=True)
        yb_ref[:, sl] = (o * lax.rsqrt(ms + EPS) * gon_ref[:, sl]
                         * szb_ref[:, sl].astype(F32)).astype(yb_ref.dtype)

    @pl.when(n == n_chunks - 1)
    def _():
        st_ref[...] = s_scr[...]


def _hgrn_prompt(qi, lf, k, zz, g_onorm, batch, seq):
    n_heads = lf.shape[0]
    e_b = n_heads * HEAD_DIM
    n_chunks = seq // CHUNK_B
    hm = pl.BlockSpec((n_heads, CHUNK_B, HEAD_DIM), lambda b, n: (0, b * n_chunks + n, 0))
    hm_i = pl.BlockSpec((n_heads, CHUNK_B, HEAD_DIM), lambda b, n: (1, b * n_chunks + n, 0))
    rm = pl.BlockSpec((CHUNK_B, e_b), lambda b, n: (b * n_chunks + n, 0))
    rm_zb = pl.BlockSpec((CHUNK_B, e_b), lambda b, n: (b * n_chunks + n, 1))
    kern = functools.partial(_hgrn_kernel, n_chunks=n_chunks, n_heads=n_heads)
    q, iv, szb = qi, qi, zz
    return pl.pallas_call(
        kern,
        grid=(batch, n_chunks),
        in_specs=[hm, hm, hm, hm_i, rm_zb, pl.BlockSpec((1, e_b), lambda b, n: (0, 0))],
        out_specs=[
            rm,
            pl.BlockSpec((None, None, n_heads, HEAD_DIM, HEAD_DIM),
                         lambda b, n: (0, b, 0, 0, 0)),
        ],
        out_shape=[
            jax.ShapeDtypeStruct((batch * seq, e_b), BF16),
            jax.ShapeDtypeStruct((1, batch, n_heads, HEAD_DIM, HEAD_DIM), F32),
        ],
        scratch_shapes=[
            pltpu.VMEM((n_heads, HEAD_DIM, HEAD_DIM), F32),
            pltpu.VMEM((n_heads, CHUNK_B, HEAD_DIM), F32),
        ] + [pltpu.VMEM((CHUNK_B, CHUNK_B), F32) for _ in range(HEAD_UNROLL)],
        compiler_params=_params(("arbitrary", "arbitrary")),
        name="hgrn_prompt",
    )(q, lf, k, iv, szb, g_onorm)


def _hgrn_sample_kernel(q_ref, lf_ref, k_ref, i_ref, st_ref, szb_ref, gon_ref,
                        sto_ref, yb_ref, ft_scr, kt_scr, o_scr, *, n_heads, nb):
    s = pl.program_id(0)
    hd = HEAD_DIM

    @pl.when(s == 0)
    def _():
        def tb(h, carry):
            ft_scr[h] = jnp.transpose(jnp.exp2(lf_ref[h]))
            kt_scr[h] = jnp.transpose(k_ref[h])
            return carry
        lax.fori_loop(0, n_heads, tb, 0)

    n_seq = lf_ref.shape[1]
    shift = jnp.where(s == 0, 0, n_seq - s * nb)

    def one_head(h):
        fr = pltpu.roll(ft_scr[h], shift, 1)
        kr = pltpu.roll(kt_scr[h], shift, 1)
        for j in range(nb):
            fb = jnp.broadcast_to(fr[:, j:j + 1], (hd, hd))
            kb = jnp.broadcast_to(kr[:, j:j + 1], (hd, hd))
            irow = i_ref[h, pl.ds(s * nb + j, 1), :]
            s_new = fb * st_ref[j, h] + kb * irow
            sto_ref[j, h] = s_new
            qrow = jnp.broadcast_to(q_ref[h, pl.ds(s * nb + j, 1), :], (SUBLANES, hd))
            o = jnp.dot(qrow.astype(BF16), s_new.astype(BF16), preferred_element_type=F32)
            o_scr[h, pl.ds(j, 1), :] = o[0:1, :]
            yield

    def head_group(grp, carry):
        heads = [one_head(grp * SAMPLE_HEAD_UNROLL + u) for u in range(SAMPLE_HEAD_UNROLL)]
        for _ in itertools.zip_longest(*heads):
            pass
        return carry

    lax.fori_loop(0, n_heads // SAMPLE_HEAD_UNROLL, head_group, 0)

    for h in range(n_heads):
        sl = slice(h * hd, (h + 1) * hd)
        o = o_scr[h]
        ms = jnp.mean(o * o, axis=-1, keepdims=True)
        yb_ref[:, sl] = (o * lax.rsqrt(ms + EPS) * gon_ref[:, sl] * szb_ref[:, sl])


def _hgrn_sample(qi, lf, k, state, zz, g_onorm):
    n_heads, n_seq, hd = lf.shape
    nb = SAMPLE_SEQ_BLOCK
    e_b = n_heads * hd
    full3 = pl.BlockSpec((n_heads, n_seq, hd), lambda s: (0, 0, 0))
    full3_i = pl.BlockSpec((n_heads, n_seq, hd), lambda s: (1, 0, 0))
    st_spec = pl.BlockSpec((None, nb, n_heads, hd, hd), lambda s: (0, s, 0, 0, 0))
    kern = functools.partial(_hgrn_sample_kernel, n_heads=n_heads, nb=nb)
    q, iv, szb = qi, qi, zz
    return pl.pallas_call(
        kern,
        grid=(n_seq // nb,),
        in_specs=[full3, full3, full3, full3_i, st_spec,
                  pl.BlockSpec((nb, e_b), lambda s: (s, 1)),
                  pl.BlockSpec((1, e_b), lambda s: (0, 0))],
        out_specs=[st_spec, pl.BlockSpec((nb, e_b), lambda s: (s, 0))],
        out_shape=[jax.ShapeDtypeStruct(state.shape, F32),
                   jax.ShapeDtypeStruct((n_seq, e_b), F32)],
        scratch_shapes=[
            pltpu.VMEM((n_heads, hd, n_seq), F32),
            pltpu.VMEM((n_heads, hd, n_seq), F32),
            pltpu.VMEM((n_heads, nb, hd), F32),
        ],
        compiler_params=_params(("arbitrary",)),
        name="hgrn_sample",
    )(q, lf, k, iv, state, szb, g_onorm)


def _merge_kernel(yap_ref, ybp_ref, gap_ref, gbp_ref, yas_ref, ybs_ref, gas_ref, gbs_ref,
                  wpa_ref, wpb_ref, mp_ref, ms_ref, wa_bf, wb_bf, *, n_prompt_tiles):
    i = pl.program_id(1)

    @pl.when(i == 0)
    def _():
        wa_bf[...] = wpa_ref[...].astype(BF16)
        wb_bf[...] = wpb_ref[...].astype(BF16)

    def run(ya, yb, ga, gb, out):
        rows = ya.shape[0]
        sub = min(rows, SUB_ROWS)
        for m in range(rows // sub):
            rs = slice(m * sub, (m + 1) * sub)
            a = jnp.dot(ya[rs, :].astype(BF16), wa_bf[...], preferred_element_type=F32)
            b = jnp.dot(yb[rs, :].astype(BF16), wb_bf[...], preferred_element_type=F32)
            out[rs, :] = (ga[rs, :].astype(F32) * a + gb[rs, :].astype(F32) * b).astype(out.dtype)

    @pl.when(i != _sample_step(n_prompt_tiles))
    def _():
        run(yap_ref, ybp_ref, gap_ref, gbp_ref, mp_ref)

    @pl.when(i == _sample_step(n_prompt_tiles))
    def _():
        run(yas_ref, ybs_ref, gas_ref, gbs_ref, ms_ref)


def _merge(ya_p, yb_p, g_p, ya_s, yb_s, g_s, w_pa, w_pb, gate_col0):
    mp, kdim = ya_p.shape
    ms = ya_s.shape[0]
    d = w_pa.shape[-1]
    tm, tn = ROW_TILE, COL_TILE
    npt = mp // tm
    nj = d // tn
    assert gate_col0 % tn == 0
    g0 = gate_col0 // tn

    def row_i(j, i):
        return _prompt_tile(j, i, npt)

    yp = pl.BlockSpec((tm, kdim), lambda j, i: (row_i(j, i), 0))
    ys = pl.BlockSpec((ms, kdim), lambda j, i: (0, 0))
    w = pl.BlockSpec((None, kdim, tn), lambda j, i: (0, 0, j))
    kern = functools.partial(_merge_kernel, n_prompt_tiles=npt)
    return pl.pallas_call(
        kern,
        grid=(nj, npt + 1),
        in_specs=[
            yp, yp,
            pl.BlockSpec((tm, tn), lambda j, i: (row_i(j, i), g0 + j)),
            pl.BlockSpec((tm, tn), lambda j, i: (row_i(j, i), g0 + nj + j)),
            ys, ys,
            pl.BlockSpec((ms, tn), lambda j, i: (0, g0 + j)),
            pl.BlockSpec((ms, tn), lambda j, i: (0, g0 + nj + j)),
            w, w,
        ],
        out_specs=[pl.BlockSpec((tm, tn), lambda j, i: (row_i(j, i), j)),
                   pl.BlockSpec((ms, tn), lambda j, i: (0, j))],
        out_shape=[jax.ShapeDtypeStruct((mp, d), BF16),
                   jax.ShapeDtypeStruct((ms, d), BF16)],
        scratch_shapes=[pltpu.VMEM((kdim, tn), BF16), pltpu.VMEM((kdim, tn), BF16)],
        compiler_params=_params(("arbitrary", "arbitrary")),
        name="gated_merge",
    )(ya_p, yb_p, g_p, g_p, ya_s, yb_s, g_s, g_s, w_pa, w_pb)


def _cast_kernel(w_ref, o_ref):
    o_ref[...] = w_ref[...].astype(o_ref.dtype)


def _cast_bf16(w3, rows):
    _, k, n = w3.shape
    return pl.pallas_call(
        _cast_kernel,
        grid=(k // rows,),
        in_specs=[pl.BlockSpec((None, rows, n), lambda i: (0, i, 0))],
        out_specs=pl.BlockSpec((rows, n), lambda i: (i, 0)),
        out_shape=jax.ShapeDtypeStruct((k, n), BF16),
        compiler_params=_params(("arbitrary",)),
        name="cast_w_out",
    )(w3)


def _out_post_kernel(mp_ref, xp_ref, ms_ref, xs_ref, w_ref, g_ref, op_ref, os_ref,
                     *, n_prompt_tiles):
    i = pl.program_id(0)

    def run(m_ref, x_ref, o_ref):
        z = jnp.dot(m_ref[...], w_ref[...], preferred_element_type=F32)
        ms = jnp.mean(z * z, axis=-1, keepdims=True)
        o_ref[...] = x_ref[...] + z * lax.rsqrt(ms + EPS) * g_ref[...]

    @pl.when(i < n_prompt_tiles)
    def _():
        run(mp_ref, xp_ref, op_ref)

    @pl.when(i == n_prompt_tiles)
    def _():
        run(ms_ref, xs_ref, os_ref)


def _out_post(m_p, x_p, m_s, x_s, w_bf, g):
    mp, d = x_p.shape
    ms = x_s.shape[0]
    tm = OUT_ROWS
    npt = mp // tm
    row = pl.BlockSpec((tm, d), lambda i: (jnp.minimum(i, npt - 1), 0))
    full_s = pl.BlockSpec((ms, d), lambda i: (0, 0))
    kern = functools.partial(_out_post_kernel, n_prompt_tiles=npt)
    return pl.pallas_call(
        kern,
        grid=(npt + 1,),
        in_specs=[row, row, full_s, full_s,
                  pl.BlockSpec(w_bf.shape, lambda i: (0, 0)),
                  pl.BlockSpec((1, d), lambda i: (0, 0))],
        out_specs=[row, full_s],
        out_shape=[jax.ShapeDtypeStruct((mp, d), F32), jax.ShapeDtypeStruct((ms, d), F32)],
        compiler_params=_params(("arbitrary",)),
        name="out_proj_post_norm",
    )(m_p, x_p, m_s, x_s, w_bf, g)


def kernel(x_prompt, x_sample, state_hgrn, lb_logits, g_pre, w_in, ln_g, ln_b, w_s, b_s,
           g_onorm, w_pa, w_pb, w_o, g_post):
    batch, seq, d = x_prompt.shape
    n_seq, dec_seq, _ = x_sample.shape
    depth = w_in.shape[0]
    assert depth == 1 and dec_seq == 1
    assert seq % CHUNK_A == 0 and seq % CHUNK_B == 0
    e = w_pa.shape[1]
    n_groups = w_s.shape[1]
    assert n_groups * CHUNK_A == e

    xp = x_prompt.reshape(batch * seq, d)
    xs = x_sample.reshape(n_seq, d)
    xn_p = _rmsnorm(xp, g_pre, NORM_ROWS)
    xn_s = _rmsnorm(xs, g_pre, n_seq)

    col = lambda idx: idx * e
    uvg_p, uvg_s = _proj(xn_p, xn_s, w_in, [(col(0), 2 * e), (col(7), 2 * d)],
                         [_ep_gelu, _ep_sigmoid],
                         [(BF16, BF16, False)], name="proj_uv_gates")
    zz_p, zz_s = _proj(xn_p, xn_s, w_in, [(col(2), e), (col(6), e)], [_ep_silu, _ep_silu],
                       [(BF16, F32, False)], name="proj_za_zb")
    qi_p, qi_s = _proj(xn_p, xn_s, w_in, [(col(3), e), (col(5), e)],
                       [_ep_silu, _ep_identity],
                       [(F32, F32, True)], name="proj_q_i")
    lf_p, lf_s, k_p, k_s = _proj(xn_p, xn_s, w_in, [(col(4), e)], [_ep_forget],
                                 [(F32, F32, True), (F32, F32, True)],
                                 aux=(lb_logits,), tn=COL_TILE, name="proj_f")

    bs_rows = jnp.repeat(b_s[0].T, CHUNK_A, axis=1)
    ya_p, vrows_p = _gate_prompt(uvg_p, zz_p, ln_g, ln_b, w_s[0], bs_rows, batch, seq, e)
    wd_row = jnp.repeat(w_s[0, :, 0, 0], CHUNK_A)[None, :]
    b0_row = jnp.repeat(b_s[0, :, 0], CHUNK_A)[None, :]
    ya_s, vrows_s = _gate_sample(uvg_s, zz_s, ln_g, ln_b, wd_row, b0_row, e)

    yb_p, state_p = _hgrn_prompt(qi_p, lf_p, k_p, zz_p, g_onorm, batch, seq)
    state_s, yb_s = _hgrn_sample(qi_s, lf_s, k_s, state_hgrn, zz_s, g_onorm)

    m_p, m_s = _merge(ya_p, yb_p, uvg_p, ya_s, yb_s, uvg_s, w_pa, w_pb, 2 * e)
    w_o_bf = _cast_bf16(w_o, CAST_ROWS)
    y_p, y_s = _out_post(m_p, xp, m_s, xs, w_o_bf, g_post)

    return (y_p.reshape(batch, seq, d), y_s.reshape(n_seq, 1, d),
            state_p, state_s,
            vrows_p, vrows_s.reshape(1, n_seq, 1, e))
```

```python
import functools
import itertools

import jax
import jax.numpy as jnp
from jax import lax
from jax.experimental import pallas as pl
from jax.experimental.pallas import tpu as pltpu

F32 = jnp.float32
BF16 = jnp.bfloat16

EPS = 1e-6
LOG2_E = 1.4426950408889634
LANES = 128
SUBLANES = 8
CHUNK_A = 128
GATE_STEP_CHUNKS = 4
HEAD_DIM = 128
CHUNK_B = 128
HGRN_STEP_CHUNKS = 4
HEAD_UNROLL = 8
VMEM_LIMIT_BYTES = 56 * 1024 * 1024
ROW_TILE = 1024
COL_TILE = 512
PROJ_COL_TILE = 1024
SUB_ROWS = 256
NORM_ROWS = 256
OUT_ROWS = 128
CAST_ROWS = 512
SAMPLE_SEQ_BLOCK = 8
SAMPLE_HEAD_UNROLL = 4


def _params(sem):
    return pltpu.CompilerParams(dimension_semantics=sem,
                                vmem_limit_bytes=VMEM_LIMIT_BYTES)


def _rmsnorm_kernel(x_ref, g_ref, o_ref):
    x = x_ref[...]
    ms = jnp.mean(x * x, axis=-1, keepdims=True)
    o_ref[...] = (x * lax.rsqrt(ms + EPS) * g_ref[...]).astype(o_ref.dtype)


def _rmsnorm(x, g, rows):
    m, d = x.shape
    return pl.pallas_call(
        _rmsnorm_kernel,
        grid=(m // rows,),
        in_specs=[pl.BlockSpec((rows, d), lambda i: (i, 0)),
                  pl.BlockSpec((1, d), lambda i: (0, 0))],
        out_specs=pl.BlockSpec((rows, d), lambda i: (i, 0)),
        out_shape=jax.ShapeDtypeStruct((m, d), BF16),
        compiler_params=_params(("arbitrary",)),
        name="pre_rmsnorm",
    )(x, g)


def _sample_step(n_prompt_tiles):
    return n_prompt_tiles // 2


def _prompt_tile(j, i, n_prompt_tiles):
    fwd = i - (i > _sample_step(n_prompt_tiles)).astype(jnp.int32)
    return jnp.where(j % 2 == 0, fwd, n_prompt_tiles - 1 - fwd)


def _proj_kernel(*refs, epilogues, out_head_major, n_aux, n_prompt_tiles, sections,
                 n_chunks):
    n_out = len(out_head_major)
    xp_ref, xs_ref, w_hbm = refs[:3]
    aux_refs = refs[3:3 + n_aux]
    out_refs = refs[3 + n_aux:3 + n_aux + 2 * n_out]
    wbf_ref, stage_ref, sem = refs[-3:]
    j = pl.program_id(0)
    i = pl.program_id(1)
    n_col_tiles = sum(count for _, count in sections)
    cur = j % 2
    _, kc, tn = stage_ref.shape

    def weight_col_tile(jj):
        tile, start = None, 0
        for first, count in sections:
            t = first + (jj - start)
            tile = t if tile is None else jnp.where(jj >= start, t, tile)
            start += count
        return tile

    def chunk_copy(jj, c, slot):
        return pltpu.make_async_copy(
            w_hbm.at[0, pl.ds(c * kc, kc), pl.ds(weight_col_tile(jj) * tn, tn)],
            stage_ref.at[slot], sem.at[slot])

    @pl.when((j == 0) & (i == 0))
    def _():
        chunk_copy(0, 0, 0).start()
        for c in range(n_chunks):
            if c + 1 < n_chunks:
                chunk_copy(0, c + 1, (c + 1) % 2).start()
            chunk_copy(0, c, c % 2).wait()
            wbf_ref[0, c * kc:(c + 1) * kc, :] = stage_ref[c % 2].astype(BF16)

    has_next = j + 1 < n_col_tiles

    @pl.when(has_next & (i >= 1))
    def _():
        chunk_copy(j + 1, i - 1, (i - 1) % 2).wait()

    @pl.when(has_next & (i < n_chunks))
    def _():
        chunk_copy(j + 1, i, i % 2).start()

    def cast_previous_chunk():
        prev = (i + n_chunks - 1) % n_chunks
        wbf_ref[1 - cur, pl.ds(prev * kc, kc), :] = stage_ref[(i + 1) % 2].astype(BF16)

    def run(x_ref, outs, epilogue):
        cast_previous_chunk()
        rows = x_ref.shape[0]
        sub = min(rows, SUB_ROWS)
        for m in range(rows // sub):
            rs = slice(m * sub, (m + 1) * sub)
            acc = jnp.dot(x_ref[rs, :].astype(BF16), wbf_ref[cur],
                          preferred_element_type=F32)
            res = epilogue(acc, *[a[...] for a in aux_refs])
            for r, o_ref, hm in zip(res, outs, out_head_major):
                if hm:
                    for hh in range(o_ref.shape[0]):
                        o_ref[hh, rs, :] = r[:, hh * LANES:(hh + 1) * LANES].astype(o_ref.dtype)
                else:
                    o_ref[rs, :] = r.astype(o_ref.dtype)

    is_sample = i == _sample_step(n_prompt_tiles)
    start = 0
    for (_, count), epilogue in zip(sections, epilogues):
        in_section = (j >= start) & (j < start + count)
        start += count

        @pl.when(in_section & jnp.logical_not(is_sample))
        def _(epilogue=epilogue):
            run(xp_ref, out_refs[0::2], epilogue)

        @pl.when(in_section & is_sample)
        def _(epilogue=epilogue):
            run(xs_ref, out_refs[1::2], epilogue)


def _proj(xp, xs, w3, col_sections, epilogues, outs, aux=(), tn=None, name="proj"):
    mp, kdim = xp.shape
    ms = xs.shape[0]
    tm = ROW_TILE
    tn = PROJ_COL_TILE if tn is None else tn
    npt = mp // tm
    assert all(c0 % tn == 0 and w % tn == 0 for c0, w in col_sections)
    sections = tuple((c0 // tn, w // tn) for c0, w in col_sections)
    ncols = sum(w for _, w in col_sections)
    nj = ncols // tn
    hpt = tn // LANES
    n_chunks = npt
    assert kdim % n_chunks == 0 and n_chunks >= 2
    kc = kdim // n_chunks

    def row_i(j, i):
        return _prompt_tile(j, i, npt)

    in_specs = [
        pl.BlockSpec((tm, kdim), lambda j, i: (row_i(j, i), 0)),
        pl.BlockSpec((ms, kdim), lambda j, i: (0, 0)),
        pl.BlockSpec(memory_space=pl.ANY),
    ]
    for a in aux:
        in_specs.append(pl.BlockSpec((a.shape[0], tn), lambda j, i: (0, j)))
    out_specs, out_shapes = [], []
    for pdt, sdt, hm in outs:
        if hm:
            out_specs.append(pl.BlockSpec((hpt, tm, LANES), lambda j, i: (j, row_i(j, i), 0)))
            out_shapes.append(jax.ShapeDtypeStruct((ncols // LANES, mp, LANES), pdt))
            out_specs.append(pl.BlockSpec((hpt, ms, LANES), lambda j, i: (j, 0, 0)))
            out_shapes.append(jax.ShapeDtypeStruct((ncols // LANES, ms, LANES), sdt))
        else:
            out_specs.append(pl.BlockSpec((tm, tn), lambda j, i: (row_i(j, i), j)))
            out_shapes.append(jax.ShapeDtypeStruct((mp, ncols), pdt))
            out_specs.append(pl.BlockSpec((ms, tn), lambda j, i: (0, j)))
            out_shapes.append(jax.ShapeDtypeStruct((ms, ncols), sdt))
    kern = functools.partial(
        _proj_kernel, epilogues=tuple(epilogues),
        out_head_major=tuple(hm for _, _, hm in outs),
        n_aux=len(aux), n_prompt_tiles=npt, sections=sections, n_chunks=n_chunks)
    return pl.pallas_call(
        kern,
        grid=(nj, npt + 1),
        in_specs=in_specs,
        out_specs=out_specs,
        out_shape=out_shapes,
        scratch_shapes=[pltpu.VMEM((2, kdim, tn), BF16),
                        pltpu.VMEM((2, kc, tn), F32),
                        pltpu.SemaphoreType.DMA((2,))],
        compiler_params=_params(("arbitrary", "arbitrary")),
        name=name,
    )(xp, xs, w3, *aux)


def _ep_gelu(acc):
    return (jax.nn.gelu(acc, approximate=True),)


def _ep_silu(acc):
    return (acc * jax.nn.sigmoid(acc),)


def _ep_sigmoid(acc):
    return (jax.nn.sigmoid(acc),)


def _ep_identity(acc):
    return (acc,)


def _ep_forget(acc, lbl):
    m = jnp.max(lbl, axis=0, keepdims=True)
    e = jnp.exp(lbl - m)
    lb = e[0:1, :] / jnp.sum(e, axis=0, keepdims=True)
    sig = jax.nn.sigmoid(acc)
    log2_f = jnp.log(lb + (1.0 - lb) * sig) * LOG2_E
    k = (1.0 - lb) * (1.0 - sig)
    return log2_f, k


def _layernorm(gv, g, b):
    mu = jnp.mean(gv, axis=-1, keepdims=True)
    xc = gv - mu
    var = jnp.mean(xc * xc, axis=-1, keepdims=True)
    return xc * lax.rsqrt(var + EPS) * g + b


def _gate_kernel(u_ref, gv_ref, sza_ref, lng_ref, lnb_ref, ws_ref, bs_ref,
                 ya_ref, vr_ref, wm_ref, *, n_steps, n_groups):
    n = pl.program_id(1)

    @pl.when((pl.program_id(0) == 0) & (n == 0))
    def _():
        r = lax.broadcasted_iota(jnp.int32, (CHUNK_A, CHUNK_A), 0)
        c = lax.broadcasted_iota(jnp.int32, (CHUNK_A, CHUNK_A), 1)
        for g in range(n_groups):
            wm_ref[g] = jnp.where(c <= r, ws_ref[g], 0.0).astype(BF16)

    step_chunks = gv_ref.shape[0] // CHUNK_A
    for cc in range(step_chunks):
        rs = slice(cc * CHUNK_A, (cc + 1) * CHUNK_A)
        vn = _layernorm(gv_ref[rs, :].astype(F32), lng_ref[...], lnb_ref[...])

        if cc == step_chunks - 1:
            @pl.when(n == n_steps - 1)
            def _():
                vr_ref[...] = vn

        vnb = vn.astype(BF16)
        for g in range(n_groups):
            sl = slice(g * LANES, (g + 1) * LANES)
            mixed = jnp.dot(wm_ref[g], vnb[:, sl], preferred_element_type=F32) + bs_ref[:, sl]
            ya_ref[rs, sl] = (u_ref[rs, sl].astype(F32) * mixed
                              * sza_ref[rs, sl].astype(F32)).astype(ya_ref.dtype)


def _gate_prompt(uv, sza, ln_g, ln_b, w_s, bs_rows, batch, seq, e_a):
    rows = GATE_STEP_CHUNKS * CHUNK_A
    assert seq % rows == 0
    n_steps = seq // rows
    n_groups = w_s.shape[0]
    blk = lambda b, n: (b * n_steps + n, 0)
    kern = functools.partial(_gate_kernel, n_steps=n_steps, n_groups=n_groups)
    return pl.pallas_call(
        kern,
        grid=(batch, n_steps),
        in_specs=[
            pl.BlockSpec((rows, e_a), blk),
            pl.BlockSpec((rows, e_a), lambda b, n: (b * n_steps + n, 1)),
            pl.BlockSpec((rows, e_a), blk),
            pl.BlockSpec((1, e_a), lambda b, n: (0, 0)),
            pl.BlockSpec((1, e_a), lambda b, n: (0, 0)),
            pl.BlockSpec(w_s.shape, lambda b, n: (0, 0, 0)),
            pl.BlockSpec((CHUNK_A, e_a), lambda b, n: (0, 0)),
        ],
        out_specs=[
            pl.BlockSpec((rows, e_a), blk),
            pl.BlockSpec((None, None, CHUNK_A, e_a), lambda b, n: (0, b, 0, 0)),
        ],
        out_shape=[
            jax.ShapeDtypeStruct((batch * seq, e_a), BF16),
            jax.ShapeDtypeStruct((1, batch, CHUNK_A, e_a), F32),
        ],
        scratch_shapes=[pltpu.VMEM((n_groups, CHUNK_A, CHUNK_A), BF16)],
        compiler_params=_params(("arbitrary", "arbitrary")),
        name="spatial_gate_prompt",
    )(uv, uv, sza, ln_g, ln_b, w_s, bs_rows)


def _gate_sample_kernel(u_ref, gv_ref, sza_ref, lng_ref, lnb_ref, wd_ref, b0_ref,
                        ya_ref, vr_ref):
    vn = _layernorm(gv_ref[...].astype(F32), lng_ref[...], lnb_ref[...])
    vr_ref[...] = vn
    mixed = wd_ref[...] * vn + b0_ref[...]
    ya_ref[...] = (u_ref[...].astype(F32) * mixed * sza_ref[...].astype(F32)).astype(ya_ref.dtype)


def _gate_sample(uv, sza, ln_g, ln_b, wd_row, b0_row, e_a):
    m = sza.shape[0]
    full = lambda i: (0, 0)
    return pl.pallas_call(
        _gate_sample_kernel,
        grid=(1,),
        in_specs=[
            pl.BlockSpec((m, e_a), full),
            pl.BlockSpec((m, e_a), lambda i: (0, 1)),
            pl.BlockSpec((m, e_a), full),
            pl.BlockSpec((1, e_a), full),
            pl.BlockSpec((1, e_a), full),
            pl.BlockSpec((1, e_a), full),
            pl.BlockSpec((1, e_a), full),
        ],
        out_specs=[pl.BlockSpec((m, e_a), full), pl.BlockSpec((m, e_a), full)],
        out_shape=[jax.ShapeDtypeStruct((m, e_a), BF16),
                   jax.ShapeDtypeStruct((m, e_a), F32)],
        compiler_params=_params(("arbitrary",)),
        name="spatial_gate_sample",
    )(uv, uv, sza, ln_g, ln_b, wd_row, b0_row)


def _dot_nt(a, b):
    return lax.dot_general(a, b, (((1,), (1,)), ((), ())), preferred_element_type=F32)


def _dot_tn(a, b):
    return lax.dot_general(a, b, (((0,), (0,)), ((), ())), preferred_element_type=F32)


def _hgrn_kernel(q_ref, lf_ref, k_ref, i_ref, szb_ref, gon_ref, yb_ref, st_ref,
                 s_scr, o_scr, *att_scrs, n_steps, n_heads):
    n = pl.program_id(1)
    c, hd, sb = CHUNK_B, HEAD_DIM, SUBLANES
    nv = c // sb

    @pl.when((pl.program_id(0) == 0) & (n == 0))
    def _():
        for att_scr in att_scrs:
            att_scr[...] = jnp.zeros_like(att_scr)

    @pl.when(n == 0)
    def _():
        s_scr[...] = jnp.zeros_like(s_scr)

    r8 = lax.broadcasted_iota(jnp.int32, (1, sb, hd), 1)
    t_idx = lax.broadcasted_iota(jnp.int32, (c, c), 0)
    s_idx = lax.broadcasted_iota(jnp.int32, (c, c), 1)
    fine = [h for h in (1, 2, 4) if 2 * h <= sb]
    coarse = [1 << l for l in range(sb.bit_length() - 1, (c // 2).bit_length())]
    second8 = {h: (r8 & (2 * h - 1)) >= h for h in fine}
    fine_mask = {}
    for h in fine:
        same = (t_idx >> h.bit_length()) == (s_idx >> h.bit_length())
        if h == 1:
            fine_mask[h] = same & (s_idx <= t_idx)
        else:
            fine_mask[h] = same & ((t_idx & (2 * h - 1)) >= h) & ((s_idx & (2 * h - 1)) < h)

    def tiles(x):
        return x.reshape(nv, sb, hd)

    def fine_boundary(p3, h):
        acc = None
        for blk in range(sb // (2 * h)):
            r = blk * 2 * h + h - 1
            row = jnp.broadcast_to(p3[:, r:r + 1, :], (nv, sb, hd))
            acc = row if acc is None else jnp.where(r8 >= blk * 2 * h, row, acc)
        return acc

    def one_head(h, rows, att_ref):
        lf = lf_ref[h, rows, :]
        q = q_ref[h, rows, :]
        k = k_ref[h, rows, :]
        ib = i_ref[h, rows, :].astype(BF16)
        q3, k3, lf3 = tiles(q), tiles(k), tiles(lf)

        odd = second8[1]
        qf = jnp.where(odd, q3 * jnp.exp2(lf3), q3).reshape(c, hd).astype(BF16)
        kf = jnp.where(odd, k3 * jnp.exp2(-lf3), k3).reshape(c, hd).astype(BF16)
        att_f = jnp.where(fine_mask[1], _dot_nt(qf, kf), 0.0)
        p3 = lf3 + jnp.where(odd, tiles(pltpu.roll(lf, 1, 0)), 0.0)
        yield
        for hh in fine[1:]:
            bnd = fine_boundary(p3, hh)
            sec = second8[hh]
            g = jnp.where(sec, p3, bnd - p3)
            x = (jnp.where(sec, q3, k3) * jnp.exp2(g)).reshape(c, hd).astype(BF16)
            att_f = jnp.where(fine_mask[hh], _dot_nt(x, x), att_f)
            p3 = p3 + jnp.where(sec, bnd, 0.0)
            yield
        for v in range(nv):
            blk = slice(v * sb, (v + 1) * sb)
            att_ref[blk, blk] = att_f[blk, blk]

        p = p3.reshape(c, hd)
        for hh in coarse:
            xs, ps = [], []
            for blk in range(c // (2 * hh)):
                lo, mid, hi = blk * 2 * hh, blk * 2 * hh + hh, (blk + 1) * 2 * hh
                bnd = jnp.broadcast_to(p[mid - 1:mid, :], (hh, hd))
                p_lo, p_hi = p[lo:mid], p[mid:hi]
                xs.append(k[lo:mid] * jnp.exp2(bnd - p_lo))
                xs.append(q[mid:hi] * jnp.exp2(p_hi))
                ps.append(p_lo)
                ps.append(p_hi + bnd)
            x = jnp.concatenate(xs, axis=0).astype(BF16)
            a = _dot_nt(x, x)
            for blk in range(c // (2 * hh)):
                lo, mid, hi = blk * 2 * hh, blk * 2 * hh + hh, (blk + 1) * 2 * hh
                att_ref[mid:hi, lo:mid] = a[mid:hi, lo:mid]
            p = jnp.concatenate(ps, axis=0)
            yield

        b_last = p[c - 1:c, :]
        qt = (q * jnp.exp2(p)).astype(BF16)
        kt = (k * jnp.exp2(jnp.broadcast_to(b_last, (c, hd)) - p)).astype(BF16)
        dec = jnp.exp2(b_last)
        s_old = s_scr[h]
        lhs = jnp.concatenate([att_ref[...].astype(BF16), qt], axis=1)
        rhs = jnp.concatenate([ib, s_old.astype(BF16)], axis=0)
        o_scr[h] = jnp.dot(lhs, rhs, preferred_element_type=F32)
        dec_col = jnp.transpose(jnp.broadcast_to(dec, (hd, hd)))
        s_scr[h] = dec_col * s_old + _dot_tn(kt, ib)

    def chunk_body(cc, chunk_carry):
        rows = pl.ds(pl.multiple_of(cc * c, c), c)

        def head_group(grp, carry):
            heads = [one_head(grp * HEAD_UNROLL + u, rows, att_scrs[u])
                     for u in range(HEAD_UNROLL)]
            for _ in itertools.zip_longest(*heads):
                pass
            return carry

        lax.fori_loop(0, n_heads // HEAD_UNROLL, head_group, 0)

        for h in range(n_heads):
            sl = slice(h * hd, (h + 1) * hd)
            o = o_scr[h]
            ms = jnp.mean(o * o, axis=-1, keepdims=True)
            yb_ref[rows, sl] = (o * lax.rsqrt(ms + EPS) * gon_ref[:, sl]
                                * szb_ref[rows, sl].astype(F32)).astype(yb_ref.dtype)
        return chunk_carry

    lax.fori_loop(0, q_ref.shape[1] // c, chunk_body, 0)

    @pl.when(n == n_steps - 1)
    def _():
        st_ref[...] = s_scr[...]


def _hgrn_prompt(qi, lf, k, zz, g_onorm, batch, seq):
    n_heads = lf.shape[0]
    e_b = n_heads * HEAD_DIM
    rows = HGRN_STEP_CHUNKS * CHUNK_B
    assert seq % rows == 0
    n_steps = seq // rows
    hm = pl.BlockSpec((n_heads, rows, HEAD_DIM), lambda b, n: (0, b * n_steps + n, 0))
    hm_i = pl.BlockSpec((n_heads, rows, HEAD_DIM), lambda b, n: (1, b * n_steps + n, 0))
    rm = pl.BlockSpec((rows, e_b), lambda b, n: (b * n_steps + n, 0))
    rm_zb = pl.BlockSpec((rows, e_b), lambda b, n: (b * n_steps + n, 1))
    kern = functools.partial(_hgrn_kernel, n_steps=n_steps, n_heads=n_heads)
    q, iv, szb = qi, qi, zz
    return pl.pallas_call(
        kern,
        grid=(batch, n_steps),
        in_specs=[hm, hm, hm, hm_i, rm_zb, pl.BlockSpec((1, e_b), lambda b, n: (0, 0))],
        out_specs=[
            rm,
            pl.BlockSpec((None, None, n_heads, HEAD_DIM, HEAD_DIM),
                         lambda b, n: (0, b, 0, 0, 0)),
        ],
        out_shape=[
            jax.ShapeDtypeStruct((batch * seq, e_b), BF16),
            jax.ShapeDtypeStruct((1, batch, n_heads, HEAD_DIM, HEAD_DIM), F32),
        ],
        scratch_shapes=[
            pltpu.VMEM((n_heads, HEAD_DIM, HEAD_DIM), F32),
            pltpu.VMEM((n_heads, CHUNK_B, HEAD_DIM), F32),
        ] + [pltpu.VMEM((CHUNK_B, CHUNK_B), F32) for _ in range(HEAD_UNROLL)],
        compiler_params=_params(("arbitrary", "arbitrary")),
        name="hgrn_prompt",
    )(q, lf, k, iv, szb, g_onorm)


def _hgrn_sample_kernel(q_ref, lf_ref, k_ref, i_ref, st_ref, szb_ref, gon_ref,
                        sto_ref, yb_ref, ft_scr, kt_scr, o_scr, *, n_heads, nb):
    s = pl.program_id(0)
    hd = HEAD_DIM

    @pl.when(s == 0)
    def _():
        def tb(h, carry):
            ft_scr[h] = jnp.transpose(jnp.exp2(lf_ref[h]))
            kt_scr[h] = jnp.transpose(k_ref[h])
            return carry
        lax.fori_loop(0, n_heads, tb, 0)

    n_seq = lf_ref.shape[1]
    shift = jnp.where(s == 0, 0, n_seq - s * nb)

    def one_head(h):
        fr = pltpu.roll(ft_scr[h], shift, 1)
        kr = pltpu.roll(kt_scr[h], shift, 1)
        for j in range(nb):
            fb = jnp.broadcast_to(fr[:, j:j + 1], (hd, hd))
            kb = jnp.broadcast_to(kr[:, j:j + 1], (hd, hd))
            irow = i_ref[h, pl.ds(s * nb + j, 1), :]
            s_new = fb * st_ref[j, h] + kb * irow
            sto_ref[j, h] = s_new
            qrow = jnp.broadcast_to(q_ref[h, pl.ds(s * nb + j, 1), :], (SUBLANES, hd))
            o = jnp.dot(qrow.astype(BF16), s_new.astype(BF16), preferred_element_type=F32)
            o_scr[h, pl.ds(j, 1), :] = o[0:1, :]
            yield

    def head_group(grp, carry):
        heads = [one_head(grp * SAMPLE_HEAD_UNROLL + u) for u in range(SAMPLE_HEAD_UNROLL)]
        for _ in itertools.zip_longest(*heads):
            pass
        return carry

    lax.fori_loop(0, n_heads // SAMPLE_HEAD_UNROLL, head_group, 0)

    for h in range(n_heads):
        sl = slice(h * hd, (h + 1) * hd)
        o = o_scr[h]
        ms = jnp.mean(o * o, axis=-1, keepdims=True)
        yb_ref[:, sl] = (o * lax.rsqrt(ms + EPS) * gon_ref[:, sl] * szb_ref[:, sl])


def _hgrn_sample(qi, lf, k, state, zz, g_onorm):
    n_heads, n_seq, hd = lf.shape
    nb = SAMPLE_SEQ_BLOCK
    e_b = n_heads * hd
    full3 = pl.BlockSpec((n_heads, n_seq, hd), lambda s: (0, 0, 0))
    full3_i = pl.BlockSpec((n_heads, n_seq, hd), lambda s: (1, 0, 0))
    st_spec = pl.BlockSpec((None, nb, n_heads, hd, hd), lambda s: (0, s, 0, 0, 0))
    kern = functools.partial(_hgrn_sample_kernel, n_heads=n_heads, nb=nb)
    q, iv, szb = qi, qi, zz
    return pl.pallas_call(
        kern,
        grid=(n_seq // nb,),
        in_specs=[full3, full3, full3, full3_i, st_spec,
                  pl.BlockSpec((nb, e_b), lambda s: (s, 1)),
                  pl.BlockSpec((1, e_b), lambda s: (0, 0))],
        out_specs=[st_spec, pl.BlockSpec((nb, e_b), lambda s: (s, 0))],
        out_shape=[jax.ShapeDtypeStruct(state.shape, F32),
                   jax.ShapeDtypeStruct((n_seq, e_b), F32)],
        scratch_shapes=[
            pltpu.VMEM((n_heads, hd, n_seq), F32),
            pltpu.VMEM((n_heads, hd, n_seq), F32),
            pltpu.VMEM((n_heads, nb, hd), F32),
        ],
        compiler_params=_params(("arbitrary",)),
        name="hgrn_sample",
    )(q, lf, k, iv, state, szb, g_onorm)


def _merge_kernel(yap_ref, ybp_ref, gap_ref, gbp_ref, yas_ref, ybs_ref, gas_ref, gbs_ref,
                  wpa_ref, wpb_ref, mp_ref, ms_ref, wa_bf, wb_bf, *, n_prompt_tiles):
    i = pl.program_id(1)

    @pl.when(i == 0)
    def _():
        wa_bf[...] = wpa_ref[...].astype(BF16)
        wb_bf[...] = wpb_ref[...].astype(BF16)

    def run(ya, yb, ga, gb, out):
        rows = ya.shape[0]
        sub = min(rows, SUB_ROWS)
        for m in range(rows // sub):
            rs = slice(m * sub, (m + 1) * sub)
            a = jnp.dot(ya[rs, :].astype(BF16), wa_bf[...], preferred_element_type=F32)
            b = jnp.dot(yb[rs, :].astype(BF16), wb_bf[...], preferred_element_type=F32)
            out[rs, :] = (ga[rs, :].astype(F32) * a + gb[rs, :].astype(F32) * b).astype(out.dtype)

    @pl.when(i != _sample_step(n_prompt_tiles))
    def _():
        run(yap_ref, ybp_ref, gap_ref, gbp_ref, mp_ref)

    @pl.when(i == _sample_step(n_prompt_tiles))
    def _():
        run(yas_ref, ybs_ref, gas_ref, gbs_ref, ms_ref)


def _merge(ya_p, yb_p, g_p, ya_s, yb_s, g_s, w_pa, w_pb, gate_col0):
    mp, kdim = ya_p.shape
    ms = ya_s.shape[0]
    d = w_pa.shape[-1]
    tm, tn = ROW_TILE, COL_TILE
    npt = mp // tm
    nj = d // tn
    assert gate_col0 % tn == 0
    g0 = gate_col0 // tn

    def row_i(j, i):
        return _prompt_tile(j, i, npt)

    yp = pl.BlockSpec((tm, kdim), lambda j, i: (row_i(j, i), 0))
    ys = pl.BlockSpec((ms, kdim), lambda j, i: (0, 0))
    w = pl.BlockSpec((None, kdim, tn), lambda j, i: (0, 0, j))
    kern = functools.partial(_merge_kernel, n_prompt_tiles=npt)
    return pl.pallas_call(
        kern,
        grid=(nj, npt + 1),
        in_specs=[
            yp, yp,
            pl.BlockSpec((tm, tn), lambda j, i: (row_i(j, i), g0 + j)),
            pl.BlockSpec((tm, tn), lambda j, i: (row_i(j, i), g0 + nj + j)),
            ys, ys,
            pl.BlockSpec((ms, tn), lambda j, i: (0, g0 + j)),
            pl.BlockSpec((ms, tn), lambda j, i: (0, g0 + nj + j)),
            w, w,
        ],
        out_specs=[pl.BlockSpec((tm, tn), lambda j, i: (row_i(j, i), j)),
                   pl.BlockSpec((ms, tn), lambda j, i: (0, j))],
        out_shape=[jax.ShapeDtypeStruct((mp, d), BF16),
                   jax.ShapeDtypeStruct((ms, d), BF16)],
        scratch_shapes=[pltpu.VMEM((kdim, tn), BF16), pltpu.VMEM((kdim, tn), BF16)],
        compiler_params=_params(("arbitrary", "arbitrary")),
        name="gated_merge",
    )(ya_p, yb_p, g_p, g_p, ya_s, yb_s, g_s, g_s, w_pa, w_pb)


def _cast_kernel(w_ref, o_ref):
    o_ref[...] = w_ref[...].astype(o_ref.dtype)


def _cast_bf16(w3, rows):
    _, k, n = w3.shape
    return pl.pallas_call(
        _cast_kernel,
        grid=(k // rows,),
        in_specs=[pl.BlockSpec((None, rows, n), lambda i: (0, i, 0))],
        out_specs=pl.BlockSpec((rows, n), lambda i: (i, 0)),
        out_shape=jax.ShapeDtypeStruct((k, n), BF16),
        compiler_params=_params(("arbitrary",)),
        name="cast_w_out",
    )(w3)


def _out_post_kernel(mp_ref, xp_ref, ms_ref, xs_ref, w_ref, g_ref, op_ref, os_ref, z_scr,
                     *, n_prompt_tiles):
    s = pl.program_id(0)

    def matmul(m_ref):
        z_scr[...] = jnp.dot(m_ref[...], w_ref[...], preferred_element_type=F32)

    def finish(x_ref, o_ref):
        z = z_scr[...]
        ms = jnp.mean(z * z, axis=-1, keepdims=True)
        o_ref[...] = x_ref[...] + z * lax.rsqrt(ms + EPS) * g_ref[...]

    @pl.when(s == 0)
    def _():
        matmul(mp_ref)

    @pl.when((s >= 1) & (s < n_prompt_tiles))
    def _():
        finish(xp_ref, op_ref)
        matmul(mp_ref)

    @pl.when(s == n_prompt_tiles)
    def _():
        finish(xp_ref, op_ref)
        matmul(ms_ref)

    @pl.when(s == n_prompt_tiles + 1)
    def _():
        finish(xs_ref, os_ref)


def _out_post(m_p, x_p, m_s, x_s, w_bf, g):
    mp, d = x_p.shape
    ms = x_s.shape[0]
    tm = OUT_ROWS
    assert ms == tm
    npt = mp // tm
    mm_row = pl.BlockSpec((tm, d), lambda s: (jnp.minimum(s, npt - 1), 0))
    fin_row = pl.BlockSpec((tm, d), lambda s: (jnp.clip(s - 1, 0, npt - 1), 0))
    full_s = pl.BlockSpec((ms, d), lambda s: (0, 0))
    kern = functools.partial(_out_post_kernel, n_prompt_tiles=npt)
    return pl.pallas_call(
        kern,
        grid=(npt + 2,),
        in_specs=[mm_row, fin_row, full_s, full_s,
                  pl.BlockSpec(w_bf.shape, lambda s: (0, 0)),
                  pl.BlockSpec((1, d), lambda s: (0, 0))],
        out_specs=[fin_row, full_s],
        out_shape=[jax.ShapeDtypeStruct((mp, d), F32), jax.ShapeDtypeStruct((ms, d), F32)],
        scratch_shapes=[pltpu.VMEM((tm, d), F32)],
        compiler_params=_params(("arbitrary",)),
        name="out_proj_post_norm",
    )(m_p, x_p, m_s, x_s, w_bf, g)


def kernel(x_prompt, x_sample, state_hgrn, lb_logits, g_pre, w_in, ln_g, ln_b, w_s, b_s,
           g_onorm, w_pa, w_pb, w_o, g_post):
    batch, seq, d = x_prompt.shape
    n_seq, dec_seq, _ = x_sample.shape
    depth = w_in.shape[0]
    assert depth == 1 and dec_seq == 1
    assert seq % CHUNK_A == 0 and seq % CHUNK_B == 0
    e = w_pa.shape[1]
    n_groups = w_s.shape[1]
    assert n_groups * CHUNK_A == e

    xp = x_prompt.reshape(batch * seq, d)
    xs = x_sample.reshape(n_seq, d)
    xn_p = _rmsnorm(xp, g_pre, NORM_ROWS)
    xn_s = _rmsnorm(xs, g_pre, n_seq)

    col = lambda idx: idx * e
    uvg_p, uvg_s = _proj(xn_p, xn_s, w_in, [(col(0), 2 * e), (col(7), 2 * d)],
                         [_ep_gelu, _ep_sigmoid],
                         [(BF16, BF16, False)], name="proj_uv_gates")
    zz_p, zz_s = _proj(xn_p, xn_s, w_in, [(col(2), e), (col(6), e)], [_ep_silu, _ep_silu],
                       [(BF16, F32, False)], name="proj_za_zb")
    qi_p, qi_s = _proj(xn_p, xn_s, w_in, [(col(3), e), (col(5), e)],
                       [_ep_silu, _ep_identity],
                       [(F32, F32, True)], name="proj_q_i")
    lf_p, lf_s, k_p, k_s = _proj(xn_p, xn_s, w_in, [(col(4), e)], [_ep_forget],
                                 [(F32, F32, True), (F32, F32, True)],
                                 aux=(lb_logits,), tn=COL_TILE, name="proj_f")

    bs_rows = jnp.repeat(b_s[0].T, CHUNK_A, axis=1)
    ya_p, vrows_p = _gate_prompt(uvg_p, zz_p, ln_g, ln_b, w_s[0], bs_rows, batch, seq, e)
    wd_row = jnp.repeat(w_s[0, :, 0, 0], CHUNK_A)[None, :]
    b0_row = jnp.repeat(b_s[0, :, 0], CHUNK_A)[None, :]
    ya_s, vrows_s = _gate_sample(uvg_s, zz_s, ln_g, ln_b, wd_row, b0_row, e)

    yb_p, state_p = _hgrn_prompt(qi_p, lf_p, k_p, zz_p, g_onorm, batch, seq)
    state_s, yb_s = _hgrn_sample(qi_s, lf_s, k_s, state_hgrn, zz_s, g_onorm)

    m_p, m_s = _merge(ya_p, yb_p, uvg_p, ya_s, yb_s, uvg_s, w_pa, w_pb, 2 * e)
    w_o_bf = _cast_bf16(w_o, CAST_ROWS)
    y_p, y_s = _out_post(m_p, xp, m_s, xs, w_o_bf, g_post)

    return (y_p.reshape(batch, seq, d), y_s.reshape(n_seq, 1, d),
            state_p, state_s,
            vrows_p, vrows_s.reshape(1, n_seq, 1, e))
```

```python
import functools
import itertools

import jax
import jax.numpy as jnp
from jax import lax
from jax.experimental import pallas as pl
from jax.experimental.pallas import tpu as pltpu

F32 = jnp.float32
BF16 = jnp.bfloat16

EPS = 1e-6
LOG2_E = 1.4426950408889634
LANES = 128
SUBLANES = 8
CHUNK_A = 128
GATE_STEP_CHUNKS = 4
HEAD_DIM = 128
CHUNK_B = 128
HGRN_STEP_CHUNKS = 4
HEAD_UNROLL = 8
VMEM_LIMIT_BYTES = 56 * 1024 * 1024
ROW_TILE = 1024
COL_TILE = 512
PROJ_COL_TILE = 1024
SUB_ROWS = 512
NORM_ROWS = 256
OUT_ROWS = 128
CAST_ROWS = 512
SAMPLE_SEQ_BLOCK = 8
SAMPLE_HEAD_UNROLL = 4


def _params(sem):
    return pltpu.CompilerParams(dimension_semantics=sem,
                                vmem_limit_bytes=VMEM_LIMIT_BYTES)


def _rmsnorm_kernel(x_ref, g_ref, o_ref):
    x = x_ref[...]
    ms = jnp.mean(x * x, axis=-1, keepdims=True)
    o_ref[...] = (x * lax.rsqrt(ms + EPS) * g_ref[...]).astype(o_ref.dtype)


def _rmsnorm(x, g, rows):
    m, d = x.shape
    return pl.pallas_call(
        _rmsnorm_kernel,
        grid=(m // rows,),
        in_specs=[pl.BlockSpec((rows, d), lambda i: (i, 0)),
                  pl.BlockSpec((1, d), lambda i: (0, 0))],
        out_specs=pl.BlockSpec((rows, d), lambda i: (i, 0)),
        out_shape=jax.ShapeDtypeStruct((m, d), BF16),
        compiler_params=_params(("arbitrary",)),
        name="pre_rmsnorm",
    )(x, g)


def _sample_step(n_prompt_tiles):
    return n_prompt_tiles // 2


def _prompt_tile(j, i, n_prompt_tiles):
    fwd = i - (i > _sample_step(n_prompt_tiles)).astype(jnp.int32)
    return jnp.where(j % 2 == 0, fwd, n_prompt_tiles - 1 - fwd)


def _proj_kernel(*refs, epilogues, out_head_major, n_aux, n_prompt_tiles, sections,
                 n_chunks):
    n_out = len(out_head_major)
    xp_ref, xs_ref, w_hbm = refs[:3]
    aux_refs = refs[3:3 + n_aux]
    out_refs = refs[3 + n_aux:3 + n_aux + 2 * n_out]
    wbf_ref, stage_ref, sem = refs[-3:]
    j = pl.program_id(0)
    i = pl.program_id(1)
    n_col_tiles = sum(count for _, count in sections)
    cur = j % 2
    _, kc, tn = stage_ref.shape

    def weight_col_tile(jj):
        tile, start = None, 0
        for first, count in sections:
            t = first + (jj - start)
            tile = t if tile is None else jnp.where(jj >= start, t, tile)
            start += count
        return tile

    def chunk_copy(jj, c, slot):
        return pltpu.make_async_copy(
            w_hbm.at[0, pl.ds(c * kc, kc), pl.ds(weight_col_tile(jj) * tn, tn)],
            stage_ref.at[slot], sem.at[slot])

    @pl.when((j == 0) & (i == 0))
    def _():
        chunk_copy(0, 0, 0).start()
        for c in range(n_chunks):
            if c + 1 < n_chunks:
                chunk_copy(0, c + 1, (c + 1) % 2).start()
            chunk_copy(0, c, c % 2).wait()
            wbf_ref[0, c * kc:(c + 1) * kc, :] = stage_ref[c % 2].astype(BF16)

    has_next = j + 1 < n_col_tiles

    @pl.when(has_next & (i >= 1))
    def _():
        chunk_copy(j + 1, i - 1, (i - 1) % 2).wait()

    @pl.when(has_next & (i < n_chunks))
    def _():
        chunk_copy(j + 1, i, i % 2).start()

    def cast_previous_chunk():
        prev = (i + n_chunks - 1) % n_chunks
        wbf_ref[1 - cur, pl.ds(prev * kc, kc), :] = stage_ref[(i + 1) % 2].astype(BF16)

    def run(x_ref, outs, epilogue):
        cast_previous_chunk()
        rows = x_ref.shape[0]
        sub = min(rows, SUB_ROWS)
        for m in range(rows // sub):
            rs = slice(m * sub, (m + 1) * sub)
            acc = jnp.dot(x_ref[rs, :].astype(BF16), wbf_ref[cur],
                          preferred_element_type=F32)
            res = epilogue(acc, *[a[...] for a in aux_refs])
            for r, o_ref, hm in zip(res, outs, out_head_major):
                if hm:
                    for hh in range(o_ref.shape[0]):
                        o_ref[hh, rs, :] = r[:, hh * LANES:(hh + 1) * LANES].astype(o_ref.dtype)
                else:
                    o_ref[rs, :] = r.astype(o_ref.dtype)

    is_sample = i == _sample_step(n_prompt_tiles)
    start = 0
    for (_, count), epilogue in zip(sections, epilogues):
        in_section = (j >= start) & (j < start + count)
        start += count

        @pl.when(in_section & jnp.logical_not(is_sample))
        def _(epilogue=epilogue):
            run(xp_ref, out_refs[0::2], epilogue)

        @pl.when(in_section & is_sample)
        def _(epilogue=epilogue):
            run(xs_ref, out_refs[1::2], epilogue)


def _proj(xp, xs, w3, col_sections, epilogues, outs, aux=(), tn=None, name="proj"):
    mp, kdim = xp.shape
    ms = xs.shape[0]
    tm = ROW_TILE
    tn = PROJ_COL_TILE if tn is None else tn
    npt = mp // tm
    assert all(c0 % tn == 0 and w % tn == 0 for c0, w in col_sections)
    sections = tuple((c0 // tn, w // tn) for c0, w in col_sections)
    ncols = sum(w for _, w in col_sections)
    nj = ncols // tn
    hpt = tn // LANES
    n_chunks = npt
    assert kdim % n_chunks == 0 and n_chunks >= 2
    kc = kdim // n_chunks

    def row_i(j, i):
        return _prompt_tile(j, i, npt)

    in_specs = [
        pl.BlockSpec((tm, kdim), lambda j, i: (row_i(j, i), 0)),
        pl.BlockSpec((ms, kdim), lambda j, i: (0, 0)),
        pl.BlockSpec(memory_space=pl.ANY),
    ]
    for a in aux:
        in_specs.append(pl.BlockSpec((a.shape[0], tn), lambda j, i: (0, j)))
    out_specs, out_shapes = [], []
    for pdt, sdt, hm in outs:
        if hm:
            out_specs.append(pl.BlockSpec((hpt, tm, LANES), lambda j, i: (j, row_i(j, i), 0)))
            out_shapes.append(jax.ShapeDtypeStruct((ncols // LANES, mp, LANES), pdt))
            out_specs.append(pl.BlockSpec((hpt, ms, LANES), lambda j, i: (j, 0, 0)))
            out_shapes.append(jax.ShapeDtypeStruct((ncols // LANES, ms, LANES), sdt))
        else:
            out_specs.append(pl.BlockSpec((tm, tn), lambda j, i: (row_i(j, i), j)))
            out_shapes.append(jax.ShapeDtypeStruct((mp, ncols), pdt))
            out_specs.append(pl.BlockSpec((ms, tn), lambda j, i: (0, j)))
            out_shapes.append(jax.ShapeDtypeStruct((ms, ncols), sdt))
    kern = functools.partial(
        _proj_kernel, epilogues=tuple(epilogues),
        out_head_major=tuple(hm for _, _, hm in outs),
        n_aux=len(aux), n_prompt_tiles=npt, sections=sections, n_chunks=n_chunks)
    return pl.pallas_call(
        kern,
        grid=(nj, npt + 1),
        in_specs=in_specs,
        out_specs=out_specs,
        out_shape=out_shapes,
        scratch_shapes=[pltpu.VMEM((2, kdim, tn), BF16),
                        pltpu.VMEM((2, kc, tn), F32),
                        pltpu.SemaphoreType.DMA((2,))],
        compiler_params=_params(("arbitrary", "arbitrary")),
        name=name,
    )(xp, xs, w3, *aux)


def _ep_gelu(acc):
    return (jax.nn.gelu(acc, approximate=True),)


def _ep_silu(acc):
    return (acc * jax.nn.sigmoid(acc),)


def _ep_sigmoid(acc):
    return (jax.nn.sigmoid(acc),)


def _ep_identity(acc):
    return (acc,)


def _ep_forget(acc, lbl):
    m = jnp.max(lbl, axis=0, keepdims=True)
    e = jnp.exp(lbl - m)
    lb = e[0:1, :] / jnp.sum(e, axis=0, keepdims=True)
    sig = jax.nn.sigmoid(acc)
    log2_f = jnp.log(lb + (1.0 - lb) * sig) * LOG2_E
    k = (1.0 - lb) * (1.0 - sig)
    return log2_f, k


def _layernorm(gv, g, b):
    mu = jnp.mean(gv, axis=-1, keepdims=True)
    xc = gv - mu
    var = jnp.mean(xc * xc, axis=-1, keepdims=True)
    return xc * lax.rsqrt(var + EPS) * g + b


def _gate_kernel(u_ref, gv_ref, sza_ref, lng_ref, lnb_ref, ws_ref, bs_ref,
                 ya_ref, vr_ref, wm_ref, *, n_steps, n_groups):
    n = pl.program_id(1)

    @pl.when((pl.program_id(0) == 0) & (n == 0))
    def _():
        r = lax.broadcasted_iota(jnp.int32, (CHUNK_A, CHUNK_A), 0)
        c = lax.broadcasted_iota(jnp.int32, (CHUNK_A, CHUNK_A), 1)
        for g in range(n_groups):
            wm_ref[g] = jnp.where(c <= r, ws_ref[g], 0.0).astype(BF16)

    step_chunks = gv_ref.shape[0] // CHUNK_A
    for cc in range(step_chunks):
        rs = slice(cc * CHUNK_A, (cc + 1) * CHUNK_A)
        vn = _layernorm(gv_ref[rs, :].astype(F32), lng_ref[...], lnb_ref[...])

        if cc == step_chunks - 1:
            @pl.when(n == n_steps - 1)
            def _():
                vr_ref[...] = vn

        vnb = vn.astype(BF16)
        for g in range(n_groups):
            sl = slice(g * LANES, (g + 1) * LANES)
            mixed = jnp.dot(wm_ref[g], vnb[:, sl], preferred_element_type=F32) + bs_ref[:, sl]
            ya_ref[rs, sl] = (u_ref[rs, sl].astype(F32) * mixed
                              * sza_ref[rs, sl].astype(F32)).astype(ya_ref.dtype)


def _gate_prompt(uv, sza, ln_g, ln_b, w_s, bs_rows, batch, seq, e_a):
    rows = GATE_STEP_CHUNKS * CHUNK_A
    assert seq % rows == 0
    n_steps = seq // rows
    n_groups = w_s.shape[0]
    blk = lambda b, n: (b * n_steps + n, 0)
    kern = functools.partial(_gate_kernel, n_steps=n_steps, n_groups=n_groups)
    return pl.pallas_call(
        kern,
        grid=(batch, n_steps),
        in_specs=[
            pl.BlockSpec((rows, e_a), blk),
            pl.BlockSpec((rows, e_a), lambda b, n: (b * n_steps + n, 1)),
            pl.BlockSpec((rows, e_a), blk),
            pl.BlockSpec((1, e_a), lambda b, n: (0, 0)),
            pl.BlockSpec((1, e_a), lambda b, n: (0, 0)),
            pl.BlockSpec(w_s.shape, lambda b, n: (0, 0, 0)),
            pl.BlockSpec((CHUNK_A, e_a), lambda b, n: (0, 0)),
        ],
        out_specs=[
            pl.BlockSpec((rows, e_a), blk),
            pl.BlockSpec((None, None, CHUNK_A, e_a), lambda b, n: (0, b, 0, 0)),
        ],
        out_shape=[
            jax.ShapeDtypeStruct((batch * seq, e_a), BF16),
            jax.ShapeDtypeStruct((1, batch, CHUNK_A, e_a), F32),
        ],
        scratch_shapes=[pltpu.VMEM((n_groups, CHUNK_A, CHUNK_A), BF16)],
        compiler_params=_params(("arbitrary", "arbitrary")),
        name="spatial_gate_prompt",
    )(uv, uv, sza, ln_g, ln_b, w_s, bs_rows)


def _gate_sample_kernel(u_ref, gv_ref, sza_ref, lng_ref, lnb_ref, wd_ref, b0_ref,
                        ya_ref, vr_ref):
    vn = _layernorm(gv_ref[...].astype(F32), lng_ref[...], lnb_ref[...])
    vr_ref[...] = vn
    mixed = wd_ref[...] * vn + b0_ref[...]
    ya_ref[...] = (u_ref[...].astype(F32) * mixed * sza_ref[...].astype(F32)).astype(ya_ref.dtype)


def _gate_sample(uv, sza, ln_g, ln_b, wd_row, b0_row, e_a):
    m = sza.shape[0]
    full = lambda i: (0, 0)
    return pl.pallas_call(
        _gate_sample_kernel,
        grid=(1,),
        in_specs=[
            pl.BlockSpec((m, e_a), full),
            pl.BlockSpec((m, e_a), lambda i: (0, 1)),
            pl.BlockSpec((m, e_a), full),
            pl.BlockSpec((1, e_a), full),
            pl.BlockSpec((1, e_a), full),
            pl.BlockSpec((1, e_a), full),
            pl.BlockSpec((1, e_a), full),
        ],
        out_specs=[pl.BlockSpec((m, e_a), full), pl.BlockSpec((m, e_a), full)],
        out_shape=[jax.ShapeDtypeStruct((m, e_a), BF16),
                   jax.ShapeDtypeStruct((m, e_a), F32)],
        compiler_params=_params(("arbitrary",)),
        name="spatial_gate_sample",
    )(uv, uv, sza, ln_g, ln_b, wd_row, b0_row)


def _dot_nt(a, b):
    return lax.dot_general(a, b, (((1,), (1,)), ((), ())), preferred_element_type=F32)


def _dot_tn(a, b):
    return lax.dot_general(a, b, (((0,), (0,)), ((), ())), preferred_element_type=F32)


def _hgrn_kernel(q_ref, lf_ref, k_ref, i_ref, szb_ref, gon_ref, yb_ref, st_ref,
                 s_scr, o_scr, *att_scrs, n_steps, n_heads):
    n = pl.program_id(1)
    c, hd, sb = CHUNK_B, HEAD_DIM, SUBLANES
    nv = c // sb

    @pl.when((pl.program_id(0) == 0) & (n == 0))
    def _():
        for att_scr in att_scrs:
            att_scr[...] = jnp.zeros_like(att_scr)

    @pl.when(n == 0)
    def _():
        s_scr[...] = jnp.zeros_like(s_scr)

    r8 = lax.broadcasted_iota(jnp.int32, (1, sb, hd), 1)
    t_idx = lax.broadcasted_iota(jnp.int32, (c, c), 0)
    s_idx = lax.broadcasted_iota(jnp.int32, (c, c), 1)
    fine = [h for h in (1, 2, 4) if 2 * h <= sb]
    coarse = [1 << l for l in range(sb.bit_length() - 1, (c // 2).bit_length())]
    second8 = {h: (r8 & (2 * h - 1)) >= h for h in fine}
    fine_mask = {}
    for h in fine:
        same = (t_idx >> h.bit_length()) == (s_idx >> h.bit_length())
        if h == 1:
            fine_mask[h] = same & (s_idx <= t_idx)
        else:
            fine_mask[h] = same & ((t_idx & (2 * h - 1)) >= h) & ((s_idx & (2 * h - 1)) < h)

    def tiles(x):
        return x.reshape(nv, sb, hd)

    def fine_boundary(p3, h):
        acc = None
        for blk in range(sb // (2 * h)):
            r = blk * 2 * h + h - 1
            row = jnp.broadcast_to(p3[:, r:r + 1, :], (nv, sb, hd))
            acc = row if acc is None else jnp.where(r8 >= blk * 2 * h, row, acc)
        return acc

    def one_head(h, rows, att_ref):
        lf = lf_ref[h, rows, :]
        q = q_ref[h, rows, :]
        k = k_ref[h, rows, :]
        ib = i_ref[h, rows, :].astype(BF16)
        q3, k3, lf3 = tiles(q), tiles(k), tiles(lf)

        odd = second8[1]
        qf = jnp.where(odd, q3 * jnp.exp2(lf3), q3).reshape(c, hd).astype(BF16)
        kf = jnp.where(odd, k3 * jnp.exp2(-lf3), k3).reshape(c, hd).astype(BF16)
        att_f = jnp.where(fine_mask[1], _dot_nt(qf, kf), 0.0)
        p3 = lf3 + jnp.where(odd, tiles(pltpu.roll(lf, 1, 0)), 0.0)
        yield
        for hh in fine[1:]:
            bnd = fine_boundary(p3, hh)
            sec = second8[hh]
            g = jnp.where(sec, p3, bnd - p3)
            x = (jnp.where(sec, q3, k3) * jnp.exp2(g)).reshape(c, hd).astype(BF16)
            att_f = jnp.where(fine_mask[hh], _dot_nt(x, x), att_f)
            p3 = p3 + jnp.where(sec, bnd, 0.0)
            yield
        for v in range(nv):
            blk = slice(v * sb, (v + 1) * sb)
            att_ref[blk, blk] = att_f[blk, blk]

        p = p3.reshape(c, hd)
        for hh in coarse:
            xs, ps = [], []
            for blk in range(c // (2 * hh)):
                lo, mid, hi = blk * 2 * hh, blk * 2 * hh + hh, (blk + 1) * 2 * hh
                bnd = jnp.broadcast_to(p[mid - 1:mid, :], (hh, hd))
                p_lo, p_hi = p[lo:mid], p[mid:hi]
                xs.append(k[lo:mid] * jnp.exp2(bnd - p_lo))
                xs.append(q[mid:hi] * jnp.exp2(p_hi))
                ps.append(p_lo)
                ps.append(p_hi + bnd)
            x = jnp.concatenate(xs, axis=0).astype(BF16)
            a = _dot_nt(x, x)
            for blk in range(c // (2 * hh)):
                lo, mid, hi = blk * 2 * hh, blk * 2 * hh + hh, (blk + 1) * 2 * hh
                att_ref[mid:hi, lo:mid] = a[mid:hi, lo:mid]
            p = jnp.concatenate(ps, axis=0)
            yield

        b_last = p[c - 1:c, :]
        qt = (q * jnp.exp2(p)).astype(BF16)
        kt = (k * jnp.exp2(jnp.broadcast_to(b_last, (c, hd)) - p)).astype(BF16)
        dec = jnp.exp2(b_last)
        s_old = s_scr[h]
        lhs = jnp.concatenate([att_ref[...].astype(BF16), qt], axis=1)
        rhs = jnp.concatenate([ib, s_old.astype(BF16)], axis=0)
        o_scr[h] = jnp.dot(lhs, rhs, preferred_element_type=F32)
        dec_col = jnp.transpose(jnp.broadcast_to(dec, (hd, hd)))
        s_scr[h] = dec_col * s_old + _dot_tn(kt, ib)

    def chunk_body(cc, chunk_carry):
        rows = pl.ds(pl.multiple_of(cc * c, c), c)

        def head_group(grp, carry):
            heads = [one_head(grp * HEAD_UNROLL + u, rows, att_scrs[u])
                     for u in range(HEAD_UNROLL)]
            for _ in itertools.zip_longest(*heads):
                pass
            return carry

        lax.fori_loop(0, n_heads // HEAD_UNROLL, head_group, 0)

        for h in range(n_heads):
            sl = slice(h * hd, (h + 1) * hd)
            o = o_scr[h]
            ms = jnp.mean(o * o, axis=-1, keepdims=True)
            yb_ref[rows, sl] = (o * lax.rsqrt(ms + EPS) * gon_ref[:, sl]
                                * szb_ref[rows, sl].astype(F32)).astype(yb_ref.dtype)
        return chunk_carry

    lax.fori_loop(0, q_ref.shape[1] // c, chunk_body, 0)

    @pl.when(n == n_steps - 1)
    def _():
        st_ref[...] = s_scr[...]


def _hgrn_prompt(qi, lf, k, zz, g_onorm, batch, seq):
    n_heads = lf.shape[0]
    e_b = n_heads * HEAD_DIM
    rows = HGRN_STEP_CHUNKS * CHUNK_B
    assert seq % rows == 0
    n_steps = seq // rows
    hm = pl.BlockSpec((n_heads, rows, HEAD_DIM), lambda b, n: (0, b * n_steps + n, 0))
    hm_i = pl.BlockSpec((n_heads, rows, HEAD_DIM), lambda b, n: (1, b * n_steps + n, 0))
    rm = pl.BlockSpec((rows, e_b), lambda b, n: (b * n_steps + n, 0))
    rm_zb = pl.BlockSpec((rows, e_b), lambda b, n: (b * n_steps + n, 1))
    kern = functools.partial(_hgrn_kernel, n_steps=n_steps, n_heads=n_heads)
    q, iv, szb = qi, qi, zz
    return pl.pallas_call(
        kern,
        grid=(batch, n_steps),
        in_specs=[hm, hm, hm, hm_i, rm_zb, pl.BlockSpec((1, e_b), lambda b, n: (0, 0))],
        out_specs=[
            rm,
            pl.BlockSpec((None, None, n_heads, HEAD_DIM, HEAD_DIM),
                         lambda b, n: (0, b, 0, 0, 0)),
        ],
        out_shape=[
            jax.ShapeDtypeStruct((batch * seq, e_b), BF16),
            jax.ShapeDtypeStruct((1, batch, n_heads, HEAD_DIM, HEAD_DIM), F32),
        ],
        scratch_shapes=[
            pltpu.VMEM((n_heads, HEAD_DIM, HEAD_DIM), F32),
            pltpu.VMEM((n_heads, CHUNK_B, HEAD_DIM), F32),
        ] + [pltpu.VMEM((CHUNK_B, CHUNK_B), F32) for _ in range(HEAD_UNROLL)],
        compiler_params=_params(("arbitrary", "arbitrary")),
        name="hgrn_prompt",
    )(q, lf, k, iv, szb, g_onorm)


def _hgrn_sample_kernel(q_ref, lf_ref, k_ref, i_ref, st_ref, szb_ref, gon_ref,
                        sto_ref, yb_ref, ft_scr, kt_scr, o_scr, *, n_heads, nb):
    s = pl.program_id(0)
    hd = HEAD_DIM

    @pl.when(s == 0)
    def _():
        def tb(h, carry):
            ft_scr[h] = jnp.transpose(jnp.exp2(lf_ref[h]))
            kt_scr[h] = jnp.transpose(k_ref[h])
            return carry
        lax.fori_loop(0, n_heads, tb, 0)

    n_seq = lf_ref.shape[1]
    shift = jnp.where(s == 0, 0, n_seq - s * nb)

    def one_head(h):
        fr = pltpu.roll(ft_scr[h], shift, 1)
        kr = pltpu.roll(kt_scr[h], shift, 1)
        for j in range(nb):
            fb = jnp.broadcast_to(fr[:, j:j + 1], (hd, hd))
            kb = jnp.broadcast_to(kr[:, j:j + 1], (hd, hd))
            irow = i_ref[h, pl.ds(s * nb + j, 1), :]
            s_new = fb * st_ref[j, h] + kb * irow
            sto_ref[j, h] = s_new
            qrow = jnp.broadcast_to(q_ref[h, pl.ds(s * nb + j, 1), :], (SUBLANES, hd))
            o = jnp.dot(qrow.astype(BF16), s_new.astype(BF16), preferred_element_type=F32)
            o_scr[h, pl.ds(j, 1), :] = o[0:1, :]
            yield

    def head_group(grp, carry):
        heads = [one_head(grp * SAMPLE_HEAD_UNROLL + u) for u in range(SAMPLE_HEAD_UNROLL)]
        for _ in itertools.zip_longest(*heads):
            pass
        return carry

    lax.fori_loop(0, n_heads // SAMPLE_HEAD_UNROLL, head_group, 0)

    for h in range(n_heads):
        sl = slice(h * hd, (h + 1) * hd)
        o = o_scr[h]
        ms = jnp.mean(o * o, axis=-1, keepdims=True)
        yb_ref[:, sl] = (o * lax.rsqrt(ms + EPS) * gon_ref[:, sl] * szb_ref[:, sl])


def _hgrn_sample(qi, lf, k, state, zz, g_onorm):
    n_heads, n_seq, hd = lf.shape
    nb = SAMPLE_SEQ_BLOCK
    e_b = n_heads * hd
    full3 = pl.BlockSpec((n_heads, n_seq, hd), lambda s: (0, 0, 0))
    full3_i = pl.BlockSpec((n_heads, n_seq, hd), lambda s: (1, 0, 0))
    st_spec = pl.BlockSpec((None, nb, n_heads, hd, hd), lambda s: (0, s, 0, 0, 0))
    kern = functools.partial(_hgrn_sample_kernel, n_heads=n_heads, nb=nb)
    q, iv, szb = qi, qi, zz
    return pl.pallas_call(
        kern,
        grid=(n_seq // nb,),
        in_specs=[full3, full3, full3, full3_i, st_spec,
                  pl.BlockSpec((nb, e_b), lambda s: (s, 1)),
                  pl.BlockSpec((1, e_b), lambda s: (0, 0))],
        out_specs=[st_spec, pl.BlockSpec((nb, e_b), lambda s: (s, 0))],
        out_shape=[jax.ShapeDtypeStruct(state.shape, F32),
                   jax.ShapeDtypeStruct((n_seq, e_b), F32)],
        scratch_shapes=[
            pltpu.VMEM((n_heads, hd, n_seq), F32),
            pltpu.VMEM((n_heads, hd, n_seq), F32),
            pltpu.VMEM((n_heads, nb, hd), F32),
        ],
        compiler_params=_params(("arbitrary",)),
        name="hgrn_sample",
    )(q, lf, k, iv, state, szb, g_onorm)


def _merge_kernel(yap_ref, ybp_ref, gap_ref, gbp_ref, yas_ref, ybs_ref, gas_ref, gbs_ref,
                  wpa_ref, wpb_ref, mp_ref, ms_ref, wa_bf, wb_bf, *, n_prompt_tiles):
    i = pl.program_id(1)

    @pl.when(i == 0)
    def _():
        wa_bf[...] = wpa_ref[...].astype(BF16)
        wb_bf[...] = wpb_ref[...].astype(BF16)

    def run(ya, yb, ga, gb, out):
        rows = ya.shape[0]
        sub = min(rows, SUB_ROWS)
        for m in range(rows // sub):
            rs = slice(m * sub, (m + 1) * sub)
            a = jnp.dot(ya[rs, :].astype(BF16), wa_bf[...], preferred_element_type=F32)
            b = jnp.dot(yb[rs, :].astype(BF16), wb_bf[...], preferred_element_type=F32)
            out[rs, :] = (ga[rs, :].astype(F32) * a + gb[rs, :].astype(F32) * b).astype(out.dtype)

    @pl.when(i != _sample_step(n_prompt_tiles))
    def _():
        run(yap_ref, ybp_ref, gap_ref, gbp_ref, mp_ref)

    @pl.when(i == _sample_step(n_prompt_tiles))
    def _():
        run(yas_ref, ybs_ref, gas_ref, gbs_ref, ms_ref)


def _merge(ya_p, yb_p, g_p, ya_s, yb_s, g_s, w_pa, w_pb, gate_col0):
    mp, kdim = ya_p.shape
    ms = ya_s.shape[0]
    d = w_pa.shape[-1]
    tm, tn = ROW_TILE, COL_TILE
    npt = mp // tm
    nj = d // tn
    assert gate_col0 % tn == 0
    g0 = gate_col0 // tn

    def row_i(j, i):
        return _prompt_tile(j, i, npt)

    yp = pl.BlockSpec((tm, kdim), lambda j, i: (row_i(j, i), 0))
    ys = pl.BlockSpec((ms, kdim), lambda j, i: (0, 0))
    w = pl.BlockSpec((None, kdim, tn), lambda j, i: (0, 0, j))
    kern = functools.partial(_merge_kernel, n_prompt_tiles=npt)
    return pl.pallas_call(
        kern,
        grid=(nj, npt + 1),
        in_specs=[
            yp, yp,
            pl.BlockSpec((tm, tn), lambda j, i: (row_i(j, i), g0 + j)),
            pl.BlockSpec((tm, tn), lambda j, i: (row_i(j, i), g0 + nj + j)),
            ys, ys,
            pl.BlockSpec((ms, tn), lambda j, i: (0, g0 + j)),
            pl.BlockSpec((ms, tn), lambda j, i: (0, g0 + nj + j)),
            w, w,
        ],
        out_specs=[pl.BlockSpec((tm, tn), lambda j, i: (row_i(j, i), j)),
                   pl.BlockSpec((ms, tn), lambda j, i: (0, j))],
        out_shape=[jax.ShapeDtypeStruct((mp, d), BF16),
                   jax.ShapeDtypeStruct((ms, d), BF16)],
        scratch_shapes=[pltpu.VMEM((kdim, tn), BF16), pltpu.VMEM((kdim, tn), BF16)],
        compiler_params=_params(("arbitrary", "arbitrary")),
        name="gated_merge",
    )(ya_p, yb_p, g_p, g_p, ya_s, yb_s, g_s, g_s, w_pa, w_pb)


def _cast_kernel(w_ref, o_ref):
    o_ref[...] = w_ref[...].astype(o_ref.dtype)


def _cast_bf16(w3, rows):
    _, k, n = w3.shape
    return pl.pallas_call(
        _cast_kernel,
        grid=(k // rows,),
        in_specs=[pl.BlockSpec((None, rows, n), lambda i: (0, i, 0))],
        out_specs=pl.BlockSpec((rows, n), lambda i: (i, 0)),
        out_shape=jax.ShapeDtypeStruct((k, n), BF16),
        compiler_params=_params(("arbitrary",)),
        name="cast_w_out",
    )(w3)


def _out_post_kernel(mp_ref, xp_ref, ms_ref, xs_ref, w_ref, g_ref, op_ref, os_ref, z_scr,
                     *, n_prompt_tiles):
    s = pl.program_id(0)

    def matmul(m_ref):
        z_scr[...] = jnp.dot(m_ref[...], w_ref[...], preferred_element_type=F32)

    def finish(x_ref, o_ref):
        z = z_scr[...]
        ms = jnp.mean(z * z, axis=-1, keepdims=True)
        o_ref[...] = x_ref[...] + z * lax.rsqrt(ms + EPS) * g_ref[...]

    @pl.when(s == 0)
    def _():
        matmul(mp_ref)

    @pl.when((s >= 1) & (s < n_prompt_tiles))
    def _():
        finish(xp_ref, op_ref)
        matmul(mp_ref)

    @pl.when(s == n_prompt_tiles)
    def _():
        finish(xp_ref, op_ref)
        matmul(ms_ref)

    @pl.when(s == n_prompt_tiles + 1)
    def _():
        finish(xs_ref, os_ref)


def _out_post(m_p, x_p, m_s, x_s, w_bf, g):
    mp, d = x_p.shape
    ms = x_s.shape[0]
    tm = OUT_ROWS
    assert ms == tm
    npt = mp // tm
    mm_row = pl.BlockSpec((tm, d), lambda s: (jnp.minimum(s, npt - 1), 0))
    fin_row = pl.BlockSpec((tm, d), lambda s: (jnp.clip(s - 1, 0, npt - 1), 0))
    full_s = pl.BlockSpec((ms, d), lambda s: (0, 0))
    kern = functools.partial(_out_post_kernel, n_prompt_tiles=npt)
    return pl.pallas_call(
        kern,
        grid=(npt + 2,),
        in_specs=[mm_row, fin_row, full_s, full_s,
                  pl.BlockSpec(w_bf.shape, lambda s: (0, 0)),
                  pl.BlockSpec((1, d), lambda s: (0, 0))],
        out_specs=[fin_row, full_s],
        out_shape=[jax.ShapeDtypeStruct((mp, d), F32), jax.ShapeDtypeStruct((ms, d), F32)],
        scratch_shapes=[pltpu.VMEM((tm, d), F32)],
        compiler_params=_params(("arbitrary",)),
        name="out_proj_post_norm",
    )(m_p, x_p, m_s, x_s, w_bf, g)


def kernel(x_prompt, x_sample, state_hgrn, lb_logits, g_pre, w_in, ln_g, ln_b, w_s, b_s,
           g_onorm, w_pa, w_pb, w_o, g_post):
    batch, seq, d = x_prompt.shape
    n_seq, dec_seq, _ = x_sample.shape
    depth = w_in.shape[0]
    assert depth == 1 and dec_seq == 1
    assert seq % CHUNK_A == 0 and seq % CHUNK_B == 0
    e = w_pa.shape[1]
    n_groups = w_s.shape[1]
    assert n_groups * CHUNK_A == e

    xp = x_prompt.reshape(batch * seq, d)
    xs = x_sample.reshape(n_seq, d)
    xn_p = _rmsnorm(xp, g_pre, NORM_ROWS)
    xn_s = _rmsnorm(xs, g_pre, n_seq)

    col = lambda idx: idx * e
    uvg_p, uvg_s = _proj(xn_p, xn_s, w_in, [(col(0), 2 * e), (col(7), 2 * d)],
                         [_ep_gelu, _ep_sigmoid],
                         [(BF16, BF16, False)], name="proj_uv_gates")
    zz_p, zz_s = _proj(xn_p, xn_s, w_in, [(col(2), e), (col(6), e)], [_ep_silu, _ep_silu],
                       [(BF16, F32, False)], name="proj_za_zb")
    qi_p, qi_s = _proj(xn_p, xn_s, w_in, [(col(3), e), (col(5), e)],
                       [_ep_silu, _ep_identity],
                       [(F32, F32, True)], name="proj_q_i")
    lf_p, lf_s, k_p, k_s = _proj(xn_p, xn_s, w_in, [(col(4), e)], [_ep_forget],
                                 [(F32, F32, True), (F32, F32, True)],
                                 aux=(lb_logits,), tn=COL_TILE, name="proj_f")

    bs_rows = jnp.repeat(b_s[0].T, CHUNK_A, axis=1)
    ya_p, vrows_p = _gate_prompt(uvg_p, zz_p, ln_g, ln_b, w_s[0], bs_rows, batch, seq, e)
    wd_row = jnp.repeat(w_s[0, :, 0, 0], CHUNK_A)[None, :]
    b0_row = jnp.repeat(b_s[0, :, 0], CHUNK_A)[None, :]
    ya_s, vrows_s = _gate_sample(uvg_s, zz_s, ln_g, ln_b, wd_row, b0_row, e)

    yb_p, state_p = _hgrn_prompt(qi_p, lf_p, k_p, zz_p, g_onorm, batch, seq)
    state_s, yb_s = _hgrn_sample(qi_s, lf_s, k_s, state_hgrn, zz_s, g_onorm)

    m_p, m_s = _merge(ya_p, yb_p, uvg_p, ya_s, yb_s, uvg_s, w_pa, w_pb, 2 * e)
    w_o_bf = _cast_bf16(w_o, CAST_ROWS)
    y_p, y_s = _out_post(m_p, xp, m_s, xs, w_o_bf, g_post)

    return (y_p.reshape(batch, seq, d), y_s.reshape(n_seq, 1, d),
            state_p, state_s,
            vrows_p, vrows_s.reshape(1, n_seq, 1, e))
```

```python
import functools
import itertools

import jax
import jax.numpy as jnp
from jax import lax
from jax.experimental import pallas as pl
from jax.experimental.pallas import tpu as pltpu

F32 = jnp.float32
BF16 = jnp.bfloat16

EPS = 1e-6
LOG2_E = 1.4426950408889634
LANES = 128
SUBLANES = 8
CHUNK_A = 128
GATE_STEP_CHUNKS = 4
HEAD_DIM = 128
CHUNK_B = 128
HGRN_STEP_CHUNKS = 4
HEAD_UNROLL = 8
VMEM_LIMIT_BYTES = 56 * 1024 * 1024
ROW_TILE = 1024
COL_TILE = 512
PROJ_COL_TILE = 1024
SUB_ROWS = 256
NORM_ROWS = 256
OUT_ROWS = 128
OUT_W_CHUNK = 64
SAMPLE_SEQ_BLOCK = 8
SAMPLE_HEAD_UNROLL = 4


def _params(sem):
    return pltpu.CompilerParams(dimension_semantics=sem,
                                vmem_limit_bytes=VMEM_LIMIT_BYTES)


def _rmsnorm_kernel(x_ref, g_ref, o_ref):
    x = x_ref[...]
    ms = jnp.mean(x * x, axis=-1, keepdims=True)
    o_ref[...] = (x * lax.rsqrt(ms + EPS) * g_ref[...]).astype(o_ref.dtype)


def _rmsnorm(x, g, rows):
    m, d = x.shape
    return pl.pallas_call(
        _rmsnorm_kernel,
        grid=(m // rows,),
        in_specs=[pl.BlockSpec((rows, d), lambda i: (i, 0)),
                  pl.BlockSpec((1, d), lambda i: (0, 0))],
        out_specs=pl.BlockSpec((rows, d), lambda i: (i, 0)),
        out_shape=jax.ShapeDtypeStruct((m, d), BF16),
        compiler_params=_params(("arbitrary",)),
        name="pre_rmsnorm",
    )(x, g)


def _sample_step(n_prompt_tiles):
    return n_prompt_tiles // 2


def _prompt_tile(j, i, n_prompt_tiles):
    fwd = i - (i > _sample_step(n_prompt_tiles)).astype(jnp.int32)
    return jnp.where(j % 2 == 0, fwd, n_prompt_tiles - 1 - fwd)


def _proj_kernel(*refs, epilogues, out_head_major, n_aux, n_prompt_tiles, sections,
                 n_chunks):
    n_out = len(out_head_major)
    xp_ref, xs_ref, w_hbm = refs[:3]
    aux_refs = refs[3:3 + n_aux]
    out_refs = refs[3 + n_aux:3 + n_aux + 2 * n_out]
    wbf_ref, stage_ref, sem = refs[-3:]
    j = pl.program_id(0)
    i = pl.program_id(1)
    n_col_tiles = sum(count for _, count in sections)
    cur = j % 2
    _, kc, tn = stage_ref.shape

    def weight_col_tile(jj):
        tile, start = None, 0
        for first, count in sections:
            t = first + (jj - start)
            tile = t if tile is None else jnp.where(jj >= start, t, tile)
            start += count
        return tile

    def chunk_copy(jj, c, slot):
        return pltpu.make_async_copy(
            w_hbm.at[0, pl.ds(c * kc, kc), pl.ds(weight_col_tile(jj) * tn, tn)],
            stage_ref.at[slot], sem.at[slot])

    @pl.when((j == 0) & (i == 0))
    def _():
        chunk_copy(0, 0, 0).start()
        for c in range(n_chunks):
            if c + 1 < n_chunks:
                chunk_copy(0, c + 1, (c + 1) % 2).start()
            chunk_copy(0, c, c % 2).wait()
            wbf_ref[0, c * kc:(c + 1) * kc, :] = stage_ref[c % 2].astype(BF16)

    has_next = j + 1 < n_col_tiles

    @pl.when(has_next & (i >= 1))
    def _():
        chunk_copy(j + 1, i - 1, (i - 1) % 2).wait()

    @pl.when(has_next & (i < n_chunks))
    def _():
        chunk_copy(j + 1, i, i % 2).start()

    def cast_previous_chunk():
        prev = (i + n_chunks - 1) % n_chunks
        wbf_ref[1 - cur, pl.ds(prev * kc, kc), :] = stage_ref[(i + 1) % 2].astype(BF16)

    def run(x_ref, outs, epilogue):
        cast_previous_chunk()
        rows = x_ref.shape[0]
        sub = min(rows, SUB_ROWS)
        for m in range(rows // sub):
            rs = slice(m * sub, (m + 1) * sub)
            acc = jnp.dot(x_ref[rs, :].astype(BF16), wbf_ref[cur],
                          preferred_element_type=F32)
            res = epilogue(acc, *[a[...] for a in aux_refs])
            for r, o_ref, hm in zip(res, outs, out_head_major):
                if hm:
                    for hh in range(o_ref.shape[0]):
                        o_ref[hh, rs, :] = r[:, hh * LANES:(hh + 1) * LANES].astype(o_ref.dtype)
                else:
                    o_ref[rs, :] = r.astype(o_ref.dtype)

    is_sample = i == _sample_step(n_prompt_tiles)
    start = 0
    for (_, count), epilogue in zip(sections, epilogues):
        in_section = (j >= start) & (j < start + count)
        start += count

        @pl.when(in_section & jnp.logical_not(is_sample))
        def _(epilogue=epilogue):
            run(xp_ref, out_refs[0::2], epilogue)

        @pl.when(in_section & is_sample)
        def _(epilogue=epilogue):
            run(xs_ref, out_refs[1::2], epilogue)


def _proj(xp, xs, w3, col_sections, epilogues, outs, aux=(), tn=None, name="proj"):
    mp, kdim = xp.shape
    ms = xs.shape[0]
    tm = ROW_TILE
    tn = PROJ_COL_TILE if tn is None else tn
    npt = mp // tm
    assert all(c0 % tn == 0 and w % tn == 0 for c0, w in col_sections)
    sections = tuple((c0 // tn, w // tn) for c0, w in col_sections)
    ncols = sum(w for _, w in col_sections)
    nj = ncols // tn
    hpt = tn // LANES
    n_chunks = npt
    assert kdim % n_chunks == 0 and n_chunks >= 2
    kc = kdim // n_chunks

    def row_i(j, i):
        return _prompt_tile(j, i, npt)

    in_specs = [
        pl.BlockSpec((tm, kdim), lambda j, i: (row_i(j, i), 0)),
        pl.BlockSpec((ms, kdim), lambda j, i: (0, 0)),
        pl.BlockSpec(memory_space=pl.ANY),
    ]
    for a in aux:
        in_specs.append(pl.BlockSpec((a.shape[0], tn), lambda j, i: (0, j)))
    out_specs, out_shapes = [], []
    for pdt, sdt, hm in outs:
        if hm:
            out_specs.append(pl.BlockSpec((hpt, tm, LANES), lambda j, i: (j, row_i(j, i), 0)))
            out_shapes.append(jax.ShapeDtypeStruct((ncols // LANES, mp, LANES), pdt))
            out_specs.append(pl.BlockSpec((hpt, ms, LANES), lambda j, i: (j, 0, 0)))
            out_shapes.append(jax.ShapeDtypeStruct((ncols // LANES, ms, LANES), sdt))
        else:
            out_specs.append(pl.BlockSpec((tm, tn), lambda j, i: (row_i(j, i), j)))
            out_shapes.append(jax.ShapeDtypeStruct((mp, ncols), pdt))
            out_specs.append(pl.BlockSpec((ms, tn), lambda j, i: (0, j)))
            out_shapes.append(jax.ShapeDtypeStruct((ms, ncols), sdt))
    kern = functools.partial(
        _proj_kernel, epilogues=tuple(epilogues),
        out_head_major=tuple(hm for _, _, hm in outs),
        n_aux=len(aux), n_prompt_tiles=npt, sections=sections, n_chunks=n_chunks)
    return pl.pallas_call(
        kern,
        grid=(nj, npt + 1),
        in_specs=in_specs,
        out_specs=out_specs,
        out_shape=out_shapes,
        scratch_shapes=[pltpu.VMEM((2, kdim, tn), BF16),
                        pltpu.VMEM((2, kc, tn), F32),
                        pltpu.SemaphoreType.DMA((2,))],
        compiler_params=_params(("arbitrary", "arbitrary")),
        name=name,
    )(xp, xs, w3, *aux)


def _ep_gelu(acc):
    return (jax.nn.gelu(acc, approximate=True),)


def _ep_silu(acc):
    return (acc * jax.nn.sigmoid(acc),)


def _ep_sigmoid(acc):
    return (jax.nn.sigmoid(acc),)


def _ep_identity(acc):
    return (acc,)


def _ep_forget(acc, lbl):
    m = jnp.max(lbl, axis=0, keepdims=True)
    e = jnp.exp(lbl - m)
    lb = e[0:1, :] / jnp.sum(e, axis=0, keepdims=True)
    sig = jax.nn.sigmoid(acc)
    log2_f = jnp.log(lb + (1.0 - lb) * sig) * LOG2_E
    k = (1.0 - lb) * (1.0 - sig)
    return log2_f, k


def _layernorm(gv, g, b):
    mu = jnp.mean(gv, axis=-1, keepdims=True)
    xc = gv - mu
    var = jnp.mean(xc * xc, axis=-1, keepdims=True)
    return xc * lax.rsqrt(var + EPS) * g + b


def _gate_kernel(u_ref, gv_ref, sza_ref, lng_ref, lnb_ref, ws_ref, bs_ref,
                 ya_ref, vr_ref, wm_ref, *, n_steps, n_groups):
    n = pl.program_id(1)

    @pl.when((pl.program_id(0) == 0) & (n == 0))
    def _():
        r = lax.broadcasted_iota(jnp.int32, (CHUNK_A, CHUNK_A), 0)
        c = lax.broadcasted_iota(jnp.int32, (CHUNK_A, CHUNK_A), 1)
        for g in range(n_groups):
            wm_ref[g] = jnp.where(c <= r, ws_ref[g], 0.0).astype(BF16)

    step_chunks = gv_ref.shape[0] // CHUNK_A
    for cc in range(step_chunks):
        rs = slice(cc * CHUNK_A, (cc + 1) * CHUNK_A)
        vn = _layernorm(gv_ref[rs, :].astype(F32), lng_ref[...], lnb_ref[...])

        if cc == step_chunks - 1:
            @pl.when(n == n_steps - 1)
            def _():
                vr_ref[...] = vn

        vnb = vn.astype(BF16)
        for g in range(n_groups):
            sl = slice(g * LANES, (g + 1) * LANES)
            mixed = jnp.dot(wm_ref[g], vnb[:, sl], preferred_element_type=F32) + bs_ref[:, sl]
            ya_ref[rs, sl] = (u_ref[rs, sl].astype(F32) * mixed
                              * sza_ref[rs, sl].astype(F32)).astype(ya_ref.dtype)


def _gate_prompt(uv, sza, ln_g, ln_b, w_s, bs_rows, batch, seq, e_a):
    rows = GATE_STEP_CHUNKS * CHUNK_A
    assert seq % rows == 0
    n_steps = seq // rows
    n_groups = w_s.shape[0]
    blk = lambda b, n: (b * n_steps + n, 0)
    kern = functools.partial(_gate_kernel, n_steps=n_steps, n_groups=n_groups)
    return pl.pallas_call(
        kern,
        grid=(batch, n_steps),
        in_specs=[
            pl.BlockSpec((rows, e_a), blk),
            pl.BlockSpec((rows, e_a), lambda b, n: (b * n_steps + n, 1)),
            pl.BlockSpec((rows, e_a), blk),
            pl.BlockSpec((1, e_a), lambda b, n: (0, 0)),
            pl.BlockSpec((1, e_a), lambda b, n: (0, 0)),
            pl.BlockSpec(w_s.shape, lambda b, n: (0, 0, 0)),
            pl.BlockSpec((CHUNK_A, e_a), lambda b, n: (0, 0)),
        ],
        out_specs=[
            pl.BlockSpec((rows, e_a), blk),
            pl.BlockSpec((None, None, CHUNK_A, e_a), lambda b, n: (0, b, 0, 0)),
        ],
        out_shape=[
            jax.ShapeDtypeStruct((batch * seq, e_a), BF16),
            jax.ShapeDtypeStruct((1, batch, CHUNK_A, e_a), F32),
        ],
        scratch_shapes=[pltpu.VMEM((n_groups, CHUNK_A, CHUNK_A), BF16)],
        compiler_params=_params(("arbitrary", "arbitrary")),
        name="spatial_gate_prompt",
    )(uv, uv, sza, ln_g, ln_b, w_s, bs_rows)


def _gate_sample_kernel(u_ref, gv_ref, sza_ref, lng_ref, lnb_ref, wd_ref, b0_ref,
                        ya_ref, vr_ref):
    vn = _layernorm(gv_ref[...].astype(F32), lng_ref[...], lnb_ref[...])
    vr_ref[...] = vn
    mixed = wd_ref[...] * vn + b0_ref[...]
    ya_ref[...] = (u_ref[...].astype(F32) * mixed * sza_ref[...].astype(F32)).astype(ya_ref.dtype)


def _gate_sample(uv, sza, ln_g, ln_b, wd_row, b0_row, e_a):
    m = sza.shape[0]
    full = lambda i: (0, 0)
    return pl.pallas_call(
        _gate_sample_kernel,
        grid=(1,),
        in_specs=[
            pl.BlockSpec((m, e_a), full),
            pl.BlockSpec((m, e_a), lambda i: (0, 1)),
            pl.BlockSpec((m, e_a), full),
            pl.BlockSpec((1, e_a), full),
            pl.BlockSpec((1, e_a), full),
            pl.BlockSpec((1, e_a), full),
            pl.BlockSpec((1, e_a), full),
        ],
        out_specs=[pl.BlockSpec((m, e_a), full), pl.BlockSpec((m, e_a), full)],
        out_shape=[jax.ShapeDtypeStruct((m, e_a), BF16),
                   jax.ShapeDtypeStruct((m, e_a), F32)],
        compiler_params=_params(("arbitrary",)),
        name="spatial_gate_sample",
    )(uv, uv, sza, ln_g, ln_b, wd_row, b0_row)


def _dot_nt(a, b):
    return lax.dot_general(a, b, (((1,), (1,)), ((), ())), preferred_element_type=F32)


def _dot_tn(a, b):
    return lax.dot_general(a, b, (((0,), (0,)), ((), ())), preferred_element_type=F32)


def _hgrn_kernel(q_ref, lf_ref, k_ref, i_ref, szb_ref, gon_ref, yb_ref, st_ref,
                 s_scr, o_scr, *att_scrs, n_steps, n_heads):
    n = pl.program_id(1)
    c, hd, sb = CHUNK_B, HEAD_DIM, SUBLANES
    nv = c // sb

    @pl.when((pl.program_id(0) == 0) & (n == 0))
    def _():
        for att_scr in att_scrs:
            att_scr[...] = jnp.zeros_like(att_scr)

    @pl.when(n == 0)
    def _():
        s_scr[...] = jnp.zeros_like(s_scr)

    r8 = lax.broadcasted_iota(jnp.int32, (1, sb, hd), 1)
    t_idx = lax.broadcasted_iota(jnp.int32, (c, c), 0)
    s_idx = lax.broadcasted_iota(jnp.int32, (c, c), 1)
    fine = [h for h in (1, 2, 4) if 2 * h <= sb]
    coarse = [1 << l for l in range(sb.bit_length() - 1, (c // 2).bit_length())]
    second8 = {h: (r8 & (2 * h - 1)) >= h for h in fine}
    fine_mask = {}
    for h in fine:
        same = (t_idx >> h.bit_length()) == (s_idx >> h.bit_length())
        if h == 1:
            fine_mask[h] = same & (s_idx <= t_idx)
        else:
            fine_mask[h] = same & ((t_idx & (2 * h - 1)) >= h) & ((s_idx & (2 * h - 1)) < h)

    def tiles(x):
        return x.reshape(nv, sb, hd)

    def fine_boundary(p3, h):
        acc = None
        for blk in range(sb // (2 * h)):
            r = blk * 2 * h + h - 1
            row = jnp.broadcast_to(p3[:, r:r + 1, :], (nv, sb, hd))
            acc = row if acc is None else jnp.where(r8 >= blk * 2 * h, row, acc)
        return acc

    def one_head(h, rows, att_ref):
        lf = lf_ref[h, rows, :]
        q = q_ref[h, rows, :]
        k = k_ref[h, rows, :].astype(F32)
        ib = i_ref[h, rows, :].astype(BF16)
        q3, k3, lf3 = tiles(q), tiles(k), tiles(lf)

        odd = second8[1]
        qf = jnp.where(odd, q3 * jnp.exp2(lf3), q3).reshape(c, hd).astype(BF16)
        kf = jnp.where(odd, k3 * jnp.exp2(-lf3), k3).reshape(c, hd).astype(BF16)
        att_f = jnp.where(fine_mask[1], _dot_nt(qf, kf), 0.0)
        p3 = lf3 + jnp.where(odd, tiles(pltpu.roll(lf, 1, 0)), 0.0)
        yield
        for hh in fine[1:]:
            bnd = fine_boundary(p3, hh)
            sec = second8[hh]
            g = jnp.where(sec, p3, bnd - p3)
            x = (jnp.where(sec, q3, k3) * jnp.exp2(g)).reshape(c, hd).astype(BF16)
            att_f = jnp.where(fine_mask[hh], _dot_nt(x, x), att_f)
            p3 = p3 + jnp.where(sec, bnd, 0.0)
            yield
        for v in range(nv):
            blk = slice(v * sb, (v + 1) * sb)
            att_ref[blk, blk] = att_f[blk, blk]

        p = p3.reshape(c, hd)
        for hh in coarse:
            xs, ps = [], []
            for blk in range(c // (2 * hh)):
                lo, mid, hi = blk * 2 * hh, blk * 2 * hh + hh, (blk + 1) * 2 * hh
                bnd = jnp.broadcast_to(p[mid - 1:mid, :], (hh, hd))
                p_lo, p_hi = p[lo:mid], p[mid:hi]
                xs.append(k[lo:mid] * jnp.exp2(bnd - p_lo))
                xs.append(q[mid:hi] * jnp.exp2(p_hi))
                ps.append(p_lo)
                ps.append(p_hi + bnd)
            x = jnp.concatenate(xs, axis=0).astype(BF16)
            a = _dot_nt(x, x)
            for blk in range(c // (2 * hh)):
                lo, mid, hi = blk * 2 * hh, blk * 2 * hh + hh, (blk + 1) * 2 * hh
                att_ref[mid:hi, lo:mid] = a[mid:hi, lo:mid]
            p = jnp.concatenate(ps, axis=0)
            yield

        b_last = p[c - 1:c, :]
        qt = (q * jnp.exp2(p)).astype(BF16)
        kt = (k * jnp.exp2(jnp.broadcast_to(b_last, (c, hd)) - p)).astype(BF16)
        dec = jnp.exp2(b_last)
        s_old = s_scr[h]
        lhs = jnp.concatenate([att_ref[...].astype(BF16), qt], axis=1)
        rhs = jnp.concatenate([ib, s_old.astype(BF16)], axis=0)
        o_scr[h] = jnp.dot(lhs, rhs, preferred_element_type=F32)
        dec_col = jnp.transpose(jnp.broadcast_to(dec, (hd, hd)))
        s_scr[h] = dec_col * s_old + _dot_tn(kt, ib)

    def chunk_body(cc, chunk_carry):
        rows = pl.ds(pl.multiple_of(cc * c, c), c)

        def head_group(grp, carry):
            heads = [one_head(grp * HEAD_UNROLL + u, rows, att_scrs[u])
                     for u in range(HEAD_UNROLL)]
            for _ in itertools.zip_longest(*heads):
                pass
            return carry

        lax.fori_loop(0, n_heads // HEAD_UNROLL, head_group, 0)

        for h in range(n_heads):
            sl = slice(h * hd, (h + 1) * hd)
            o = o_scr[h]
            ms = jnp.mean(o * o, axis=-1, keepdims=True)
            yb_ref[rows, sl] = (o * lax.rsqrt(ms + EPS) * gon_ref[:, sl]
                                * szb_ref[rows, sl].astype(F32)).astype(yb_ref.dtype)
        return chunk_carry

    lax.fori_loop(0, q_ref.shape[1] // c, chunk_body, 0)

    @pl.when(n == n_steps - 1)
    def _():
        st_ref[...] = s_scr[...]


def _hgrn_prompt(qi, lf, k, zz, g_onorm, batch, seq):
    n_heads = lf.shape[0]
    e_b = n_heads * HEAD_DIM
    rows = HGRN_STEP_CHUNKS * CHUNK_B
    assert seq % rows == 0
    n_steps = seq // rows
    hm = pl.BlockSpec((n_heads, rows, HEAD_DIM), lambda b, n: (0, b * n_steps + n, 0))
    hm_i = pl.BlockSpec((n_heads, rows, HEAD_DIM), lambda b, n: (1, b * n_steps + n, 0))
    rm = pl.BlockSpec((rows, e_b), lambda b, n: (b * n_steps + n, 0))
    rm_zb = pl.BlockSpec((rows, e_b), lambda b, n: (b * n_steps + n, 1))
    kern = functools.partial(_hgrn_kernel, n_steps=n_steps, n_heads=n_heads)
    q, iv, szb = qi, qi, zz
    return pl.pallas_call(
        kern,
        grid=(batch, n_steps),
        in_specs=[hm, hm, hm, hm_i, rm_zb, pl.BlockSpec((1, e_b), lambda b, n: (0, 0))],
        out_specs=[
            rm,
            pl.BlockSpec((None, None, n_heads, HEAD_DIM, HEAD_DIM),
                         lambda b, n: (0, b, 0, 0, 0)),
        ],
        out_shape=[
            jax.ShapeDtypeStruct((batch * seq, e_b), BF16),
            jax.ShapeDtypeStruct((1, batch, n_heads, HEAD_DIM, HEAD_DIM), F32),
        ],
        scratch_shapes=[
            pltpu.VMEM((n_heads, HEAD_DIM, HEAD_DIM), F32),
            pltpu.VMEM((n_heads, CHUNK_B, HEAD_DIM), F32),
        ] + [pltpu.VMEM((CHUNK_B, CHUNK_B), F32) for _ in range(HEAD_UNROLL)],
        compiler_params=_params(("arbitrary", "arbitrary")),
        name="hgrn_prompt",
    )(q, lf, k, iv, szb, g_onorm)


def _hgrn_sample_kernel(q_ref, lf_ref, k_ref, i_ref, st_ref, szb_ref, gon_ref,
                        sto_ref, yb_ref, ft_scr, kt_scr, o_scr, *, n_heads, nb):
    s = pl.program_id(0)
    hd = HEAD_DIM

    @pl.when(s == 0)
    def _():
        def tb(h, carry):
            ft_scr[h] = jnp.transpose(jnp.exp2(lf_ref[h]))
            kt_scr[h] = jnp.transpose(k_ref[h])
            return carry
        lax.fori_loop(0, n_heads, tb, 0)

    n_seq = lf_ref.shape[1]
    shift = jnp.where(s == 0, 0, n_seq - s * nb)

    def one_head(h):
        fr = pltpu.roll(ft_scr[h], shift, 1)
        kr = pltpu.roll(kt_scr[h], shift, 1)
        for j in range(nb):
            fb = jnp.broadcast_to(fr[:, j:j + 1], (hd, hd))
            kb = jnp.broadcast_to(kr[:, j:j + 1], (hd, hd))
            irow = i_ref[h, pl.ds(s * nb + j, 1), :]
            s_new = fb * st_ref[j, h] + kb * irow
            sto_ref[j, h] = s_new
            qrow = jnp.broadcast_to(q_ref[h, pl.ds(s * nb + j, 1), :], (SUBLANES, hd))
            o = jnp.dot(qrow.astype(BF16), s_new.astype(BF16), preferred_element_type=F32)
            o_scr[h, pl.ds(j, 1), :] = o[0:1, :]
            yield

    def head_group(grp, carry):
        heads = [one_head(grp * SAMPLE_HEAD_UNROLL + u) for u in range(SAMPLE_HEAD_UNROLL)]
        for _ in itertools.zip_longest(*heads):
            pass
        return carry

    lax.fori_loop(0, n_heads // SAMPLE_HEAD_UNROLL, head_group, 0)

    for h in range(n_heads):
        sl = slice(h * hd, (h + 1) * hd)
        o = o_scr[h]
        ms = jnp.mean(o * o, axis=-1, keepdims=True)
        yb_ref[:, sl] = (o * lax.rsqrt(ms + EPS) * gon_ref[:, sl] * szb_ref[:, sl])


def _hgrn_sample(qi, lf, k, state, zz, g_onorm):
    n_heads, n_seq, hd = lf.shape
    nb = SAMPLE_SEQ_BLOCK
    e_b = n_heads * hd
    full3 = pl.BlockSpec((n_heads, n_seq, hd), lambda s: (0, 0, 0))
    full3_i = pl.BlockSpec((n_heads, n_seq, hd), lambda s: (1, 0, 0))
    st_spec = pl.BlockSpec((None, nb, n_heads, hd, hd), lambda s: (0, s, 0, 0, 0))
    kern = functools.partial(_hgrn_sample_kernel, n_heads=n_heads, nb=nb)
    q, iv, szb = qi, qi, zz
    return pl.pallas_call(
        kern,
        grid=(n_seq // nb,),
        in_specs=[full3, full3, full3, full3_i, st_spec,
                  pl.BlockSpec((nb, e_b), lambda s: (s, 1)),
                  pl.BlockSpec((1, e_b), lambda s: (0, 0))],
        out_specs=[st_spec, pl.BlockSpec((nb, e_b), lambda s: (s, 0))],
        out_shape=[jax.ShapeDtypeStruct(state.shape, F32),
                   jax.ShapeDtypeStruct((n_seq, e_b), F32)],
        scratch_shapes=[
            pltpu.VMEM((n_heads, hd, n_seq), F32),
            pltpu.VMEM((n_heads, hd, n_seq), F32),
            pltpu.VMEM((n_heads, nb, hd), F32),
        ],
        compiler_params=_params(("arbitrary",)),
        name="hgrn_sample",
    )(q, lf, k, iv, state, szb, g_onorm)


def _merge_kernel(yap_ref, ybp_ref, gap_ref, gbp_ref, yas_ref, ybs_ref, gas_ref, gbs_ref,
                  wpa_ref, wpb_ref, mp_ref, ms_ref, wa_bf, wb_bf, *, n_prompt_tiles):
    i = pl.program_id(1)

    @pl.when(i == 0)
    def _():
        wa_bf[...] = wpa_ref[...].astype(BF16)
        wb_bf[...] = wpb_ref[...].astype(BF16)

    def run(ya, yb, ga, gb, out):
        rows = ya.shape[0]
        sub = min(rows, SUB_ROWS)
        for m in range(rows // sub):
            rs = slice(m * sub, (m + 1) * sub)
            a = jnp.dot(ya[rs, :].astype(BF16), wa_bf[...], preferred_element_type=F32)
            b = jnp.dot(yb[rs, :].astype(BF16), wb_bf[...], preferred_element_type=F32)
            out[rs, :] = (ga[rs, :].astype(F32) * a + gb[rs, :].astype(F32) * b).astype(out.dtype)

    @pl.when(i != _sample_step(n_prompt_tiles))
    def _():
        run(yap_ref, ybp_ref, gap_ref, gbp_ref, mp_ref)

    @pl.when(i == _sample_step(n_prompt_tiles))
    def _():
        run(yas_ref, ybs_ref, gas_ref, gbs_ref, ms_ref)


def _merge(ya_p, yb_p, g_p, ya_s, yb_s, g_s, w_pa, w_pb, gate_col0):
    mp, kdim = ya_p.shape
    ms = ya_s.shape[0]
    d = w_pa.shape[-1]
    tm, tn = ROW_TILE, COL_TILE
    npt = mp // tm
    nj = d // tn
    assert gate_col0 % tn == 0
    g0 = gate_col0 // tn

    def row_i(j, i):
        return _prompt_tile(j, i, npt)

    yp = pl.BlockSpec((tm, kdim), lambda j, i: (row_i(j, i), 0))
    ys = pl.BlockSpec((ms, kdim), lambda j, i: (0, 0))
    w = pl.BlockSpec((None, kdim, tn), lambda j, i: (0, 0, j))
    kern = functools.partial(_merge_kernel, n_prompt_tiles=npt)
    return pl.pallas_call(
        kern,
        grid=(nj, npt + 1),
        in_specs=[
            yp, yp,
            pl.BlockSpec((tm, tn), lambda j, i: (row_i(j, i), g0 + j)),
            pl.BlockSpec((tm, tn), lambda j, i: (row_i(j, i), g0 + nj + j)),
            ys, ys,
            pl.BlockSpec((ms, tn), lambda j, i: (0, g0 + j)),
            pl.BlockSpec((ms, tn), lambda j, i: (0, g0 + nj + j)),
            w, w,
        ],
        out_specs=[pl.BlockSpec((tm, tn), lambda j, i: (row_i(j, i), j)),
                   pl.BlockSpec((ms, tn), lambda j, i: (0, j))],
        out_shape=[jax.ShapeDtypeStruct((mp, d), BF16),
                   jax.ShapeDtypeStruct((ms, d), BF16)],
        scratch_shapes=[pltpu.VMEM((kdim, tn), BF16), pltpu.VMEM((kdim, tn), BF16)],
        compiler_params=_params(("arbitrary", "arbitrary")),
        name="gated_merge",
    )(ya_p, yb_p, g_p, g_p, ya_s, yb_s, g_s, g_s, w_pa, w_pb)


def _out_post_kernel(mp_ref, xp_ref, ms_ref, xs_ref, w_hbm, g_ref, op_ref, os_ref, z_scr,
                     w_ref, stage_ref, sem, *, n_prompt_tiles):
    s = pl.program_id(0)
    _, kc, _ = stage_ref.shape
    n_w_chunks = w_ref.shape[0] // kc

    def chunk_copy(c, slot):
        return pltpu.make_async_copy(w_hbm.at[0, pl.ds(c * kc, kc), :],
                                     stage_ref.at[slot], sem.at[slot])

    def matmul(m_ref):
        z_scr[...] = jnp.dot(m_ref[...], w_ref[...], preferred_element_type=F32)

    def finish(x_ref, o_ref):
        z = z_scr[...]
        ms = jnp.mean(z * z, axis=-1, keepdims=True)
        o_ref[...] = x_ref[...] + z * lax.rsqrt(ms + EPS) * g_ref[...]

    @pl.when(s == 0)
    def _():
        chunk_copy(0, 0).start()
        for c in range(n_w_chunks):
            if c + 1 < n_w_chunks:
                chunk_copy(c + 1, (c + 1) % 2).start()
            chunk_copy(c, c % 2).wait()
            w_ref[c * kc:(c + 1) * kc, :] = stage_ref[c % 2].astype(BF16)
        matmul(mp_ref)

    @pl.when((s >= 1) & (s < n_prompt_tiles))
    def _():
        finish(xp_ref, op_ref)
        matmul(mp_ref)

    @pl.when(s == n_prompt_tiles)
    def _():
        finish(xp_ref, op_ref)
        matmul(ms_ref)

    @pl.when(s == n_prompt_tiles + 1)
    def _():
        finish(xs_ref, os_ref)


def _out_post(m_p, x_p, m_s, x_s, w3, g):
    mp, d = x_p.shape
    ms = x_s.shape[0]
    kdim = w3.shape[1]
    tm = OUT_ROWS
    assert ms == tm
    assert kdim % OUT_W_CHUNK == 0 and kdim // OUT_W_CHUNK >= 2
    npt = mp // tm
    mm_row = pl.BlockSpec((tm, d), lambda s: (jnp.minimum(s, npt - 1), 0))
    fin_row = pl.BlockSpec((tm, d), lambda s: (jnp.clip(s - 1, 0, npt - 1), 0))
    full_s = pl.BlockSpec((ms, d), lambda s: (0, 0))
    kern = functools.partial(_out_post_kernel, n_prompt_tiles=npt)
    return pl.pallas_call(
        kern,
        grid=(npt + 2,),
        in_specs=[mm_row, fin_row, full_s, full_s,
                  pl.BlockSpec(memory_space=pl.ANY),
                  pl.BlockSpec((1, d), lambda s: (0, 0))],
        out_specs=[fin_row, full_s],
        out_shape=[jax.ShapeDtypeStruct((mp, d), F32), jax.ShapeDtypeStruct((ms, d), F32)],
        scratch_shapes=[pltpu.VMEM((tm, d), F32),
                        pltpu.VMEM((kdim, d), BF16),
                        pltpu.VMEM((2, OUT_W_CHUNK, d), F32),
                        pltpu.SemaphoreType.DMA((2,))],
        compiler_params=_params(("arbitrary",)),
        name="out_proj_post_norm",
    )(m_p, x_p, m_s, x_s, w3, g)


def kernel(x_prompt, x_sample, state_hgrn, lb_logits, g_pre, w_in, ln_g, ln_b, w_s, b_s,
           g_onorm, w_pa, w_pb, w_o, g_post):
    batch, seq, d = x_prompt.shape
    n_seq, dec_seq, _ = x_sample.shape
    depth = w_in.shape[0]
    assert depth == 1 and dec_seq == 1
    assert seq % CHUNK_A == 0 and seq % CHUNK_B == 0
    e = w_pa.shape[1]
    n_groups = w_s.shape[1]
    assert n_groups * CHUNK_A == e

    xp = x_prompt.reshape(batch * seq, d)
    xs = x_sample.reshape(n_seq, d)
    xn_p = _rmsnorm(xp, g_pre, NORM_ROWS)
    xn_s = _rmsnorm(xs, g_pre, n_seq)

    col = lambda idx: idx * e
    uvg_p, uvg_s = _proj(xn_p, xn_s, w_in, [(col(0), 2 * e), (col(7), 2 * d)],
                         [_ep_gelu, _ep_sigmoid],
                         [(BF16, BF16, False)], name="proj_uv_gates")
    zz_p, zz_s = _proj(xn_p, xn_s, w_in, [(col(2), e), (col(6), e)], [_ep_silu, _ep_silu],
                       [(BF16, F32, False)], name="proj_za_zb")
    qi_p, qi_s = _proj(xn_p, xn_s, w_in, [(col(3), e), (col(5), e)],
                       [_ep_silu, _ep_identity],
                       [(F32, F32, True)], name="proj_q_i")
    lf_p, lf_s, k_p, k_s = _proj(xn_p, xn_s, w_in, [(col(4), e)], [_ep_forget],
                                 [(F32, F32, True), (BF16, F32, True)],
                                 aux=(lb_logits,), name="proj_f")

    bs_rows = jnp.repeat(b_s[0].T, CHUNK_A, axis=1)
    ya_p, vrows_p = _gate_prompt(uvg_p, zz_p, ln_g, ln_b, w_s[0], bs_rows, batch, seq, e)
    wd_row = jnp.repeat(w_s[0, :, 0, 0], CHUNK_A)[None, :]
    b0_row = jnp.repeat(b_s[0, :, 0], CHUNK_A)[None, :]
    ya_s, vrows_s = _gate_sample(uvg_s, zz_s, ln_g, ln_b, wd_row, b0_row, e)

    yb_p, state_p = _hgrn_prompt(qi_p, lf_p, k_p, zz_p, g_onorm, batch, seq)
    state_s, yb_s = _hgrn_sample(qi_s, lf_s, k_s, state_hgrn, zz_s, g_onorm)

    m_p, m_s = _merge(ya_p, yb_p, uvg_p, ya_s, yb_s, uvg_s, w_pa, w_pb, 2 * e)
    y_p, y_s = _out_post(m_p, xp, m_s, xs, w_o, g_post)

    return (y_p.reshape(batch, seq, d), y_s.reshape(n_seq, 1, d),
            state_p, state_s,
            vrows_p, vrows_s.reshape(1, n_seq, 1, e))
```

```python
import functools
import itertools

import jax
import jax.numpy as jnp
from jax import lax
from jax.experimental import pallas as pl
from jax.experimental.pallas import tpu as pltpu

F32 = jnp.float32
BF16 = jnp.bfloat16

EPS = 1e-6
LOG2_E = 1.4426950408889634
LANES = 128
SUBLANES = 8
CHUNK_A = 128
GATE_STEP_CHUNKS = 4
HEAD_DIM = 128
CHUNK_B = 128
HGRN_STEP_CHUNKS = 4
HEAD_UNROLL = 8
VMEM_LIMIT_BYTES = 56 * 1024 * 1024
ROW_TILE = 1024
COL_TILE = 512
PROJ_COL_TILE = 1024
SUB_ROWS = 128
NORM_ROWS = 512
OUT_ROWS = 128
OUT_W_CHUNK = 32
OUT_W_SLOTS = 4
SAMPLE_SEQ_BLOCK = 8
SAMPLE_HEAD_UNROLL = 4


def _params(sem):
    return pltpu.CompilerParams(dimension_semantics=sem,
                                vmem_limit_bytes=VMEM_LIMIT_BYTES)


def _rmsnorm_kernel(x_ref, g_ref, o_ref):
    x = x_ref[...]
    ms = jnp.mean(x * x, axis=-1, keepdims=True)
    o_ref[...] = (x * lax.rsqrt(ms + EPS) * g_ref[...]).astype(o_ref.dtype)


def _rmsnorm(x, g, rows):
    m, d = x.shape
    return pl.pallas_call(
        _rmsnorm_kernel,
        grid=(m // rows,),
        in_specs=[pl.BlockSpec((rows, d), lambda i: (i, 0)),
                  pl.BlockSpec((1, d), lambda i: (0, 0))],
        out_specs=pl.BlockSpec((rows, d), lambda i: (i, 0)),
        out_shape=jax.ShapeDtypeStruct((m, d), BF16),
        compiler_params=_params(("arbitrary",)),
        name="pre_rmsnorm",
    )(x, g)


def _sample_step(n_prompt_tiles):
    return n_prompt_tiles // 2


def _prompt_tile(j, i, n_prompt_tiles):
    fwd = i - (i > _sample_step(n_prompt_tiles)).astype(jnp.int32)
    return jnp.where(j % 2 == 0, fwd, n_prompt_tiles - 1 - fwd)


def _proj_kernel(*refs, epilogues, out_head_major, n_aux, n_prompt_tiles, sections,
                 n_chunks):
    n_out = len(out_head_major)
    xp_ref, xs_ref, w_hbm = refs[:3]
    aux_refs = refs[3:3 + n_aux]
    out_refs = refs[3 + n_aux:3 + n_aux + 2 * n_out]
    wbf_ref, stage_ref, sem = refs[-3:]
    j = pl.program_id(0)
    i = pl.program_id(1)
    n_col_tiles = sum(count for _, count in sections)
    cur = j % 2
    _, kc, tn = stage_ref.shape

    def weight_col_tile(jj):
        tile, start = None, 0
        for first, count in sections:
            t = first + (jj - start)
            tile = t if tile is None else jnp.where(jj >= start, t, tile)
            start += count
        return tile

    def chunk_copy(jj, c, slot):
        return pltpu.make_async_copy(
            w_hbm.at[0, pl.ds(c * kc, kc), pl.ds(weight_col_tile(jj) * tn, tn)],
            stage_ref.at[slot], sem.at[slot])

    @pl.when((j == 0) & (i == 0))
    def _():
        chunk_copy(0, 0, 0).start()
        for c in range(n_chunks):
            if c + 1 < n_chunks:
                chunk_copy(0, c + 1, (c + 1) % 2).start()
            chunk_copy(0, c, c % 2).wait()
            wbf_ref[0, c * kc:(c + 1) * kc, :] = stage_ref[c % 2].astype(BF16)

    has_next = j + 1 < n_col_tiles

    @pl.when(has_next & (i >= 1))
    def _():
        chunk_copy(j + 1, i - 1, (i - 1) % 2).wait()

    @pl.when(has_next & (i < n_chunks))
    def _():
        chunk_copy(j + 1, i, i % 2).start()

    def cast_previous_chunk():
        prev = (i + n_chunks - 1) % n_chunks
        wbf_ref[1 - cur, pl.ds(prev * kc, kc), :] = stage_ref[(i + 1) % 2].astype(BF16)

    def run(x_ref, outs, epilogue):
        cast_previous_chunk()
        rows = x_ref.shape[0]
        sub = min(rows, SUB_ROWS)
        for m in range(rows // sub):
            rs = slice(m * sub, (m + 1) * sub)
            acc = jnp.dot(x_ref[rs, :].astype(BF16), wbf_ref[cur],
                          preferred_element_type=F32)
            res = epilogue(acc, *[a[...] for a in aux_refs])
            for r, o_ref, hm in zip(res, outs, out_head_major):
                if hm:
                    for hh in range(o_ref.shape[0]):
                        o_ref[hh, rs, :] = r[:, hh * LANES:(hh + 1) * LANES].astype(o_ref.dtype)
                else:
                    o_ref[rs, :] = r.astype(o_ref.dtype)

    is_sample = i == _sample_step(n_prompt_tiles)
    start = 0
    for (_, count), epilogue in zip(sections, epilogues):
        in_section = (j >= start) & (j < start + count)
        start += count

        @pl.when(in_section & jnp.logical_not(is_sample))
        def _(epilogue=epilogue):
            run(xp_ref, out_refs[0::2], epilogue)

        @pl.when(in_section & is_sample)
        def _(epilogue=epilogue):
            run(xs_ref, out_refs[1::2], epilogue)


def _proj(xp, xs, w3, col_sections, epilogues, outs, aux=(), tn=None, name="proj"):
    mp, kdim = xp.shape
    ms = xs.shape[0]
    tm = ROW_TILE
    tn = PROJ_COL_TILE if tn is None else tn
    npt = mp // tm
    assert all(c0 % tn == 0 and w % tn == 0 for c0, w in col_sections)
    sections = tuple((c0 // tn, w // tn) for c0, w in col_sections)
    ncols = sum(w for _, w in col_sections)
    nj = ncols // tn
    hpt = tn // LANES
    n_chunks = npt
    assert kdim % n_chunks == 0 and n_chunks >= 2
    kc = kdim // n_chunks

    def row_i(j, i):
        return _prompt_tile(j, i, npt)

    in_specs = [
        pl.BlockSpec((tm, kdim), lambda j, i: (row_i(j, i), 0)),
        pl.BlockSpec((ms, kdim), lambda j, i: (0, 0)),
        pl.BlockSpec(memory_space=pl.ANY),
    ]
    for a in aux:
        in_specs.append(pl.BlockSpec((a.shape[0], tn), lambda j, i: (0, j)))
    out_specs, out_shapes = [], []
    for pdt, sdt, hm in outs:
        if hm:
            out_specs.append(pl.BlockSpec((hpt, tm, LANES), lambda j, i: (j, row_i(j, i), 0)))
            out_shapes.append(jax.ShapeDtypeStruct((ncols // LANES, mp, LANES), pdt))
            out_specs.append(pl.BlockSpec((hpt, ms, LANES), lambda j, i: (j, 0, 0)))
            out_shapes.append(jax.ShapeDtypeStruct((ncols // LANES, ms, LANES), sdt))
        else:
            out_specs.append(pl.BlockSpec((tm, tn), lambda j, i: (row_i(j, i), j)))
            out_shapes.append(jax.ShapeDtypeStruct((mp, ncols), pdt))
            out_specs.append(pl.BlockSpec((ms, tn), lambda j, i: (0, j)))
            out_shapes.append(jax.ShapeDtypeStruct((ms, ncols), sdt))
    kern = functools.partial(
        _proj_kernel, epilogues=tuple(epilogues),
        out_head_major=tuple(hm for _, _, hm in outs),
        n_aux=len(aux), n_prompt_tiles=npt, sections=sections, n_chunks=n_chunks)
    return pl.pallas_call(
        kern,
        grid=(nj, npt + 1),
        in_specs=in_specs,
        out_specs=out_specs,
        out_shape=out_shapes,
        scratch_shapes=[pltpu.VMEM((2, kdim, tn), BF16),
                        pltpu.VMEM((2, kc, tn), F32),
                        pltpu.SemaphoreType.DMA((2,))],
        compiler_params=_params(("arbitrary", "arbitrary")),
        name=name,
    )(xp, xs, w3, *aux)


def _ep_gelu(acc):
    return (jax.nn.gelu(acc, approximate=True),)


def _ep_silu(acc):
    return (acc * jax.nn.sigmoid(acc),)


def _ep_sigmoid(acc):
    return (jax.nn.sigmoid(acc),)


def _ep_identity(acc):
    return (acc,)


def _ep_forget(acc, lbl):
    m = jnp.max(lbl, axis=0, keepdims=True)
    e = jnp.exp(lbl - m)
    lb = e[0:1, :] / jnp.sum(e, axis=0, keepdims=True)
    sig = jax.nn.sigmoid(acc)
    log2_f = jnp.log(lb + (1.0 - lb) * sig) * LOG2_E
    k = (1.0 - lb) * (1.0 - sig)
    return log2_f, k


def _layernorm(gv, g, b):
    mu = jnp.mean(gv, axis=-1, keepdims=True)
    xc = gv - mu
    var = jnp.mean(xc * xc, axis=-1, keepdims=True)
    return xc * lax.rsqrt(var + EPS) * g + b


def _gate_kernel(u_ref, gv_ref, sza_ref, lng_ref, lnb_ref, ws_ref, bs_ref,
                 ya_ref, vr_ref, wm_ref, *, n_steps, n_groups):
    n = pl.program_id(1)

    @pl.when((pl.program_id(0) == 0) & (n == 0))
    def _():
        r = lax.broadcasted_iota(jnp.int32, (CHUNK_A, CHUNK_A), 0)
        c = lax.broadcasted_iota(jnp.int32, (CHUNK_A, CHUNK_A), 1)
        for g in range(n_groups):
            wm_ref[g] = jnp.where(c <= r, ws_ref[g], 0.0).astype(BF16)

    step_chunks = gv_ref.shape[0] // CHUNK_A
    for cc in range(step_chunks):
        rs = slice(cc * CHUNK_A, (cc + 1) * CHUNK_A)
        vn = _layernorm(gv_ref[rs, :].astype(F32), lng_ref[...], lnb_ref[...])

        if cc == step_chunks - 1:
            @pl.when(n == n_steps - 1)
            def _():
                vr_ref[...] = vn

        vnb = vn.astype(BF16)
        for g in range(n_groups):
            sl = slice(g * LANES, (g + 1) * LANES)
            mixed = jnp.dot(wm_ref[g], vnb[:, sl], preferred_element_type=F32) + bs_ref[:, sl]
            ya_ref[rs, sl] = (u_ref[rs, sl].astype(F32) * mixed
                              * sza_ref[rs, sl].astype(F32)).astype(ya_ref.dtype)


def _gate_prompt(uv, sza, ln_g, ln_b, w_s, bs_rows, batch, seq, e_a):
    rows = GATE_STEP_CHUNKS * CHUNK_A
    assert seq % rows == 0
    n_steps = seq // rows
    n_groups = w_s.shape[0]
    blk = lambda b, n: (b * n_steps + n, 0)
    kern = functools.partial(_gate_kernel, n_steps=n_steps, n_groups=n_groups)
    return pl.pallas_call(
        kern,
        grid=(batch, n_steps),
        in_specs=[
            pl.BlockSpec((rows, e_a), blk),
            pl.BlockSpec((rows, e_a), lambda b, n: (b * n_steps + n, 1)),
            pl.BlockSpec((rows, e_a), blk),
            pl.BlockSpec((1, e_a), lambda b, n: (0, 0)),
            pl.BlockSpec((1, e_a), lambda b, n: (0, 0)),
            pl.BlockSpec(w_s.shape, lambda b, n: (0, 0, 0)),
            pl.BlockSpec((CHUNK_A, e_a), lambda b, n: (0, 0)),
        ],
        out_specs=[
            pl.BlockSpec((rows, e_a), blk),
            pl.BlockSpec((None, None, CHUNK_A, e_a), lambda b, n: (0, b, 0, 0)),
        ],
        out_shape=[
            jax.ShapeDtypeStruct((batch * seq, e_a), BF16),
            jax.ShapeDtypeStruct((1, batch, CHUNK_A, e_a), F32),
        ],
        scratch_shapes=[pltpu.VMEM((n_groups, CHUNK_A, CHUNK_A), BF16)],
        compiler_params=_params(("arbitrary", "arbitrary")),
        name="spatial_gate_prompt",
    )(uv, uv, sza, ln_g, ln_b, w_s, bs_rows)


def _gate_sample_kernel(u_ref, gv_ref, sza_ref, lng_ref, lnb_ref, wd_ref, b0_ref,
                        ya_ref, vr_ref):
    vn = _layernorm(gv_ref[...].astype(F32), lng_ref[...], lnb_ref[...])
    vr_ref[...] = vn
    mixed = wd_ref[...] * vn + b0_ref[...]
    ya_ref[...] = (u_ref[...].astype(F32) * mixed * sza_ref[...].astype(F32)).astype(ya_ref.dtype)


def _gate_sample(uv, sza, ln_g, ln_b, wd_row, b0_row, e_a):
    m = sza.shape[0]
    full = lambda i: (0, 0)
    return pl.pallas_call(
        _gate_sample_kernel,
        grid=(1,),
        in_specs=[
            pl.BlockSpec((m, e_a), full),
            pl.BlockSpec((m, e_a), lambda i: (0, 1)),
            pl.BlockSpec((m, e_a), full),
            pl.BlockSpec((1, e_a), full),
            pl.BlockSpec((1, e_a), full),
            pl.BlockSpec((1, e_a), full),
            pl.BlockSpec((1, e_a), full),
        ],
        out_specs=[pl.BlockSpec((m, e_a), full), pl.BlockSpec((m, e_a), full)],
        out_shape=[jax.ShapeDtypeStruct((m, e_a), BF16),
                   jax.ShapeDtypeStruct((m, e_a), F32)],
        compiler_params=_params(("arbitrary",)),
        name="spatial_gate_sample",
    )(uv, uv, sza, ln_g, ln_b, wd_row, b0_row)


def _dot_nt(a, b):
    return lax.dot_general(a, b, (((1,), (1,)), ((), ())), preferred_element_type=F32)


def _dot_tn(a, b):
    return lax.dot_general(a, b, (((0,), (0,)), ((), ())), preferred_element_type=F32)


def _hgrn_kernel(q_ref, lf_ref, k_ref, i_ref, szb_ref, gon_ref, yb_ref, st_ref,
                 s_scr, o_scr, *att_scrs, n_steps, n_heads):
    n = pl.program_id(1)
    c, hd, sb = CHUNK_B, HEAD_DIM, SUBLANES
    nv = c // sb

    @pl.when((pl.program_id(0) == 0) & (n == 0))
    def _():
        for att_scr in att_scrs:
            att_scr[...] = jnp.zeros_like(att_scr)

    @pl.when(n == 0)
    def _():
        s_scr[...] = jnp.zeros_like(s_scr)

    r8 = lax.broadcasted_iota(jnp.int32, (1, sb, hd), 1)
    t_idx = lax.broadcasted_iota(jnp.int32, (c, c), 0)
    s_idx = lax.broadcasted_iota(jnp.int32, (c, c), 1)
    fine = [h for h in (1, 2, 4) if 2 * h <= sb]
    coarse = [1 << l for l in range(sb.bit_length() - 1, (c // 2).bit_length())]
    second8 = {h: (r8 & (2 * h - 1)) >= h for h in fine}
    fine_mask = {}
    for h in fine:
        same = (t_idx >> h.bit_length()) == (s_idx >> h.bit_length())
        if h == 1:
            fine_mask[h] = same & (s_idx <= t_idx)
        else:
            fine_mask[h] = same & ((t_idx & (2 * h - 1)) >= h) & ((s_idx & (2 * h - 1)) < h)

    def tiles(x):
        return x.reshape(nv, sb, hd)

    def fine_boundary(p3, h):
        acc = None
        for blk in range(sb // (2 * h)):
            r = blk * 2 * h + h - 1
            row = jnp.broadcast_to(p3[:, r:r + 1, :], (nv, sb, hd))
            acc = row if acc is None else jnp.where(r8 >= blk * 2 * h, row, acc)
        return acc

    def one_head(h, rows, att_ref):
        lf = lf_ref[h, rows, :]
        q = q_ref[h, rows, :]
        k = k_ref[h, rows, :].astype(F32)
        ib = i_ref[h, rows, :].astype(BF16)
        q3, k3, lf3 = tiles(q), tiles(k), tiles(lf)

        odd = second8[1]
        qf = jnp.where(odd, q3 * jnp.exp2(lf3), q3).reshape(c, hd).astype(BF16)
        kf = jnp.where(odd, k3 * jnp.exp2(-lf3), k3).reshape(c, hd).astype(BF16)
        att_f = jnp.where(fine_mask[1], _dot_nt(qf, kf), 0.0)
        p3 = lf3 + jnp.where(odd, tiles(pltpu.roll(lf, 1, 0)), 0.0)
        yield
        for hh in fine[1:]:
            bnd = fine_boundary(p3, hh)
            sec = second8[hh]
            g = jnp.where(sec, p3, bnd - p3)
            x = (jnp.where(sec, q3, k3) * jnp.exp2(g)).reshape(c, hd).astype(BF16)
            att_f = jnp.where(fine_mask[hh], _dot_nt(x, x), att_f)
            p3 = p3 + jnp.where(sec, bnd, 0.0)
            yield
        for v in range(nv):
            blk = slice(v * sb, (v + 1) * sb)
            att_ref[blk, blk] = att_f[blk, blk]

        p = p3.reshape(c, hd)
        for hh in coarse:
            xs, ps = [], []
            for blk in range(c // (2 * hh)):
                lo, mid, hi = blk * 2 * hh, blk * 2 * hh + hh, (blk + 1) * 2 * hh
                bnd = jnp.broadcast_to(p[mid - 1:mid, :], (hh, hd))
                p_lo, p_hi = p[lo:mid], p[mid:hi]
                xs.append(k[lo:mid] * jnp.exp2(bnd - p_lo))
                xs.append(q[mid:hi] * jnp.exp2(p_hi))
                ps.append(p_lo)
                ps.append(p_hi + bnd)
            x = jnp.concatenate(xs, axis=0).astype(BF16)
            a = _dot_nt(x, x)
            for blk in range(c // (2 * hh)):
                lo, mid, hi = blk * 2 * hh, blk * 2 * hh + hh, (blk + 1) * 2 * hh
                att_ref[mid:hi, lo:mid] = a[mid:hi, lo:mid]
            p = jnp.concatenate(ps, axis=0)
            yield

        b_last = p[c - 1:c, :]
        qt = (q * jnp.exp2(p)).astype(BF16)
        kt = (k * jnp.exp2(jnp.broadcast_to(b_last, (c, hd)) - p)).astype(BF16)
        dec = jnp.exp2(b_last)
        s_old = s_scr[h]
        lhs = jnp.concatenate([att_ref[...].astype(BF16), qt], axis=1)
        rhs = jnp.concatenate([ib, s_old.astype(BF16)], axis=0)
        o_scr[h] = jnp.dot(lhs, rhs, preferred_element_type=F32)
        dec_col = jnp.transpose(jnp.broadcast_to(dec, (hd, hd)))
        s_scr[h] = dec_col * s_old + _dot_tn(kt, ib)

    def chunk_body(cc, chunk_carry):
        rows = pl.ds(pl.multiple_of(cc * c, c), c)

        def head_group(grp, carry):
            heads = [one_head(grp * HEAD_UNROLL + u, rows, att_scrs[u])
                     for u in range(HEAD_UNROLL)]
            for _ in itertools.zip_longest(*heads):
                pass
            return carry

        lax.fori_loop(0, n_heads // HEAD_UNROLL, head_group, 0)

        for h in range(n_heads):
            sl = slice(h * hd, (h + 1) * hd)
            o = o_scr[h]
            ms = jnp.mean(o * o, axis=-1, keepdims=True)
            yb_ref[rows, sl] = (o * lax.rsqrt(ms + EPS) * gon_ref[:, sl]
                                * szb_ref[rows, sl].astype(F32)).astype(yb_ref.dtype)
        return chunk_carry

    lax.fori_loop(0, q_ref.shape[1] // c, chunk_body, 0)

    @pl.when(n == n_steps - 1)
    def _():
        st_ref[...] = s_scr[...]


def _hgrn_prompt(qi, lf, k, zz, g_onorm, batch, seq):
    n_heads = lf.shape[0]
    e_b = n_heads * HEAD_DIM
    rows = HGRN_STEP_CHUNKS * CHUNK_B
    assert seq % rows == 0
    n_steps = seq // rows
    hm = pl.BlockSpec((n_heads, rows, HEAD_DIM), lambda b, n: (0, b * n_steps + n, 0))
    hm_i = pl.BlockSpec((n_heads, rows, HEAD_DIM), lambda b, n: (1, b * n_steps + n, 0))
    rm = pl.BlockSpec((rows, e_b), lambda b, n: (b * n_steps + n, 0))
    rm_zb = pl.BlockSpec((rows, e_b), lambda b, n: (b * n_steps + n, 1))
    kern = functools.partial(_hgrn_kernel, n_steps=n_steps, n_heads=n_heads)
    q, iv, szb = qi, qi, zz
    return pl.pallas_call(
        kern,
        grid=(batch, n_steps),
        in_specs=[hm, hm, hm, hm_i, rm_zb, pl.BlockSpec((1, e_b), lambda b, n: (0, 0))],
        out_specs=[
            rm,
            pl.BlockSpec((None, None, n_heads, HEAD_DIM, HEAD_DIM),
                         lambda b, n: (0, b, 0, 0, 0)),
        ],
        out_shape=[
            jax.ShapeDtypeStruct((batch * seq, e_b), BF16),
            jax.ShapeDtypeStruct((1, batch, n_heads, HEAD_DIM, HEAD_DIM), F32),
        ],
        scratch_shapes=[
            pltpu.VMEM((n_heads, HEAD_DIM, HEAD_DIM), F32),
            pltpu.VMEM((n_heads, CHUNK_B, HEAD_DIM), F32),
        ] + [pltpu.VMEM((CHUNK_B, CHUNK_B), F32) for _ in range(HEAD_UNROLL)],
        compiler_params=_params(("arbitrary", "arbitrary")),
        name="hgrn_prompt",
    )(q, lf, k, iv, szb, g_onorm)


def _hgrn_sample_kernel(q_ref, lf_ref, k_ref, i_ref, st_ref, szb_ref, gon_ref,
                        sto_ref, yb_ref, ft_scr, kt_scr, o_scr, *, n_heads, nb):
    s = pl.program_id(0)
    hd = HEAD_DIM

    @pl.when(s == 0)
    def _():
        def tb(h, carry):
            ft_scr[h] = jnp.transpose(jnp.exp2(lf_ref[h]))
            kt_scr[h] = jnp.transpose(k_ref[h])
            return carry
        lax.fori_loop(0, n_heads, tb, 0)

    n_seq = lf_ref.shape[1]
    shift = jnp.where(s == 0, 0, n_seq - s * nb)

    def one_head(h):
        fr = pltpu.roll(ft_scr[h], shift, 1)
        kr = pltpu.roll(kt_scr[h], shift, 1)
        for j in range(nb):
            fb = jnp.broadcast_to(fr[:, j:j + 1], (hd, hd))
            kb = jnp.broadcast_to(kr[:, j:j + 1], (hd, hd))
            irow = i_ref[h, pl.ds(s * nb + j, 1), :]
            s_new = fb * st_ref[j, h] + kb * irow
            sto_ref[j, h] = s_new
            qrow = jnp.broadcast_to(q_ref[h, pl.ds(s * nb + j, 1), :], (SUBLANES, hd))
            o = jnp.dot(qrow.astype(BF16), s_new.astype(BF16), preferred_element_type=F32)
            o_scr[h, pl.ds(j, 1), :] = o[0:1, :]
            yield

    def head_group(grp, carry):
        heads = [one_head(grp * SAMPLE_HEAD_UNROLL + u) for u in range(SAMPLE_HEAD_UNROLL)]
        for _ in itertools.zip_longest(*heads):
            pass
        return carry

    lax.fori_loop(0, n_heads // SAMPLE_HEAD_UNROLL, head_group, 0)

    for h in range(n_heads):
        sl = slice(h * hd, (h + 1) * hd)
        o = o_scr[h]
        ms = jnp.mean(o * o, axis=-1, keepdims=True)
        yb_ref[:, sl] = (o * lax.rsqrt(ms + EPS) * gon_ref[:, sl] * szb_ref[:, sl])


def _hgrn_sample(qi, lf, k, state, zz, g_onorm):
    n_heads, n_seq, hd = lf.shape
    nb = SAMPLE_SEQ_BLOCK
    e_b = n_heads * hd
    full3 = pl.BlockSpec((n_heads, n_seq, hd), lambda s: (0, 0, 0))
    full3_i = pl.BlockSpec((n_heads, n_seq, hd), lambda s: (1, 0, 0))
    st_spec = pl.BlockSpec((None, nb, n_heads, hd, hd), lambda s: (0, s, 0, 0, 0))
    kern = functools.partial(_hgrn_sample_kernel, n_heads=n_heads, nb=nb)
    q, iv, szb = qi, qi, zz
    return pl.pallas_call(
        kern,
        grid=(n_seq // nb,),
        in_specs=[full3, full3, full3, full3_i, st_spec,
                  pl.BlockSpec((nb, e_b), lambda s: (s, 1)),
                  pl.BlockSpec((1, e_b), lambda s: (0, 0))],
        out_specs=[st_spec, pl.BlockSpec((nb, e_b), lambda s: (s, 0))],
        out_shape=[jax.ShapeDtypeStruct(state.shape, F32),
                   jax.ShapeDtypeStruct((n_seq, e_b), F32)],
        scratch_shapes=[
            pltpu.VMEM((n_heads, hd, n_seq), F32),
            pltpu.VMEM((n_heads, hd, n_seq), F32),
            pltpu.VMEM((n_heads, nb, hd), F32),
        ],
        compiler_params=_params(("arbitrary",)),
        name="hgrn_sample",
    )(q, lf, k, iv, state, szb, g_onorm)


def _merge_kernel(yap_ref, ybp_ref, gap_ref, gbp_ref, yas_ref, ybs_ref, gas_ref, gbs_ref,
                  wpa_ref, wpb_ref, mp_ref, ms_ref, wa_bf, wb_bf, *, n_prompt_tiles):
    i = pl.program_id(1)

    @pl.when(i == 0)
    def _():
        wa_bf[...] = wpa_ref[...].astype(BF16)
        wb_bf[...] = wpb_ref[...].astype(BF16)

    def run(ya, yb, ga, gb, out):
        rows = ya.shape[0]
        sub = min(rows, SUB_ROWS)
        for m in range(rows // sub):
            rs = slice(m * sub, (m + 1) * sub)
            a = jnp.dot(ya[rs, :].astype(BF16), wa_bf[...], preferred_element_type=F32)
            b = jnp.dot(yb[rs, :].astype(BF16), wb_bf[...], preferred_element_type=F32)
            out[rs, :] = (ga[rs, :].astype(F32) * a + gb[rs, :].astype(F32) * b).astype(out.dtype)

    @pl.when(i != _sample_step(n_prompt_tiles))
    def _():
        run(yap_ref, ybp_ref, gap_ref, gbp_ref, mp_ref)

    @pl.when(i == _sample_step(n_prompt_tiles))
    def _():
        run(yas_ref, ybs_ref, gas_ref, gbs_ref, ms_ref)


def _merge(ya_p, yb_p, g_p, ya_s, yb_s, g_s, w_pa, w_pb, gate_col0):
    mp, kdim = ya_p.shape
    ms = ya_s.shape[0]
    d = w_pa.shape[-1]
    tm, tn = ROW_TILE, COL_TILE
    npt = mp // tm
    nj = d // tn
    assert gate_col0 % tn == 0
    g0 = gate_col0 // tn

    def row_i(j, i):
        return _prompt_tile(j, i, npt)

    yp = pl.BlockSpec((tm, kdim), lambda j, i: (row_i(j, i), 0))
    ys = pl.BlockSpec((ms, kdim), lambda j, i: (0, 0))
    w = pl.BlockSpec((None, kdim, tn), lambda j, i: (0, 0, j))
    kern = functools.partial(_merge_kernel, n_prompt_tiles=npt)
    return pl.pallas_call(
        kern,
        grid=(nj, npt + 1),
        in_specs=[
            yp, yp,
            pl.BlockSpec((tm, tn), lambda j, i: (row_i(j, i), g0 + j)),
            pl.BlockSpec((tm, tn), lambda j, i: (row_i(j, i), g0 + nj + j)),
            ys, ys,
            pl.BlockSpec((ms, tn), lambda j, i: (0, g0 + j)),
            pl.BlockSpec((ms, tn), lambda j, i: (0, g0 + nj + j)),
            w, w,
        ],
        out_specs=[pl.BlockSpec((tm, tn), lambda j, i: (row_i(j, i), j)),
                   pl.BlockSpec((ms, tn), lambda j, i: (0, j))],
        out_shape=[jax.ShapeDtypeStruct((mp, d), BF16),
                   jax.ShapeDtypeStruct((ms, d), BF16)],
        scratch_shapes=[pltpu.VMEM((kdim, tn), BF16), pltpu.VMEM((kdim, tn), BF16)],
        compiler_params=_params(("arbitrary", "arbitrary")),
        name="gated_merge",
    )(ya_p, yb_p, g_p, g_p, ya_s, yb_s, g_s, g_s, w_pa, w_pb)


def _out_post_kernel(mp_ref, xp_ref, ms_ref, xs_ref, w_hbm, g_ref, op_ref, os_ref, z_scr,
                     w_ref, stage_ref, sem, *, n_prompt_tiles):
    s = pl.program_id(0)
    n_slots, kc, _ = stage_ref.shape
    n_w_chunks = w_ref.shape[0] // kc

    def chunk_copy(c, slot):
        return pltpu.make_async_copy(w_hbm.at[0, pl.ds(c * kc, kc), :],
                                     stage_ref.at[slot], sem.at[slot])

    def matmul(m_ref):
        z_scr[...] = jnp.dot(m_ref[...], w_ref[...], preferred_element_type=F32)

    def finish(x_ref, o_ref):
        z = z_scr[...]
        ms = jnp.mean(z * z, axis=-1, keepdims=True)
        o_ref[...] = x_ref[...] + z * lax.rsqrt(ms + EPS) * g_ref[...]

    @pl.when(s == 0)
    def _():
        for c in range(n_slots - 1):
            chunk_copy(c, c % n_slots).start()
        for c in range(n_w_chunks):
            ahead = c + n_slots - 1
            if ahead < n_w_chunks:
                chunk_copy(ahead, ahead % n_slots).start()
            chunk_copy(c, c % n_slots).wait()
            w_ref[c * kc:(c + 1) * kc, :] = stage_ref[c % n_slots].astype(BF16)
        matmul(mp_ref)

    @pl.when((s >= 1) & (s < n_prompt_tiles))
    def _():
        finish(xp_ref, op_ref)
        matmul(mp_ref)

    @pl.when(s == n_prompt_tiles)
    def _():
        finish(xp_ref, op_ref)
        matmul(ms_ref)

    @pl.when(s == n_prompt_tiles + 1)
    def _():
        finish(xs_ref, os_ref)


def _out_post(m_p, x_p, m_s, x_s, w3, g):
    mp, d = x_p.shape
    ms = x_s.shape[0]
    kdim = w3.shape[1]
    tm = OUT_ROWS
    assert ms == tm
    assert kdim % OUT_W_CHUNK == 0 and kdim // OUT_W_CHUNK >= OUT_W_SLOTS
    npt = mp // tm
    mm_row = pl.BlockSpec((tm, d), lambda s: (jnp.minimum(s, npt - 1), 0))
    fin_row = pl.BlockSpec((tm, d), lambda s: (jnp.clip(s - 1, 0, npt - 1), 0))
    full_s = pl.BlockSpec((ms, d), lambda s: (0, 0))
    kern = functools.partial(_out_post_kernel, n_prompt_tiles=npt)
    return pl.pallas_call(
        kern,
        grid=(npt + 2,),
        in_specs=[mm_row, fin_row, full_s, full_s,
                  pl.BlockSpec(memory_space=pl.ANY),
                  pl.BlockSpec((1, d), lambda s: (0, 0))],
        out_specs=[fin_row, full_s],
        out_shape=[jax.ShapeDtypeStruct((mp, d), F32), jax.ShapeDtypeStruct((ms, d), F32)],
        scratch_shapes=[pltpu.VMEM((tm, d), F32),
                        pltpu.VMEM((kdim, d), BF16),
                        pltpu.VMEM((OUT_W_SLOTS, OUT_W_CHUNK, d), F32),
                        pltpu.SemaphoreType.DMA((OUT_W_SLOTS,))],
        compiler_params=_params(("arbitrary",)),
        name="out_proj_post_norm",
    )(m_p, x_p, m_s, x_s, w3, g)


def kernel(x_prompt, x_sample, state_hgrn, lb_logits, g_pre, w_in, ln_g, ln_b, w_s, b_s,
           g_onorm, w_pa, w_pb, w_o, g_post):
    batch, seq, d = x_prompt.shape
    n_seq, dec_seq, _ = x_sample.shape
    depth = w_in.shape[0]
    assert depth == 1 and dec_seq == 1
    assert seq % CHUNK_A == 0 and seq % CHUNK_B == 0
    e = w_pa.shape[1]
    n_groups = w_s.shape[1]
    assert n_groups * CHUNK_A == e

    xp = x_prompt.reshape(batch * seq, d)
    xs = x_sample.reshape(n_seq, d)
    xn_p = _rmsnorm(xp, g_pre, NORM_ROWS)
    xn_s = _rmsnorm(xs, g_pre, n_seq)

    col = lambda idx: idx * e
    uvg_p, uvg_s = _proj(xn_p, xn_s, w_in, [(col(0), 2 * e), (col(7), 2 * d)],
                         [_ep_gelu, _ep_sigmoid],
                         [(BF16, BF16, False)], name="proj_uv_gates")
    zz_p, zz_s = _proj(xn_p, xn_s, w_in, [(col(2), e), (col(6), e)], [_ep_silu, _ep_silu],
                       [(BF16, F32, False)], name="proj_za_zb")
    qi_p, qi_s = _proj(xn_p, xn_s, w_in, [(col(3), e), (col(5), e)],
                       [_ep_silu, _ep_identity],
                       [(F32, F32, True)], name="proj_q_i")
    lf_p, lf_s, k_p, k_s = _proj(xn_p, xn_s, w_in, [(col(4), e)], [_ep_forget],
                                 [(F32, F32, True), (BF16, F32, True)],
                                 aux=(lb_logits,), name="proj_f")

    bs_rows = jnp.repeat(b_s[0].T, CHUNK_A, axis=1)
    ya_p, vrows_p = _gate_prompt(uvg_p, zz_p, ln_g, ln_b, w_s[0], bs_rows, batch, seq, e)
    wd_row = jnp.repeat(w_s[0, :, 0, 0], CHUNK_A)[None, :]
    b0_row = jnp.repeat(b_s[0, :, 0], CHUNK_A)[None, :]
    ya_s, vrows_s = _gate_sample(uvg_s, zz_s, ln_g, ln_b, wd_row, b0_row, e)

    yb_p, state_p = _hgrn_prompt(qi_p, lf_p, k_p, zz_p, g_onorm, batch, seq)
    state_s, yb_s = _hgrn_sample(qi_s, lf_s, k_s, state_hgrn, zz_s, g_onorm)

    m_p, m_s = _merge(ya_p, yb_p, uvg_p, ya_s, yb_s, uvg_s, w_pa, w_pb, 2 * e)
    y_p, y_s = _out_post(m_p, xp, m_s, xs, w_o, g_post)

    return (y_p.reshape(batch, seq, d), y_s.reshape(n_seq, 1, d),
            state_p, state_s,
            vrows_p, vrows_s.reshape(1, n_seq, 1, e))
```

```python
import functools
import itertools

import jax
import jax.numpy as jnp
from jax import lax
from jax.experimental import pallas as pl
from jax.experimental.pallas import tpu as pltpu

F32 = jnp.float32
BF16 = jnp.bfloat16

EPS = 1e-6
LOG2_E = 1.4426950408889634
LANES = 128
SUBLANES = 8
CHUNK_A = 128
GATE_STEP_CHUNKS = 8
HEAD_DIM = 128
CHUNK_B = 128
HGRN_STEP_CHUNKS = 4
HEAD_UNROLL = 8
VMEM_LIMIT_BYTES = 56 * 1024 * 1024
ROW_TILE = 1024
COL_TILE = 512
PROJ_COL_TILE = 1024
SUB_ROWS = 256
NORM_ROWS = 512
OUT_ROWS = 128
OUT_W_CHUNK = 32
OUT_W_SLOTS = 4
SAMPLE_SEQ_BLOCK = 8
SAMPLE_HEAD_UNROLL = 4


def _params(sem):
    return pltpu.CompilerParams(dimension_semantics=sem,
                                vmem_limit_bytes=VMEM_LIMIT_BYTES)


def _rmsnorm_kernel(x_ref, g_ref, o_ref):
    x = x_ref[...]
    ms = jnp.mean(x * x, axis=-1, keepdims=True)
    o_ref[...] = (x * lax.rsqrt(ms + EPS) * g_ref[...]).astype(o_ref.dtype)


def _rmsnorm(x, g, rows):
    m, d = x.shape
    return pl.pallas_call(
        _rmsnorm_kernel,
        grid=(m // rows,),
        in_specs=[pl.BlockSpec((rows, d), lambda i: (i, 0)),
                  pl.BlockSpec((1, d), lambda i: (0, 0))],
        out_specs=pl.BlockSpec((rows, d), lambda i: (i, 0)),
        out_shape=jax.ShapeDtypeStruct((m, d), BF16),
        compiler_params=_params(("arbitrary",)),
        name="pre_rmsnorm",
    )(x, g)


def _sample_step(n_prompt_tiles):
    return n_prompt_tiles // 2


def _prompt_tile(j, i, n_prompt_tiles):
    fwd = i - (i > _sample_step(n_prompt_tiles)).astype(jnp.int32)
    return jnp.where(j % 2 == 0, fwd, n_prompt_tiles - 1 - fwd)


def _proj_kernel(*refs, epilogues, out_head_major, n_aux, n_prompt_tiles, sections,
                 n_chunks):
    n_out = len(out_head_major)
    xp_ref, xs_ref, w_hbm = refs[:3]
    aux_refs = refs[3:3 + n_aux]
    out_refs = refs[3 + n_aux:3 + n_aux + 2 * n_out]
    wbf_ref, stage_ref, sem = refs[-3:]
    j = pl.program_id(0)
    i = pl.program_id(1)
    n_col_tiles = sum(count for _, count in sections)
    cur = j % 2
    _, kc, tn = stage_ref.shape

    def weight_col_tile(jj):
        tile, start = None, 0
        for first, count in sections:
            t = first + (jj - start)
            tile = t if tile is None else jnp.where(jj >= start, t, tile)
            start += count
        return tile

    def chunk_copy(jj, c, slot):
        return pltpu.make_async_copy(
            w_hbm.at[0, pl.ds(c * kc, kc), pl.ds(weight_col_tile(jj) * tn, tn)],
            stage_ref.at[slot], sem.at[slot])

    @pl.when((j == 0) & (i == 0))
    def _():
        chunk_copy(0, 0, 0).start()
        for c in range(n_chunks):
            if c + 1 < n_chunks:
                chunk_copy(0, c + 1, (c + 1) % 2).start()
            chunk_copy(0, c, c % 2).wait()
            wbf_ref[0, c * kc:(c + 1) * kc, :] = stage_ref[c % 2].astype(BF16)

    has_next = j + 1 < n_col_tiles

    @pl.when(has_next & (i >= 1))
    def _():
        chunk_copy(j + 1, i - 1, (i - 1) % 2).wait()

    @pl.when(has_next & (i < n_chunks))
    def _():
        chunk_copy(j + 1, i, i % 2).start()

    def cast_previous_chunk():
        prev = (i + n_chunks - 1) % n_chunks
        wbf_ref[1 - cur, pl.ds(prev * kc, kc), :] = stage_ref[(i + 1) % 2].astype(BF16)

    def run(x_ref, outs, epilogue):
        cast_previous_chunk()
        rows = x_ref.shape[0]
        sub = min(rows, SUB_ROWS)
        for m in range(rows // sub):
            rs = slice(m * sub, (m + 1) * sub)
            acc = jnp.dot(x_ref[rs, :].astype(BF16), wbf_ref[cur],
                          preferred_element_type=F32)
            res = epilogue(acc, *[a[...] for a in aux_refs])
            for r, o_ref, hm in zip(res, outs, out_head_major):
                if hm:
                    for hh in range(o_ref.shape[0]):
                        o_ref[hh, rs, :] = r[:, hh * LANES:(hh + 1) * LANES].astype(o_ref.dtype)
                else:
                    o_ref[rs, :] = r.astype(o_ref.dtype)

    is_sample = i == _sample_step(n_prompt_tiles)
    start = 0
    for (_, count), epilogue in zip(sections, epilogues):
        in_section = (j >= start) & (j < start + count)
        start += count

        @pl.when(in_section & jnp.logical_not(is_sample))
        def _(epilogue=epilogue):
            run(xp_ref, out_refs[0::2], epilogue)

        @pl.when(in_section & is_sample)
        def _(epilogue=epilogue):
            run(xs_ref, out_refs[1::2], epilogue)


def _proj(xp, xs, w3, col_sections, epilogues, outs, aux=(), tn=None, name="proj"):
    mp, kdim = xp.shape
    ms = xs.shape[0]
    tm = ROW_TILE
    tn = PROJ_COL_TILE if tn is None else tn
    npt = mp // tm
    assert all(c0 % tn == 0 and w % tn == 0 for c0, w in col_sections)
    sections = tuple((c0 // tn, w // tn) for c0, w in col_sections)
    ncols = sum(w for _, w in col_sections)
    nj = ncols // tn
    hpt = tn // LANES
    n_chunks = npt
    assert kdim % n_chunks == 0 and n_chunks >= 2
    kc = kdim // n_chunks

    def row_i(j, i):
        return _prompt_tile(j, i, npt)

    in_specs = [
        pl.BlockSpec((tm, kdim), lambda j, i: (row_i(j, i), 0)),
        pl.BlockSpec((ms, kdim), lambda j, i: (0, 0)),
        pl.BlockSpec(memory_space=pl.ANY),
    ]
    for a in aux:
        in_specs.append(pl.BlockSpec((a.shape[0], tn), lambda j, i: (0, j)))
    out_specs, out_shapes = [], []
    for pdt, sdt, hm in outs:
        if hm:
            out_specs.append(pl.BlockSpec((hpt, tm, LANES), lambda j, i: (j, row_i(j, i), 0)))
            out_shapes.append(jax.ShapeDtypeStruct((ncols // LANES, mp, LANES), pdt))
            out_specs.append(pl.BlockSpec((hpt, ms, LANES), lambda j, i: (j, 0, 0)))
            out_shapes.append(jax.ShapeDtypeStruct((ncols // LANES, ms, LANES), sdt))
        else:
            out_specs.append(pl.BlockSpec((tm, tn), lambda j, i: (row_i(j, i), j)))
            out_shapes.append(jax.ShapeDtypeStruct((mp, ncols), pdt))
            out_specs.append(pl.BlockSpec((ms, tn), lambda j, i: (0, j)))
            out_shapes.append(jax.ShapeDtypeStruct((ms, ncols), sdt))
    kern = functools.partial(
        _proj_kernel, epilogues=tuple(epilogues),
        out_head_major=tuple(hm for _, _, hm in outs),
        n_aux=len(aux), n_prompt_tiles=npt, sections=sections, n_chunks=n_chunks)
    return pl.pallas_call(
        kern,
        grid=(nj, npt + 1),
        in_specs=in_specs,
        out_specs=out_specs,
        out_shape=out_shapes,
        scratch_shapes=[pltpu.VMEM((2, kdim, tn), BF16),
                        pltpu.VMEM((2, kc, tn), F32),
                        pltpu.SemaphoreType.DMA((2,))],
        compiler_params=_params(("arbitrary", "arbitrary")),
        name=name,
    )(xp, xs, w3, *aux)


def _ep_gelu(acc):
    return (jax.nn.gelu(acc, approximate=True),)


def _ep_silu(acc):
    return (acc * jax.nn.sigmoid(acc),)


def _ep_sigmoid(acc):
    return (jax.nn.sigmoid(acc),)


def _ep_identity(acc):
    return (acc,)


def _ep_forget(acc, lbl):
    m = jnp.max(lbl, axis=0, keepdims=True)
    e = jnp.exp(lbl - m)
    lb = e[0:1, :] / jnp.sum(e, axis=0, keepdims=True)
    sig = jax.nn.sigmoid(acc)
    log2_f = jnp.log(lb + (1.0 - lb) * sig) * LOG2_E
    k = (1.0 - lb) * (1.0 - sig)
    return log2_f, k


def _layernorm(gv, g, b):
    mu = jnp.mean(gv, axis=-1, keepdims=True)
    xc = gv - mu
    var = jnp.mean(xc * xc, axis=-1, keepdims=True)
    return xc * lax.rsqrt(var + EPS) * g + b


def _gate_kernel(u_ref, gv_ref, sza_ref, lng_ref, lnb_ref, ws_ref, bs_ref,
                 ya_ref, vr_ref, wm_ref, *, n_steps, n_groups):
    n = pl.program_id(1)

    @pl.when((pl.program_id(0) == 0) & (n == 0))
    def _():
        r = lax.broadcasted_iota(jnp.int32, (CHUNK_A, CHUNK_A), 0)
        c = lax.broadcasted_iota(jnp.int32, (CHUNK_A, CHUNK_A), 1)
        for g in range(n_groups):
            wm_ref[g] = jnp.where(c <= r, ws_ref[g], 0.0).astype(BF16)

    step_chunks = gv_ref.shape[0] // CHUNK_A
    for cc in range(step_chunks):
        rs = slice(cc * CHUNK_A, (cc + 1) * CHUNK_A)
        vn = _layernorm(gv_ref[rs, :].astype(F32), lng_ref[...], lnb_ref[...])

        if cc == step_chunks - 1:
            @pl.when(n == n_steps - 1)
            def _():
                vr_ref[...] = vn

        vnb = vn.astype(BF16)
        for g in range(n_groups):
            sl = slice(g * LANES, (g + 1) * LANES)
            mixed = jnp.dot(wm_ref[g], vnb[:, sl], preferred_element_type=F32) + bs_ref[:, sl]
            ya_ref[rs, sl] = (u_ref[rs, sl].astype(F32) * mixed
                              * sza_ref[rs, sl].astype(F32)).astype(ya_ref.dtype)


def _gate_prompt(uv, sza, ln_g, ln_b, w_s, bs_rows, batch, seq, e_a):
    rows = GATE_STEP_CHUNKS * CHUNK_A
    assert seq % rows == 0
    n_steps = seq // rows
    n_groups = w_s.shape[0]
    blk = lambda b, n: (b * n_steps + n, 0)
    kern = functools.partial(_gate_kernel, n_steps=n_steps, n_groups=n_groups)
    return pl.pallas_call(
        kern,
        grid=(batch, n_steps),
        in_specs=[
            pl.BlockSpec((rows, e_a), blk),
            pl.BlockSpec((rows, e_a), lambda b, n: (b * n_steps + n, 1)),
            pl.BlockSpec((rows, e_a), blk),
            pl.BlockSpec((1, e_a), lambda b, n: (0, 0)),
            pl.BlockSpec((1, e_a), lambda b, n: (0, 0)),
            pl.BlockSpec(w_s.shape, lambda b, n: (0, 0, 0)),
            pl.BlockSpec((CHUNK_A, e_a), lambda b, n: (0, 0)),
        ],
        out_specs=[
            pl.BlockSpec((rows, e_a), blk),
            pl.BlockSpec((None, None, CHUNK_A, e_a), lambda b, n: (0, b, 0, 0)),
        ],
        out_shape=[
            jax.ShapeDtypeStruct((batch * seq, e_a), BF16),
            jax.ShapeDtypeStruct((1, batch, CHUNK_A, e_a), F32),
        ],
        scratch_shapes=[pltpu.VMEM((n_groups, CHUNK_A, CHUNK_A), BF16)],
        compiler_params=_params(("arbitrary", "arbitrary")),
        name="spatial_gate_prompt",
    )(uv, uv, sza, ln_g, ln_b, w_s, bs_rows)


def _gate_sample_kernel(u_ref, gv_ref, sza_ref, lng_ref, lnb_ref, wd_ref, b0_ref,
                        ya_ref, vr_ref):
    vn = _layernorm(gv_ref[...].astype(F32), lng_ref[...], lnb_ref[...])
    vr_ref[...] = vn
    mixed = wd_ref[...] * vn + b0_ref[...]
    ya_ref[...] = (u_ref[...].astype(F32) * mixed * sza_ref[...].astype(F32)).astype(ya_ref.dtype)


def _gate_sample(uv, sza, ln_g, ln_b, wd_row, b0_row, e_a):
    m = sza.shape[0]
    full = lambda i: (0, 0)
    return pl.pallas_call(
        _gate_sample_kernel,
        grid=(1,),
        in_specs=[
            pl.BlockSpec((m, e_a), full),
            pl.BlockSpec((m, e_a), lambda i: (0, 1)),
            pl.BlockSpec((m, e_a), full),
            pl.BlockSpec((1, e_a), full),
            pl.BlockSpec((1, e_a), full),
            pl.BlockSpec((1, e_a), full),
            pl.BlockSpec((1, e_a), full),
        ],
        out_specs=[pl.BlockSpec((m, e_a), full), pl.BlockSpec((m, e_a), full)],
        out_shape=[jax.ShapeDtypeStruct((m, e_a), BF16),
                   jax.ShapeDtypeStruct((m, e_a), F32)],
        compiler_params=_params(("arbitrary",)),
        name="spatial_gate_sample",
    )(uv, uv, sza, ln_g, ln_b, wd_row, b0_row)


def _dot_nt(a, b):
    return lax.dot_general(a, b, (((1,), (1,)), ((), ())), preferred_element_type=F32)


def _dot_tn(a, b):
    return lax.dot_general(a, b, (((0,), (0,)), ((), ())), preferred_element_type=F32)


def _hgrn_kernel(q_ref, lf_ref, k_ref, i_ref, szb_ref, gon_ref, yb_ref, st_ref,
                 s_scr, o_scr, *att_scrs, n_steps, n_heads):
    n = pl.program_id(1)
    c, hd, sb = CHUNK_B, HEAD_DIM, SUBLANES
    nv = c // sb

    @pl.when((pl.program_id(0) == 0) & (n == 0))
    def _():
        for att_scr in att_scrs:
            att_scr[...] = jnp.zeros_like(att_scr)

    @pl.when(n == 0)
    def _():
        s_scr[...] = jnp.zeros_like(s_scr)

    r8 = lax.broadcasted_iota(jnp.int32, (1, sb, hd), 1)
    t_idx = lax.broadcasted_iota(jnp.int32, (c, c), 0)
    s_idx = lax.broadcasted_iota(jnp.int32, (c, c), 1)
    fine = [h for h in (1, 2, 4) if 2 * h <= sb]
    coarse = [1 << l for l in range(sb.bit_length() - 1, (c // 2).bit_length())]
    second8 = {h: (r8 & (2 * h - 1)) >= h for h in fine}
    fine_mask = {}
    for h in fine:
        same = (t_idx >> h.bit_length()) == (s_idx >> h.bit_length())
        if h == 1:
            fine_mask[h] = same & (s_idx <= t_idx)
        else:
            fine_mask[h] = same & ((t_idx & (2 * h - 1)) >= h) & ((s_idx & (2 * h - 1)) < h)

    def tiles(x):
        return x.reshape(nv, sb, hd)

    def fine_boundary(p3, h):
        acc = None
        for blk in range(sb // (2 * h)):
            r = blk * 2 * h + h - 1
            row = jnp.broadcast_to(p3[:, r:r + 1, :], (nv, sb, hd))
            acc = row if acc is None else jnp.where(r8 >= blk * 2 * h, row, acc)
        return acc

    def one_head(h, rows, att_ref):
        lf = lf_ref[h, rows, :]
        q = q_ref[h, rows, :]
        k = k_ref[h, rows, :].astype(F32)
        ib = i_ref[h, rows, :].astype(BF16)
        q3, k3, lf3 = tiles(q), tiles(k), tiles(lf)

        odd = second8[1]
        qf = jnp.where(odd, q3 * jnp.exp2(lf3), q3).reshape(c, hd).astype(BF16)
        kf = jnp.where(odd, k3 * jnp.exp2(-lf3), k3).reshape(c, hd).astype(BF16)
        att_f = jnp.where(fine_mask[1], _dot_nt(qf, kf), 0.0)
        p3 = lf3 + jnp.where(odd, tiles(pltpu.roll(lf, 1, 0)), 0.0)
        yield
        for hh in fine[1:]:
            bnd = fine_boundary(p3, hh)
            sec = second8[hh]
            g = jnp.where(sec, p3, bnd - p3)
            x = (jnp.where(sec, q3, k3) * jnp.exp2(g)).reshape(c, hd).astype(BF16)
            att_f = jnp.where(fine_mask[hh], _dot_nt(x, x), att_f)
            p3 = p3 + jnp.where(sec, bnd, 0.0)
            yield
        for v in range(nv):
            blk = slice(v * sb, (v + 1) * sb)
            att_ref[blk, blk] = att_f[blk, blk]

        p = p3.reshape(c, hd)
        for hh in coarse:
            xs, ps = [], []
            for blk in range(c // (2 * hh)):
                lo, mid, hi = blk * 2 * hh, blk * 2 * hh + hh, (blk + 1) * 2 * hh
                bnd = jnp.broadcast_to(p[mid - 1:mid, :], (hh, hd))
                p_lo, p_hi = p[lo:mid], p[mid:hi]
                xs.append(k[lo:mid] * jnp.exp2(bnd - p_lo))
                xs.append(q[mid:hi] * jnp.exp2(p_hi))
                ps.append(p_lo)
                ps.append(p_hi + bnd)
            x = jnp.concatenate(xs, axis=0).astype(BF16)
            a = _dot_nt(x, x)
            for blk in range(c // (2 * hh)):
                lo, mid, hi = blk * 2 * hh, blk * 2 * hh + hh, (blk + 1) * 2 * hh
                att_ref[mid:hi, lo:mid] = a[mid:hi, lo:mid]
            p = jnp.concatenate(ps, axis=0)
            yield

        b_last = p[c - 1:c, :]
        qt = (q * jnp.exp2(p)).astype(BF16)
        kt = (k * jnp.exp2(jnp.broadcast_to(b_last, (c, hd)) - p)).astype(BF16)
        dec = jnp.exp2(b_last)
        s_old = s_scr[h]
        lhs = jnp.concatenate([att_ref[...].astype(BF16), qt], axis=1)
        rhs = jnp.concatenate([ib, s_old.astype(BF16)], axis=0)
        o_scr[h] = jnp.dot(lhs, rhs, preferred_element_type=F32)
        dec_col = jnp.transpose(jnp.broadcast_to(dec, (hd, hd)))
        s_scr[h] = dec_col * s_old + _dot_tn(kt, ib)

    def chunk_body(cc, chunk_carry):
        rows = pl.ds(pl.multiple_of(cc * c, c), c)

        def head_group(grp, carry):
            heads = [one_head(grp * HEAD_UNROLL + u, rows, att_scrs[u])
                     for u in range(HEAD_UNROLL)]
            for _ in itertools.zip_longest(*heads):
                pass
            return carry

        lax.fori_loop(0, n_heads // HEAD_UNROLL, head_group, 0)

        for h in range(n_heads):
            sl = slice(h * hd, (h + 1) * hd)
            o = o_scr[h]
            ms = jnp.mean(o * o, axis=-1, keepdims=True)
            yb_ref[rows, sl] = (o * lax.rsqrt(ms + EPS) * gon_ref[:, sl]
                                * szb_ref[rows, sl].astype(F32)).astype(yb_ref.dtype)
        return chunk_carry

    lax.fori_loop(0, q_ref.shape[1] // c, chunk_body, 0)

    @pl.when(n == n_steps - 1)
    def _():
        st_ref[...] = s_scr[...]


def _hgrn_prompt(qi, lf, k, zz, g_onorm, batch, seq):
    n_heads = lf.shape[0]
    e_b = n_heads * HEAD_DIM
    rows = HGRN_STEP_CHUNKS * CHUNK_B
    assert seq % rows == 0
    n_steps = seq // rows
    hm = pl.BlockSpec((n_heads, rows, HEAD_DIM), lambda b, n: (0, b * n_steps + n, 0))
    hm_i = pl.BlockSpec((n_heads, rows, HEAD_DIM), lambda b, n: (1, b * n_steps + n, 0))
    rm = pl.BlockSpec((rows, e_b), lambda b, n: (b * n_steps + n, 0))
    rm_zb = pl.BlockSpec((rows, e_b), lambda b, n: (b * n_steps + n, 1))
    kern = functools.partial(_hgrn_kernel, n_steps=n_steps, n_heads=n_heads)
    q, iv, szb = qi, qi, zz
    return pl.pallas_call(
        kern,
        grid=(batch, n_steps),
        in_specs=[hm, hm, hm, hm_i, rm_zb, pl.BlockSpec((1, e_b), lambda b, n: (0, 0))],
        out_specs=[
            rm,
            pl.BlockSpec((None, None, n_heads, HEAD_DIM, HEAD_DIM),
                         lambda b, n: (0, b, 0, 0, 0)),
        ],
        out_shape=[
            jax.ShapeDtypeStruct((batch * seq, e_b), BF16),
            jax.ShapeDtypeStruct((1, batch, n_heads, HEAD_DIM, HEAD_DIM), F32),
        ],
        scratch_shapes=[
            pltpu.VMEM((n_heads, HEAD_DIM, HEAD_DIM), F32),
            pltpu.VMEM((n_heads, CHUNK_B, HEAD_DIM), F32),
        ] + [pltpu.VMEM((CHUNK_B, CHUNK_B), F32) for _ in range(HEAD_UNROLL)],
        compiler_params=_params(("arbitrary", "arbitrary")),
        name="hgrn_prompt",
    )(q, lf, k, iv, szb, g_onorm)


def _hgrn_sample_kernel(q_ref, lf_ref, k_ref, i_ref, st_ref, szb_ref, gon_ref,
                        sto_ref, yb_ref, ft_scr, kt_scr, o_scr, *, n_heads, nb):
    s = pl.program_id(0)
    hd = HEAD_DIM

    @pl.when(s == 0)
    def _():
        def tb(h, carry):
            ft_scr[h] = jnp.transpose(jnp.exp2(lf_ref[h]))
            kt_scr[h] = jnp.transpose(k_ref[h])
            return carry
        lax.fori_loop(0, n_heads, tb, 0)

    n_seq = lf_ref.shape[1]
    shift = jnp.where(s == 0, 0, n_seq - s * nb)

    def one_head(h):
        fr = pltpu.roll(ft_scr[h], shift, 1)
        kr = pltpu.roll(kt_scr[h], shift, 1)
        for j in range(nb):
            fb = jnp.broadcast_to(fr[:, j:j + 1], (hd, hd))
            kb = jnp.broadcast_to(kr[:, j:j + 1], (hd, hd))
            irow = i_ref[h, pl.ds(s * nb + j, 1), :]
            s_new = fb * st_ref[j, h] + kb * irow
            sto_ref[j, h] = s_new
            qrow = jnp.broadcast_to(q_ref[h, pl.ds(s * nb + j, 1), :], (SUBLANES, hd))
            o = jnp.dot(qrow.astype(BF16), s_new.astype(BF16), preferred_element_type=F32)
            o_scr[h, pl.ds(j, 1), :] = o[0:1, :]
            yield

    def head_group(grp, carry):
        heads = [one_head(grp * SAMPLE_HEAD_UNROLL + u) for u in range(SAMPLE_HEAD_UNROLL)]
        for _ in itertools.zip_longest(*heads):
            pass
        return carry

    lax.fori_loop(0, n_heads // SAMPLE_HEAD_UNROLL, head_group, 0)

    for h in range(n_heads):
        sl = slice(h * hd, (h + 1) * hd)
        o = o_scr[h]
        ms = jnp.mean(o * o, axis=-1, keepdims=True)
        yb_ref[:, sl] = (o * lax.rsqrt(ms + EPS) * gon_ref[:, sl] * szb_ref[:, sl])


def _hgrn_sample(qi, lf, k, state, zz, g_onorm):
    n_heads, n_seq, hd = lf.shape
    nb = SAMPLE_SEQ_BLOCK
    e_b = n_heads * hd
    full3 = pl.BlockSpec((n_heads, n_seq, hd), lambda s: (0, 0, 0))
    full3_i = pl.BlockSpec((n_heads, n_seq, hd), lambda s: (1, 0, 0))
    st_spec = pl.BlockSpec((None, nb, n_heads, hd, hd), lambda s: (0, s, 0, 0, 0))
    kern = functools.partial(_hgrn_sample_kernel, n_heads=n_heads, nb=nb)
    q, iv, szb = qi, qi, zz
    return pl.pallas_call(
        kern,
        grid=(n_seq // nb,),
        in_specs=[full3, full3, full3, full3_i, st_spec,
                  pl.BlockSpec((nb, e_b), lambda s: (s, 1)),
                  pl.BlockSpec((1, e_b), lambda s: (0, 0))],
        out_specs=[st_spec, pl.BlockSpec((nb, e_b), lambda s: (s, 0))],
        out_shape=[jax.ShapeDtypeStruct(state.shape, F32),
                   jax.ShapeDtypeStruct((n_seq, e_b), F32)],
        scratch_shapes=[
            pltpu.VMEM((n_heads, hd, n_seq), F32),
            pltpu.VMEM((n_heads, hd, n_seq), F32),
            pltpu.VMEM((n_heads, nb, hd), F32),
        ],
        compiler_params=_params(("arbitrary",)),
        name="hgrn_sample",
    )(q, lf, k, iv, state, szb, g_onorm)


def _merge_kernel(yap_ref, ybp_ref, gap_ref, gbp_ref, yas_ref, ybs_ref, gas_ref, gbs_ref,
                  wpa_ref, wpb_ref, mp_ref, ms_ref, wa_bf, wb_bf, *, n_prompt_tiles):
    i = pl.program_id(1)

    @pl.when(i == 0)
    def _():
        wa_bf[...] = wpa_ref[...].astype(BF16)
        wb_bf[...] = wpb_ref[...].astype(BF16)

    def run(ya, yb, ga, gb, out):
        rows = ya.shape[0]
        sub = min(rows, SUB_ROWS)
        for m in range(rows // sub):
            rs = slice(m * sub, (m + 1) * sub)
            a = jnp.dot(ya[rs, :].astype(BF16), wa_bf[...], preferred_element_type=F32)
            b = jnp.dot(yb[rs, :].astype(BF16), wb_bf[...], preferred_element_type=F32)
            out[rs, :] = (ga[rs, :].astype(F32) * a + gb[rs, :].astype(F32) * b).astype(out.dtype)

    @pl.when(i != _sample_step(n_prompt_tiles))
    def _():
        run(yap_ref, ybp_ref, gap_ref, gbp_ref, mp_ref)

    @pl.when(i == _sample_step(n_prompt_tiles))
    def _():
        run(yas_ref, ybs_ref, gas_ref, gbs_ref, ms_ref)


def _merge(ya_p, yb_p, g_p, ya_s, yb_s, g_s, w_pa, w_pb, gate_col0):
    mp, kdim = ya_p.shape
    ms = ya_s.shape[0]
    d = w_pa.shape[-1]
    tm, tn = ROW_TILE, COL_TILE
    npt = mp // tm
    nj = d // tn
    assert gate_col0 % tn == 0
    g0 = gate_col0 // tn

    def row_i(j, i):
        return _prompt_tile(j, i, npt)

    yp = pl.BlockSpec((tm, kdim), lambda j, i: (row_i(j, i), 0))
    ys = pl.BlockSpec((ms, kdim), lambda j, i: (0, 0))
    w = pl.BlockSpec((None, kdim, tn), lambda j, i: (0, 0, j))
    kern = functools.partial(_merge_kernel, n_prompt_tiles=npt)
    return pl.pallas_call(
        kern,
        grid=(nj, npt + 1),
        in_specs=[
            yp, yp,
            pl.BlockSpec((tm, tn), lambda j, i: (row_i(j, i), g0 + j)),
            pl.BlockSpec((tm, tn), lambda j, i: (row_i(j, i), g0 + nj + j)),
            ys, ys,
            pl.BlockSpec((ms, tn), lambda j, i: (0, g0 + j)),
            pl.BlockSpec((ms, tn), lambda j, i: (0, g0 + nj + j)),
            w, w,
        ],
        out_specs=[pl.BlockSpec((tm, tn), lambda j, i: (row_i(j, i), j)),
                   pl.BlockSpec((ms, tn), lambda j, i: (0, j))],
        out_shape=[jax.ShapeDtypeStruct((mp, d), BF16),
                   jax.ShapeDtypeStruct((ms, d), BF16)],
        scratch_shapes=[pltpu.VMEM((kdim, tn), BF16), pltpu.VMEM((kdim, tn), BF16)],
        compiler_params=_params(("arbitrary", "arbitrary")),
        name="gated_merge",
    )(ya_p, yb_p, g_p, g_p, ya_s, yb_s, g_s, g_s, w_pa, w_pb)


def _out_post_kernel(mp_ref, xp_ref, ms_ref, xs_ref, w_hbm, g_ref, op_ref, os_ref, z_scr,
                     w_ref, stage_ref, sem, *, n_prompt_tiles):
    s = pl.program_id(0)
    n_slots, kc, _ = stage_ref.shape
    n_w_chunks = w_ref.shape[0] // kc

    def chunk_copy(c, slot):
        return pltpu.make_async_copy(w_hbm.at[0, pl.ds(c * kc, kc), :],
                                     stage_ref.at[slot], sem.at[slot])

    def matmul(m_ref):
        z_scr[...] = jnp.dot(m_ref[...], w_ref[...], preferred_element_type=F32)

    def finish(x_ref, o_ref):
        z = z_scr[...]
        ms = jnp.mean(z * z, axis=-1, keepdims=True)
        o_ref[...] = x_ref[...] + z * lax.rsqrt(ms + EPS) * g_ref[...]

    @pl.when(s == 0)
    def _():
        for c in range(n_slots - 1):
            chunk_copy(c, c % n_slots).start()
        for c in range(n_w_chunks):
            ahead = c + n_slots - 1
            if ahead < n_w_chunks:
                chunk_copy(ahead, ahead % n_slots).start()
            chunk_copy(c, c % n_slots).wait()
            w_ref[c * kc:(c + 1) * kc, :] = stage_ref[c % n_slots].astype(BF16)
        matmul(mp_ref)

    @pl.when((s >= 1) & (s < n_prompt_tiles))
    def _():
        finish(xp_ref, op_ref)
        matmul(mp_ref)

    @pl.when(s == n_prompt_tiles)
    def _():
        finish(xp_ref, op_ref)
        matmul(ms_ref)

    @pl.when(s == n_prompt_tiles + 1)
    def _():
        finish(xs_ref, os_ref)


def _out_post(m_p, x_p, m_s, x_s, w3, g):
    mp, d = x_p.shape
    ms = x_s.shape[0]
    kdim = w3.shape[1]
    tm = OUT_ROWS
    assert ms == tm
    assert kdim % OUT_W_CHUNK == 0 and kdim // OUT_W_CHUNK >= OUT_W_SLOTS
    npt = mp // tm
    mm_row = pl.BlockSpec((tm, d), lambda s: (jnp.minimum(s, npt - 1), 0))
    fin_row = pl.BlockSpec((tm, d), lambda s: (jnp.clip(s - 1, 0, npt - 1), 0))
    full_s = pl.BlockSpec((ms, d), lambda s: (0, 0))
    kern = functools.partial(_out_post_kernel, n_prompt_tiles=npt)
    return pl.pallas_call(
        kern,
        grid=(npt + 2,),
        in_specs=[mm_row, fin_row, full_s, full_s,
                  pl.BlockSpec(memory_space=pl.ANY),
                  pl.BlockSpec((1, d), lambda s: (0, 0))],
        out_specs=[fin_row, full_s],
        out_shape=[jax.ShapeDtypeStruct((mp, d), F32), jax.ShapeDtypeStruct((ms, d), F32)],
        scratch_shapes=[pltpu.VMEM((tm, d), F32),
                        pltpu.VMEM((kdim, d), BF16),
                        pltpu.VMEM((OUT_W_SLOTS, OUT_W_CHUNK, d), F32),
                        pltpu.SemaphoreType.DMA((OUT_W_SLOTS,))],
        compiler_params=_params(("arbitrary",)),
        name="out_proj_post_norm",
    )(m_p, x_p, m_s, x_s, w3, g)


def kernel(x_prompt, x_sample, state_hgrn, lb_logits, g_pre, w_in, ln_g, ln_b, w_s, b_s,
           g_onorm, w_pa, w_pb, w_o, g_post):
    batch, seq, d = x_prompt.shape
    n_seq, dec_seq, _ = x_sample.shape
    depth = w_in.shape[0]
    assert depth == 1 and dec_seq == 1
    assert seq % CHUNK_A == 0 and seq % CHUNK_B == 0
    e = w_pa.shape[1]
    n_groups = w_s.shape[1]
    assert n_groups * CHUNK_A == e

    xp = x_prompt.reshape(batch * seq, d)
    xs = x_sample.reshape(n_seq, d)
    xn_p = _rmsnorm(xp, g_pre, NORM_ROWS)
    xn_s = _rmsnorm(xs, g_pre, n_seq)

    col = lambda idx: idx * e
    uvg_p, uvg_s = _proj(xn_p, xn_s, w_in, [(col(0), 2 * e), (col(7), 2 * d)],
                         [_ep_gelu, _ep_sigmoid],
                         [(BF16, BF16, False)], name="proj_uv_gates")
    zz_p, zz_s = _proj(xn_p, xn_s, w_in, [(col(2), e), (col(6), e)], [_ep_silu, _ep_silu],
                       [(BF16, F32, False)], name="proj_za_zb")
    qi_p, qi_s = _proj(xn_p, xn_s, w_in, [(col(3), e), (col(5), e)],
                       [_ep_silu, _ep_identity],
                       [(F32, F32, True)], name="proj_q_i")
    lf_p, lf_s, k_p, k_s = _proj(xn_p, xn_s, w_in, [(col(4), e)], [_ep_forget],
                                 [(F32, F32, True), (BF16, F32, True)],
                                 aux=(lb_logits,), name="proj_f")

    bs_rows = jnp.repeat(b_s[0].T, CHUNK_A, axis=1)
    ya_p, vrows_p = _gate_prompt(uvg_p, zz_p, ln_g, ln_b, w_s[0], bs_rows, batch, seq, e)
    wd_row = jnp.repeat(w_s[0, :, 0, 0], CHUNK_A)[None, :]
    b0_row = jnp.repeat(b_s[0, :, 0], CHUNK_A)[None, :]
    ya_s, vrows_s = _gate_sample(uvg_s, zz_s, ln_g, ln_b, wd_row, b0_row, e)

    yb_p, state_p = _hgrn_prompt(qi_p, lf_p, k_p, zz_p, g_onorm, batch, seq)
    state_s, yb_s = _hgrn_sample(qi_s, lf_s, k_s, state_hgrn, zz_s, g_onorm)

    m_p, m_s = _merge(ya_p, yb_p, uvg_p, ya_s, yb_s, uvg_s, w_pa, w_pb, 2 * e)
    y_p, y_s = _out_post(m_p, xp, m_s, xs, w_o, g_post)

    return (y_p.reshape(batch, seq, d), y_s.reshape(n_seq, 1, d),
            state_p, state_s,
            vrows_p, vrows_s.reshape(1, n_seq, 1, e))
```

```python
import functools
import itertools

import jax
import jax.numpy as jnp
from jax import lax
from jax.experimental import pallas as pl
from jax.experimental.pallas import tpu as pltpu

F32 = jnp.float32
BF16 = jnp.bfloat16

EPS = 1e-6
LOG2_E = 1.4426950408889634
LANES = 128
SUBLANES = 8
CHUNK_A = 128
GATE_STEP_CHUNKS = 8
HEAD_DIM = 128
CHUNK_B = 128
HGRN_STEP_CHUNKS = 4
HEAD_UNROLL = 8
VMEM_LIMIT_BYTES = 56 * 1024 * 1024
ROW_TILE = 1024
PROJ_COL_TILE = 1024
SUB_ROWS = 256
NORM_ROWS = 512
OUT_ROWS = 128
OUT_W_CHUNK = 32
OUT_W_SLOTS = 4
SAMPLE_SEQ_BLOCK = 8
SAMPLE_HEAD_UNROLL = 8


def _params(sem):
    return pltpu.CompilerParams(dimension_semantics=sem,
                                vmem_limit_bytes=VMEM_LIMIT_BYTES)


def _rmsnorm_kernel(x_ref, g_ref, o_ref):
    x = x_ref[...]
    ms = jnp.mean(x * x, axis=-1, keepdims=True)
    o_ref[...] = (x * lax.rsqrt(ms + EPS) * g_ref[...]).astype(o_ref.dtype)


def _rmsnorm(x, g, rows):
    m, d = x.shape
    return pl.pallas_call(
        _rmsnorm_kernel,
        grid=(m // rows,),
        in_specs=[pl.BlockSpec((rows, d), lambda i: (i, 0)),
                  pl.BlockSpec((1, d), lambda i: (0, 0))],
        out_specs=pl.BlockSpec((rows, d), lambda i: (i, 0)),
        out_shape=jax.ShapeDtypeStruct((m, d), BF16),
        compiler_params=_params(("arbitrary",)),
        name="pre_rmsnorm",
    )(x, g)


def _sample_step(n_prompt_tiles):
    return n_prompt_tiles // 2


def _prompt_tile(j, i, n_prompt_tiles):
    fwd = i - (i > _sample_step(n_prompt_tiles)).astype(jnp.int32)
    return jnp.where(j % 2 == 0, fwd, n_prompt_tiles - 1 - fwd)


def _proj_kernel(*refs, epilogues, out_head_major, n_aux, n_prompt_tiles, sections,
                 n_chunks):
    n_out = len(out_head_major)
    xp_ref, xs_ref, w_hbm = refs[:3]
    aux_refs = refs[3:3 + n_aux]
    out_refs = refs[3 + n_aux:3 + n_aux + 2 * n_out]
    wbf_ref, stage_ref, sem = refs[-3:]
    j = pl.program_id(0)
    i = pl.program_id(1)
    n_col_tiles = sum(count for _, count in sections)
    cur = j % 2
    _, kc, tn = stage_ref.shape

    def weight_col_tile(jj):
        tile, start = None, 0
        for first, count in sections:
            t = first + (jj - start)
            tile = t if tile is None else jnp.where(jj >= start, t, tile)
            start += count
        return tile

    def chunk_copy(jj, c, slot):
        return pltpu.make_async_copy(
            w_hbm.at[0, pl.ds(c * kc, kc), pl.ds(weight_col_tile(jj) * tn, tn)],
            stage_ref.at[slot], sem.at[slot])

    @pl.when((j == 0) & (i == 0))
    def _():
        chunk_copy(0, 0, 0).start()
        for c in range(n_chunks):
            if c + 1 < n_chunks:
                chunk_copy(0, c + 1, (c + 1) % 2).start()
            chunk_copy(0, c, c % 2).wait()
            wbf_ref[0, c * kc:(c + 1) * kc, :] = stage_ref[c % 2].astype(BF16)

    has_next = j + 1 < n_col_tiles

    @pl.when(has_next & (i >= 1))
    def _():
        chunk_copy(j + 1, i - 1, (i - 1) % 2).wait()

    @pl.when(has_next & (i < n_chunks))
    def _():
        chunk_copy(j + 1, i, i % 2).start()

    def cast_previous_chunk():
        prev = (i + n_chunks - 1) % n_chunks
        wbf_ref[1 - cur, pl.ds(prev * kc, kc), :] = stage_ref[(i + 1) % 2].astype(BF16)

    def run(x_ref, outs, epilogue):
        cast_previous_chunk()
        rows = x_ref.shape[0]
        sub = min(rows, SUB_ROWS)
        for m in range(rows // sub):
            rs = slice(m * sub, (m + 1) * sub)
            acc = jnp.dot(x_ref[rs, :].astype(BF16), wbf_ref[cur],
                          preferred_element_type=F32)
            res = epilogue(acc, *[a[...] for a in aux_refs])
            for r, o_ref, hm in zip(res, outs, out_head_major):
                if hm:
                    for hh in range(o_ref.shape[0]):
                        o_ref[hh, rs, :] = r[:, hh * LANES:(hh + 1) * LANES].astype(o_ref.dtype)
                else:
                    o_ref[rs, :] = r.astype(o_ref.dtype)

    is_sample = i == _sample_step(n_prompt_tiles)
    start = 0
    for (_, count), epilogue in zip(sections, epilogues):
        in_section = (j >= start) & (j < start + count)
        start += count

        @pl.when(in_section & jnp.logical_not(is_sample))
        def _(epilogue=epilogue):
            run(xp_ref, out_refs[0::2], epilogue)

        @pl.when(in_section & is_sample)
        def _(epilogue=epilogue):
            run(xs_ref, out_refs[1::2], epilogue)


def _proj(xp, xs, w3, col_sections, epilogues, outs, aux=(), tn=None, name="proj"):
    mp, kdim = xp.shape
    ms = xs.shape[0]
    tm = ROW_TILE
    tn = PROJ_COL_TILE if tn is None else tn
    npt = mp // tm
    assert all(c0 % tn == 0 and w % tn == 0 for c0, w in col_sections)
    sections = tuple((c0 // tn, w // tn) for c0, w in col_sections)
    ncols = sum(w for _, w in col_sections)
    nj = ncols // tn
    hpt = tn // LANES
    n_chunks = npt
    assert kdim % n_chunks == 0 and n_chunks >= 2
    kc = kdim // n_chunks

    def row_i(j, i):
        return _prompt_tile(j, i, npt)

    in_specs = [
        pl.BlockSpec((tm, kdim), lambda j, i: (row_i(j, i), 0)),
        pl.BlockSpec((ms, kdim), lambda j, i: (0, 0)),
        pl.BlockSpec(memory_space=pl.ANY),
    ]
    for a in aux:
        in_specs.append(pl.BlockSpec((a.shape[0], tn), lambda j, i: (0, j)))
    out_specs, out_shapes = [], []
    for pdt, sdt, hm in outs:
        if hm:
            out_specs.append(pl.BlockSpec((hpt, tm, LANES), lambda j, i: (j, row_i(j, i), 0)))
            out_shapes.append(jax.ShapeDtypeStruct((ncols // LANES, mp, LANES), pdt))
            out_specs.append(pl.BlockSpec((hpt, ms, LANES), lambda j, i: (j, 0, 0)))
            out_shapes.append(jax.ShapeDtypeStruct((ncols // LANES, ms, LANES), sdt))
        else:
            out_specs.append(pl.BlockSpec((tm, tn), lambda j, i: (row_i(j, i), j)))
            out_shapes.append(jax.ShapeDtypeStruct((mp, ncols), pdt))
            out_specs.append(pl.BlockSpec((ms, tn), lambda j, i: (0, j)))
            out_shapes.append(jax.ShapeDtypeStruct((ms, ncols), sdt))
    kern = functools.partial(
        _proj_kernel, epilogues=tuple(epilogues),
        out_head_major=tuple(hm for _, _, hm in outs),
        n_aux=len(aux), n_prompt_tiles=npt, sections=sections, n_chunks=n_chunks)
    return pl.pallas_call(
        kern,
        grid=(nj, npt + 1),
        in_specs=in_specs,
        out_specs=out_specs,
        out_shape=out_shapes,
        scratch_shapes=[pltpu.VMEM((2, kdim, tn), BF16),
                        pltpu.VMEM((2, kc, tn), F32),
                        pltpu.SemaphoreType.DMA((2,))],
        compiler_params=_params(("arbitrary", "arbitrary")),
        name=name,
    )(xp, xs, w3, *aux)


def _ep_gelu(acc):
    return (jax.nn.gelu(acc, approximate=True),)


def _ep_silu(acc):
    return (acc * jax.nn.sigmoid(acc),)


def _ep_sigmoid(acc):
    return (jax.nn.sigmoid(acc),)


def _ep_identity(acc):
    return (acc,)


def _ep_forget(acc, lbl):
    m = jnp.max(lbl, axis=0, keepdims=True)
    e = jnp.exp(lbl - m)
    lb = e[0:1, :] / jnp.sum(e, axis=0, keepdims=True)
    sig = jax.nn.sigmoid(acc)
    log2_f = jnp.log(lb + (1.0 - lb) * sig) * LOG2_E
    k = (1.0 - lb) * (1.0 - sig)
    return log2_f, k


def _layernorm(gv, g, b):
    mu = jnp.mean(gv, axis=-1, keepdims=True)
    xc = gv - mu
    var = jnp.mean(xc * xc, axis=-1, keepdims=True)
    return xc * lax.rsqrt(var + EPS) * g + b


def _gate_kernel(u_ref, gv_ref, sza_ref, lng_ref, lnb_ref, ws_ref, bs_ref,
                 ya_ref, vr_ref, wm_ref, *, n_steps, n_groups):
    n = pl.program_id(1)

    @pl.when((pl.program_id(0) == 0) & (n == 0))
    def _():
        r = lax.broadcasted_iota(jnp.int32, (CHUNK_A, CHUNK_A), 0)
        c = lax.broadcasted_iota(jnp.int32, (CHUNK_A, CHUNK_A), 1)
        for g in range(n_groups):
            wm_ref[g] = jnp.where(c <= r, ws_ref[g], 0.0).astype(BF16)

    step_chunks = gv_ref.shape[0] // CHUNK_A
    for cc in range(step_chunks):
        rs = slice(cc * CHUNK_A, (cc + 1) * CHUNK_A)
        vn = _layernorm(gv_ref[rs, :].astype(F32), lng_ref[...], lnb_ref[...])

        if cc == step_chunks - 1:
            @pl.when(n == n_steps - 1)
            def _():
                vr_ref[...] = vn

        vnb = vn.astype(BF16)
        for g in range(n_groups):
            sl = slice(g * LANES, (g + 1) * LANES)
            mixed = jnp.dot(wm_ref[g], vnb[:, sl], preferred_element_type=F32) + bs_ref[:, sl]
            ya_ref[rs, sl] = (u_ref[rs, sl].astype(F32) * mixed
                              * sza_ref[rs, sl].astype(F32)).astype(ya_ref.dtype)


def _gate_prompt(uv, sza, ln_g, ln_b, w_s, bs_rows, batch, seq, e_a):
    rows = GATE_STEP_CHUNKS * CHUNK_A
    assert seq % rows == 0
    n_steps = seq // rows
    n_groups = w_s.shape[0]
    blk = lambda b, n: (b * n_steps + n, 0)
    kern = functools.partial(_gate_kernel, n_steps=n_steps, n_groups=n_groups)
    return pl.pallas_call(
        kern,
        grid=(batch, n_steps),
        in_specs=[
            pl.BlockSpec((rows, e_a), blk),
            pl.BlockSpec((rows, e_a), lambda b, n: (b * n_steps + n, 1)),
            pl.BlockSpec((rows, e_a), blk),
            pl.BlockSpec((1, e_a), lambda b, n: (0, 0)),
            pl.BlockSpec((1, e_a), lambda b, n: (0, 0)),
            pl.BlockSpec(w_s.shape, lambda b, n: (0, 0, 0)),
            pl.BlockSpec((CHUNK_A, e_a), lambda b, n: (0, 0)),
        ],
        out_specs=[
            pl.BlockSpec((rows, e_a), blk),
            pl.BlockSpec((None, None, CHUNK_A, e_a), lambda b, n: (0, b, 0, 0)),
        ],
        out_shape=[
            jax.ShapeDtypeStruct((batch * seq, e_a), BF16),
            jax.ShapeDtypeStruct((1, batch, CHUNK_A, e_a), F32),
        ],
        scratch_shapes=[pltpu.VMEM((n_groups, CHUNK_A, CHUNK_A), BF16)],
        compiler_params=_params(("arbitrary", "arbitrary")),
        name="spatial_gate_prompt",
    )(uv, uv, sza, ln_g, ln_b, w_s, bs_rows)


def _gate_sample_kernel(u_ref, gv_ref, sza_ref, lng_ref, lnb_ref, wd_ref, b0_ref,
                        ya_ref, vr_ref):
    vn = _layernorm(gv_ref[...].astype(F32), lng_ref[...], lnb_ref[...])
    vr_ref[...] = vn
    mixed = wd_ref[...] * vn + b0_ref[...]
    ya_ref[...] = (u_ref[...].astype(F32) * mixed * sza_ref[...].astype(F32)).astype(ya_ref.dtype)


def _gate_sample(uv, sza, ln_g, ln_b, wd_row, b0_row, e_a):
    m = sza.shape[0]
    full = lambda i: (0, 0)
    return pl.pallas_call(
        _gate_sample_kernel,
        grid=(1,),
        in_specs=[
            pl.BlockSpec((m, e_a), full),
            pl.BlockSpec((m, e_a), lambda i: (0, 1)),
            pl.BlockSpec((m, e_a), full),
            pl.BlockSpec((1, e_a), full),
            pl.BlockSpec((1, e_a), full),
            pl.BlockSpec((1, e_a), full),
            pl.BlockSpec((1, e_a), full),
        ],
        out_specs=[pl.BlockSpec((m, e_a), full), pl.BlockSpec((m, e_a), full)],
        out_shape=[jax.ShapeDtypeStruct((m, e_a), BF16),
                   jax.ShapeDtypeStruct((m, e_a), F32)],
        compiler_params=_params(("arbitrary",)),
        name="spatial_gate_sample",
    )(uv, uv, sza, ln_g, ln_b, wd_row, b0_row)


def _dot_nt(a, b):
    return lax.dot_general(a, b, (((1,), (1,)), ((), ())), preferred_element_type=F32)


def _dot_tn(a, b):
    return lax.dot_general(a, b, (((0,), (0,)), ((), ())), preferred_element_type=F32)


def _hgrn_kernel(q_ref, lf_ref, k_ref, i_ref, szb_ref, gon_ref, yb_ref, st_ref,
                 s_scr, o_scr, *att_scrs, n_steps, n_heads):
    n = pl.program_id(1)
    c, hd, sb = CHUNK_B, HEAD_DIM, SUBLANES
    nv = c // sb

    @pl.when((pl.program_id(0) == 0) & (n == 0))
    def _():
        for att_scr in att_scrs:
            att_scr[...] = jnp.zeros_like(att_scr)

    @pl.when(n == 0)
    def _():
        s_scr[...] = jnp.zeros_like(s_scr)

    r8 = lax.broadcasted_iota(jnp.int32, (1, sb, hd), 1)
    t_idx = lax.broadcasted_iota(jnp.int32, (c, c), 0)
    s_idx = lax.broadcasted_iota(jnp.int32, (c, c), 1)
    fine = [h for h in (1, 2, 4) if 2 * h <= sb]
    coarse = [1 << l for l in range(sb.bit_length() - 1, (c // 2).bit_length())]
    second8 = {h: (r8 & (2 * h - 1)) >= h for h in fine}
    fine_mask = {}
    for h in fine:
        same = (t_idx >> h.bit_length()) == (s_idx >> h.bit_length())
        if h == 1:
            fine_mask[h] = same & (s_idx <= t_idx)
        else:
            fine_mask[h] = same & ((t_idx & (2 * h - 1)) >= h) & ((s_idx & (2 * h - 1)) < h)

    def tiles(x):
        return x.reshape(nv, sb, hd)

    def fine_boundary(p3, h):
        acc = None
        for blk in range(sb // (2 * h)):
            r = blk * 2 * h + h - 1
            row = jnp.broadcast_to(p3[:, r:r + 1, :], (nv, sb, hd))
            acc = row if acc is None else jnp.where(r8 >= blk * 2 * h, row, acc)
        return acc

    def one_head(h, rows, att_ref):
        lf = lf_ref[h, rows, :]
        q = q_ref[h, rows, :]
        k = k_ref[h, rows, :].astype(F32)
        ib = i_ref[h, rows, :].astype(BF16)
        q3, k3, lf3 = tiles(q), tiles(k), tiles(lf)

        odd = second8[1]
        qf = jnp.where(odd, q3 * jnp.exp2(lf3), q3).reshape(c, hd).astype(BF16)
        kf = jnp.where(odd, k3 * jnp.exp2(-lf3), k3).reshape(c, hd).astype(BF16)
        att_f = jnp.where(fine_mask[1], _dot_nt(qf, kf), 0.0)
        p3 = lf3 + jnp.where(odd, tiles(pltpu.roll(lf, 1, 0)), 0.0)
        yield
        for hh in fine[1:]:
            bnd = fine_boundary(p3, hh)
            sec = second8[hh]
            g = jnp.where(sec, p3, bnd - p3)
            x = (jnp.where(sec, q3, k3) * jnp.exp2(g)).reshape(c, hd).astype(BF16)
            att_f = jnp.where(fine_mask[hh], _dot_nt(x, x), att_f)
            p3 = p3 + jnp.where(sec, bnd, 0.0)
            yield
        for v in range(nv):
            blk = slice(v * sb, (v + 1) * sb)
            att_ref[blk, blk] = att_f[blk, blk]

        p = p3.reshape(c, hd)
        for hh in coarse:
            xs, ps = [], []
            for blk in range(c // (2 * hh)):
                lo, mid, hi = blk * 2 * hh, blk * 2 * hh + hh, (blk + 1) * 2 * hh
                bnd = jnp.broadcast_to(p[mid - 1:mid, :], (hh, hd))
                p_lo, p_hi = p[lo:mid], p[mid:hi]
                xs.append(k[lo:mid] * jnp.exp2(bnd - p_lo))
                xs.append(q[mid:hi] * jnp.exp2(p_hi))
                ps.append(p_lo)
                ps.append(p_hi + bnd)
            x = jnp.concatenate(xs, axis=0).astype(BF16)
            a = _dot_nt(x, x)
            for blk in range(c // (2 * hh)):
                lo, mid, hi = blk * 2 * hh, blk * 2 * hh + hh, (blk + 1) * 2 * hh
                att_ref[mid:hi, lo:mid] = a[mid:hi, lo:mid]
            p = jnp.concatenate(ps, axis=0)
            yield

        b_last = p[c - 1:c, :]
        qt = (q * jnp.exp2(p)).astype(BF16)
        kt = (k * jnp.exp2(jnp.broadcast_to(b_last, (c, hd)) - p)).astype(BF16)
        dec = jnp.exp2(b_last)
        s_old = s_scr[h]
        lhs = jnp.concatenate([att_ref[...].astype(BF16), qt], axis=1)
        rhs = jnp.concatenate([ib, s_old.astype(BF16)], axis=0)
        o_scr[h] = jnp.dot(lhs, rhs, preferred_element_type=F32)
        dec_col = jnp.transpose(jnp.broadcast_to(dec, (hd, hd)))
        s_scr[h] = dec_col * s_old + _dot_tn(kt, ib)

    def chunk_body(cc, chunk_carry):
        rows = pl.ds(pl.multiple_of(cc * c, c), c)

        def head_group(grp, carry):
            heads = [one_head(grp * HEAD_UNROLL + u, rows, att_scrs[u])
                     for u in range(HEAD_UNROLL)]
            for _ in itertools.zip_longest(*heads):
                pass
            return carry

        lax.fori_loop(0, n_heads // HEAD_UNROLL, head_group, 0)

        for h in range(n_heads):
            sl = slice(h * hd, (h + 1) * hd)
            o = o_scr[h]
            ms = jnp.mean(o * o, axis=-1, keepdims=True)
            yb_ref[rows, sl] = (o * lax.rsqrt(ms + EPS) * gon_ref[:, sl]
                                * szb_ref[rows, sl].astype(F32)).astype(yb_ref.dtype)
        return chunk_carry

    lax.fori_loop(0, q_ref.shape[1] // c, chunk_body, 0)

    @pl.when(n == n_steps - 1)
    def _():
        st_ref[...] = s_scr[...]


def _hgrn_prompt(qi, lf, k, zz, g_onorm, batch, seq):
    n_heads = lf.shape[0]
    e_b = n_heads * HEAD_DIM
    rows = HGRN_STEP_CHUNKS * CHUNK_B
    assert seq % rows == 0
    n_steps = seq // rows
    hm = pl.BlockSpec((n_heads, rows, HEAD_DIM), lambda b, n: (0, b * n_steps + n, 0))
    hm_i = pl.BlockSpec((n_heads, rows, HEAD_DIM), lambda b, n: (1, b * n_steps + n, 0))
    rm = pl.BlockSpec((rows, e_b), lambda b, n: (b * n_steps + n, 0))
    rm_zb = pl.BlockSpec((rows, e_b), lambda b, n: (b * n_steps + n, 1))
    kern = functools.partial(_hgrn_kernel, n_steps=n_steps, n_heads=n_heads)
    q, iv, szb = qi, qi, zz
    return pl.pallas_call(
        kern,
        grid=(batch, n_steps),
        in_specs=[hm, hm, hm, hm_i, rm_zb, pl.BlockSpec((1, e_b), lambda b, n: (0, 0))],
        out_specs=[
            rm,
            pl.BlockSpec((None, None, n_heads, HEAD_DIM, HEAD_DIM),
                         lambda b, n: (0, b, 0, 0, 0)),
        ],
        out_shape=[
            jax.ShapeDtypeStruct((batch * seq, e_b), BF16),
            jax.ShapeDtypeStruct((1, batch, n_heads, HEAD_DIM, HEAD_DIM), F32),
        ],
        scratch_shapes=[
            pltpu.VMEM((n_heads, HEAD_DIM, HEAD_DIM), F32),
            pltpu.VMEM((n_heads, CHUNK_B, HEAD_DIM), F32),
        ] + [pltpu.VMEM((CHUNK_B, CHUNK_B), F32) for _ in range(HEAD_UNROLL)],
        compiler_params=_params(("arbitrary", "arbitrary")),
        name="hgrn_prompt",
    )(q, lf, k, iv, szb, g_onorm)


def _hgrn_sample_kernel(q_ref, lf_ref, k_ref, i_ref, st_ref, szb_ref, gon_ref,
                        sto_ref, yb_ref, ft_scr, kt_scr, o_scr, *, n_heads, nb):
    s = pl.program_id(0)
    hd = HEAD_DIM

    @pl.when(s == 0)
    def _():
        def tb(h, carry):
            ft_scr[h] = jnp.transpose(jnp.exp2(lf_ref[h]))
            kt_scr[h] = jnp.transpose(k_ref[h])
            return carry
        lax.fori_loop(0, n_heads, tb, 0)

    n_seq = lf_ref.shape[1]
    shift = jnp.where(s == 0, 0, n_seq - s * nb)

    def one_head(h):
        fr = pltpu.roll(ft_scr[h], shift, 1)
        kr = pltpu.roll(kt_scr[h], shift, 1)
        for j in range(nb):
            fb = jnp.broadcast_to(fr[:, j:j + 1], (hd, hd))
            kb = jnp.broadcast_to(kr[:, j:j + 1], (hd, hd))
            irow = i_ref[h, pl.ds(s * nb + j, 1), :]
            s_new = fb * st_ref[j, h] + kb * irow
            sto_ref[j, h] = s_new
            qrow = jnp.broadcast_to(q_ref[h, pl.ds(s * nb + j, 1), :], (SUBLANES, hd))
            o = jnp.dot(qrow.astype(BF16), s_new.astype(BF16), preferred_element_type=F32)
            o_scr[h, pl.ds(j, 1), :] = o[0:1, :]
            yield

    def head_group(grp, carry):
        heads = [one_head(grp * SAMPLE_HEAD_UNROLL + u) for u in range(SAMPLE_HEAD_UNROLL)]
        for _ in itertools.zip_longest(*heads):
            pass
        return carry

    lax.fori_loop(0, n_heads // SAMPLE_HEAD_UNROLL, head_group, 0)

    for h in range(n_heads):
        sl = slice(h * hd, (h + 1) * hd)
        o = o_scr[h]
        ms = jnp.mean(o * o, axis=-1, keepdims=True)
        yb_ref[:, sl] = (o * lax.rsqrt(ms + EPS) * gon_ref[:, sl] * szb_ref[:, sl])


def _hgrn_sample(qi, lf, k, state, zz, g_onorm):
    n_heads, n_seq, hd = lf.shape
    nb = SAMPLE_SEQ_BLOCK
    e_b = n_heads * hd
    full3 = pl.BlockSpec((n_heads, n_seq, hd), lambda s: (0, 0, 0))
    full3_i = pl.BlockSpec((n_heads, n_seq, hd), lambda s: (1, 0, 0))
    st_spec = pl.BlockSpec((None, nb, n_heads, hd, hd), lambda s: (0, s, 0, 0, 0))
    kern = functools.partial(_hgrn_sample_kernel, n_heads=n_heads, nb=nb)
    q, iv, szb = qi, qi, zz
    return pl.pallas_call(
        kern,
        grid=(n_seq // nb,),
        in_specs=[full3, full3, full3, full3_i, st_spec,
                  pl.BlockSpec((nb, e_b), lambda s: (s, 1)),
                  pl.BlockSpec((1, e_b), lambda s: (0, 0))],
        out_specs=[st_spec, pl.BlockSpec((nb, e_b), lambda s: (s, 0))],
        out_shape=[jax.ShapeDtypeStruct(state.shape, F32),
                   jax.ShapeDtypeStruct((n_seq, e_b), F32)],
        scratch_shapes=[
            pltpu.VMEM((n_heads, hd, n_seq), F32),
            pltpu.VMEM((n_heads, hd, n_seq), F32),
            pltpu.VMEM((n_heads, nb, hd), F32),
        ],
        compiler_params=_params(("arbitrary",)),
        name="hgrn_sample",
    )(q, lf, k, iv, state, szb, g_onorm)


def _merge_kernel(yap_ref, ybp_ref, gap_ref, gbp_ref, yas_ref, ybs_ref, gas_ref, gbs_ref,
                  wpa_hbm, wpb_hbm, mp_ref, ms_ref, wa_bf, wb_bf, stage_ref, sem,
                  *, n_prompt_tiles, n_col_tiles, n_chunks):
    j = pl.program_id(0)
    i = pl.program_id(1)
    cur = j % 2
    _, _, kc, tn = stage_ref.shape
    weights = ((wpa_hbm, wa_bf), (wpb_hbm, wb_bf))

    def chunk_copy(widx, jj, c, slot):
        return pltpu.make_async_copy(
            weights[widx][0].at[0, pl.ds(c * kc, kc), pl.ds(jj * tn, tn)],
            stage_ref.at[slot, widx], sem.at[2 * slot + widx])

    @pl.when((j == 0) & (i == 0))
    def _():
        for widx in range(2):
            chunk_copy(widx, 0, 0, 0).start()
        for c in range(n_chunks):
            for widx in range(2):
                if c + 1 < n_chunks:
                    chunk_copy(widx, 0, c + 1, (c + 1) % 2).start()
                chunk_copy(widx, 0, c, c % 2).wait()
                weights[widx][1][0, c * kc:(c + 1) * kc, :] = (
                    stage_ref[c % 2, widx].astype(BF16))

    has_next = j + 1 < n_col_tiles

    @pl.when(has_next & (i >= 1))
    def _():
        for widx in range(2):
            chunk_copy(widx, j + 1, i - 1, (i - 1) % 2).wait()

    @pl.when(has_next & (i < n_chunks))
    def _():
        for widx in range(2):
            chunk_copy(widx, j + 1, i, i % 2).start()

    def cast_previous_chunk():
        prev = (i + n_chunks - 1) % n_chunks
        for widx in range(2):
            weights[widx][1][1 - cur, pl.ds(prev * kc, kc), :] = (
                stage_ref[(i + 1) % 2, widx].astype(BF16))

    def run(ya, yb, ga, gb, out):
        cast_previous_chunk()
        rows = ya.shape[0]
        sub = min(rows, SUB_ROWS)
        for m in range(rows // sub):
            rs = slice(m * sub, (m + 1) * sub)
            a = jnp.dot(ya[rs, :].astype(BF16), wa_bf[cur], preferred_element_type=F32)
            b = jnp.dot(yb[rs, :].astype(BF16), wb_bf[cur], preferred_element_type=F32)
            out[rs, :] = (ga[rs, :].astype(F32) * a + gb[rs, :].astype(F32) * b).astype(out.dtype)

    @pl.when(i != _sample_step(n_prompt_tiles))
    def _():
        run(yap_ref, ybp_ref, gap_ref, gbp_ref, mp_ref)

    @pl.when(i == _sample_step(n_prompt_tiles))
    def _():
        run(yas_ref, ybs_ref, gas_ref, gbs_ref, ms_ref)


def _merge(ya_p, yb_p, g_p, ya_s, yb_s, g_s, w_pa, w_pb, gate_col0):
    mp, kdim = ya_p.shape
    ms = ya_s.shape[0]
    d = w_pa.shape[-1]
    tm, tn = ROW_TILE, PROJ_COL_TILE
    npt = mp // tm
    nj = d // tn
    assert gate_col0 % tn == 0 and d % tn == 0
    g0 = gate_col0 // tn
    n_chunks = npt
    assert kdim % n_chunks == 0 and n_chunks >= 2
    kc = kdim // n_chunks

    def row_i(j, i):
        return _prompt_tile(j, i, npt)

    yp = pl.BlockSpec((tm, kdim), lambda j, i: (row_i(j, i), 0))
    ys = pl.BlockSpec((ms, kdim), lambda j, i: (0, 0))
    w = pl.BlockSpec(memory_space=pl.ANY)
    kern = functools.partial(_merge_kernel, n_prompt_tiles=npt, n_col_tiles=nj,
                             n_chunks=n_chunks)
    return pl.pallas_call(
        kern,
        grid=(nj, npt + 1),
        in_specs=[
            yp, yp,
            pl.BlockSpec((tm, tn), lambda j, i: (row_i(j, i), g0 + j)),
            pl.BlockSpec((tm, tn), lambda j, i: (row_i(j, i), g0 + nj + j)),
            ys, ys,
            pl.BlockSpec((ms, tn), lambda j, i: (0, g0 + j)),
            pl.BlockSpec((ms, tn), lambda j, i: (0, g0 + nj + j)),
            w, w,
        ],
        out_specs=[pl.BlockSpec((tm, tn), lambda j, i: (row_i(j, i), j)),
                   pl.BlockSpec((ms, tn), lambda j, i: (0, j))],
        out_shape=[jax.ShapeDtypeStruct((mp, d), BF16),
                   jax.ShapeDtypeStruct((ms, d), BF16)],
        scratch_shapes=[pltpu.VMEM((2, kdim, tn), BF16), pltpu.VMEM((2, kdim, tn), BF16),
                        pltpu.VMEM((2, 2, kc, tn), F32),
                        pltpu.SemaphoreType.DMA((4,))],
        compiler_params=_params(("arbitrary", "arbitrary")),
        name="gated_merge",
    )(ya_p, yb_p, g_p, g_p, ya_s, yb_s, g_s, g_s, w_pa, w_pb)


def _out_post_kernel(mp_ref, xp_ref, ms_ref, xs_ref, w_hbm, g_ref, op_ref, os_ref, z_scr,
                     w_ref, stage_ref, sem, *, n_prompt_tiles):
    s = pl.program_id(0)
    n_slots, kc, _ = stage_ref.shape
    n_w_chunks = w_ref.shape[0] // kc

    def chunk_copy(c, slot):
        return pltpu.make_async_copy(w_hbm.at[0, pl.ds(c * kc, kc), :],
                                     stage_ref.at[slot], sem.at[slot])

    def matmul(m_ref):
        z_scr[...] = jnp.dot(m_ref[...], w_ref[...], preferred_element_type=F32)

    def finish(x_ref, o_ref):
        z = z_scr[...]
        ms = jnp.mean(z * z, axis=-1, keepdims=True)
        o_ref[...] = x_ref[...] + z * lax.rsqrt(ms + EPS) * g_ref[...]

    @pl.when(s == 0)
    def _():
        for c in range(n_slots - 1):
            chunk_copy(c, c % n_slots).start()
        for c in range(n_w_chunks):
            ahead = c + n_slots - 1
            if ahead < n_w_chunks:
                chunk_copy(ahead, ahead % n_slots).start()
            chunk_copy(c, c % n_slots).wait()
            w_ref[c * kc:(c + 1) * kc, :] = stage_ref[c % n_slots].astype(BF16)
        matmul(mp_ref)

    @pl.when((s >= 1) & (s < n_prompt_tiles))
    def _():
        finish(xp_ref, op_ref)
        matmul(mp_ref)

    @pl.when(s == n_prompt_tiles)
    def _():
        finish(xp_ref, op_ref)
        matmul(ms_ref)

    @pl.when(s == n_prompt_tiles + 1)
    def _():
        finish(xs_ref, os_ref)


def _out_post(m_p, x_p, m_s, x_s, w3, g):
    mp, d = x_p.shape
    ms = x_s.shape[0]
    kdim = w3.shape[1]
    tm = OUT_ROWS
    assert ms == tm
    assert kdim % OUT_W_CHUNK == 0 and kdim // OUT_W_CHUNK >= OUT_W_SLOTS
    npt = mp // tm
    mm_row = pl.BlockSpec((tm, d), lambda s: (jnp.minimum(s, npt - 1), 0))
    fin_row = pl.BlockSpec((tm, d), lambda s: (jnp.clip(s - 1, 0, npt - 1), 0))
    full_s = pl.BlockSpec((ms, d), lambda s: (0, 0))
    kern = functools.partial(_out_post_kernel, n_prompt_tiles=npt)
    return pl.pallas_call(
        kern,
        grid=(npt + 2,),
        in_specs=[mm_row, fin_row, full_s, full_s,
                  pl.BlockSpec(memory_space=pl.ANY),
                  pl.BlockSpec((1, d), lambda s: (0, 0))],
        out_specs=[fin_row, full_s],
        out_shape=[jax.ShapeDtypeStruct((mp, d), F32), jax.ShapeDtypeStruct((ms, d), F32)],
        scratch_shapes=[pltpu.VMEM((tm, d), F32),
                        pltpu.VMEM((kdim, d), BF16),
                        pltpu.VMEM((OUT_W_SLOTS, OUT_W_CHUNK, d), F32),
                        pltpu.SemaphoreType.DMA((OUT_W_SLOTS,))],
        compiler_params=_params(("arbitrary",)),
        name="out_proj_post_norm",
    )(m_p, x_p, m_s, x_s, w3, g)


def kernel(x_prompt, x_sample, state_hgrn, lb_logits, g_pre, w_in, ln_g, ln_b, w_s, b_s,
           g_onorm, w_pa, w_pb, w_o, g_post):
    batch, seq, d = x_prompt.shape
    n_seq, dec_seq, _ = x_sample.shape
    depth = w_in.shape[0]
    assert depth == 1 and dec_seq == 1
    assert seq % CHUNK_A == 0 and seq % CHUNK_B == 0
    e = w_pa.shape[1]
    n_groups = w_s.shape[1]
    assert n_groups * CHUNK_A == e

    xp = x_prompt.reshape(batch * seq, d)
    xs = x_sample.reshape(n_seq, d)
    xn_p = _rmsnorm(xp, g_pre, NORM_ROWS)
    xn_s = _rmsnorm(xs, g_pre, n_seq)

    col = lambda idx: idx * e
    uvg_p, uvg_s = _proj(xn_p, xn_s, w_in, [(col(0), 2 * e), (col(7), 2 * d)],
                         [_ep_gelu, _ep_sigmoid],
                         [(BF16, BF16, False)], name="proj_uv_gates")
    zz_p, zz_s = _proj(xn_p, xn_s, w_in, [(col(2), e), (col(6), e)], [_ep_silu, _ep_silu],
                       [(BF16, F32, False)], name="proj_za_zb")
    qi_p, qi_s = _proj(xn_p, xn_s, w_in, [(col(3), e), (col(5), e)],
                       [_ep_silu, _ep_identity],
                       [(F32, F32, True)], name="proj_q_i")
    lf_p, lf_s, k_p, k_s = _proj(xn_p, xn_s, w_in, [(col(4), e)], [_ep_forget],
                                 [(F32, F32, True), (BF16, F32, True)],
                                 aux=(lb_logits,), name="proj_f")

    bs_rows = jnp.repeat(b_s[0].T, CHUNK_A, axis=1)
    ya_p, vrows_p = _gate_prompt(uvg_p, zz_p, ln_g, ln_b, w_s[0], bs_rows, batch, seq, e)
    wd_row = jnp.repeat(w_s[0, :, 0, 0], CHUNK_A)[None, :]
    b0_row = jnp.repeat(b_s[0, :, 0], CHUNK_A)[None, :]
    ya_s, vrows_s = _gate_sample(uvg_s, zz_s, ln_g, ln_b, wd_row, b0_row, e)

    yb_p, state_p = _hgrn_prompt(qi_p, lf_p, k_p, zz_p, g_onorm, batch, seq)
    state_s, yb_s = _hgrn_sample(qi_s, lf_s, k_s, state_hgrn, zz_s, g_onorm)

    m_p, m_s = _merge(ya_p, yb_p, uvg_p, ya_s, yb_s, uvg_s, w_pa, w_pb, 2 * e)
    y_p, y_s = _out_post(m_p, xp, m_s, xs, w_o, g_post)

    return (y_p.reshape(batch, seq, d), y_s.reshape(n_seq, 1, d),
            state_p, state_s,
            vrows_p, vrows_s.reshape(1, n_seq, 1, e))
```

```python
import functools
import itertools

import jax
import jax.numpy as jnp
from jax import lax
from jax.experimental import pallas as pl
from jax.experimental.pallas import tpu as pltpu

F32 = jnp.float32
BF16 = jnp.bfloat16

EPS = 1e-6
LOG2_E = 1.4426950408889634
LANES = 128
SUBLANES = 8
CHUNK_A = 128
GATE_STEP_CHUNKS = 8
HEAD_DIM = 128
CHUNK_B = 128
HGRN_STEP_CHUNKS = 4
HEAD_UNROLL = 8
VMEM_LIMIT_BYTES = 56 * 1024 * 1024
ROW_TILE = 1024
PROJ_COL_TILE = 1024
SUB_ROWS = 256
NORM_ROWS = 512
OUT_ROWS = 128
OUT_W_CHUNK = 32
OUT_W_SLOTS = 6
SAMPLE_SEQ_BLOCK = 8
SAMPLE_HEAD_UNROLL = 8


def _params(sem):
    return pltpu.CompilerParams(dimension_semantics=sem,
                                vmem_limit_bytes=VMEM_LIMIT_BYTES)


def _rmsnorm_kernel(x_ref, g_ref, o_ref):
    x = x_ref[...]
    ms = jnp.mean(x * x, axis=-1, keepdims=True)
    o_ref[...] = (x * lax.rsqrt(ms + EPS) * g_ref[...]).astype(o_ref.dtype)


def _rmsnorm(x, g, rows):
    m, d = x.shape
    return pl.pallas_call(
        _rmsnorm_kernel,
        grid=(m // rows,),
        in_specs=[pl.BlockSpec((rows, d), lambda i: (i, 0)),
                  pl.BlockSpec((1, d), lambda i: (0, 0))],
        out_specs=pl.BlockSpec((rows, d), lambda i: (i, 0)),
        out_shape=jax.ShapeDtypeStruct((m, d), BF16),
        compiler_params=_params(("arbitrary",)),
        name="pre_rmsnorm",
    )(x, g)


def _sample_step(n_prompt_tiles):
    return n_prompt_tiles // 2


def _prompt_tile(j, i, n_prompt_tiles):
    fwd = i - (i > _sample_step(n_prompt_tiles)).astype(jnp.int32)
    return jnp.where(j % 2 == 0, fwd, n_prompt_tiles - 1 - fwd)


def _proj_kernel(*refs, epilogues, out_head_major, n_aux, n_prompt_tiles, sections,
                 n_chunks):
    n_out = len(out_head_major)
    xp_ref, xs_ref, w_hbm = refs[:3]
    aux_refs = refs[3:3 + n_aux]
    out_refs = refs[3 + n_aux:3 + n_aux + 2 * n_out]
    wbf_ref, stage_ref, sem = refs[-3:]
    j = pl.program_id(0)
    i = pl.program_id(1)
    n_col_tiles = sum(count for _, count in sections)
    cur = j % 2
    _, kc, tn = stage_ref.shape

    def weight_col_tile(jj):
        tile, start = None, 0
        for first, count in sections:
            t = first + (jj - start)
            tile = t if tile is None else jnp.where(jj >= start, t, tile)
            start += count
        return tile

    def chunk_copy(jj, c, slot):
        return pltpu.make_async_copy(
            w_hbm.at[0, pl.ds(c * kc, kc), pl.ds(weight_col_tile(jj) * tn, tn)],
            stage_ref.at[slot], sem.at[slot])

    @pl.when((j == 0) & (i == 0))
    def _():
        chunk_copy(0, 0, 0).start()
        for c in range(n_chunks):
            if c + 1 < n_chunks:
                chunk_copy(0, c + 1, (c + 1) % 2).start()
            chunk_copy(0, c, c % 2).wait()
            wbf_ref[0, c * kc:(c + 1) * kc, :] = stage_ref[c % 2].astype(BF16)

    has_next = j + 1 < n_col_tiles

    @pl.when(has_next & (i >= 1))
    def _():
        chunk_copy(j + 1, i - 1, (i - 1) % 2).wait()

    @pl.when(has_next & (i < n_chunks))
    def _():
        chunk_copy(j + 1, i, i % 2).start()

    def cast_previous_chunk():
        prev = (i + n_chunks - 1) % n_chunks
        wbf_ref[1 - cur, pl.ds(prev * kc, kc), :] = stage_ref[(i + 1) % 2].astype(BF16)

    def run(x_ref, outs, epilogue):
        cast_previous_chunk()
        rows = x_ref.shape[0]
        sub = min(rows, SUB_ROWS)
        for m in range(rows // sub):
            rs = slice(m * sub, (m + 1) * sub)
            acc = jnp.dot(x_ref[rs, :].astype(BF16), wbf_ref[cur],
                          preferred_element_type=F32)
            res = epilogue(acc, *[a[...] for a in aux_refs])
            for r, o_ref, hm in zip(res, outs, out_head_major):
                if hm:
                    for hh in range(o_ref.shape[0]):
                        o_ref[hh, rs, :] = r[:, hh * LANES:(hh + 1) * LANES].astype(o_ref.dtype)
                else:
                    o_ref[rs, :] = r.astype(o_ref.dtype)

    is_sample = i == _sample_step(n_prompt_tiles)
    start = 0
    for (_, count), epilogue in zip(sections, epilogues):
        in_section = (j >= start) & (j < start + count)
        start += count

        @pl.when(in_section & jnp.logical_not(is_sample))
        def _(epilogue=epilogue):
            run(xp_ref, out_refs[0::2], epilogue)

        @pl.when(in_section & is_sample)
        def _(epilogue=epilogue):
            run(xs_ref, out_refs[1::2], epilogue)


def _proj(xp, xs, w3, col_sections, epilogues, outs, aux=(), tn=None, name="proj"):
    mp, kdim = xp.shape
    ms = xs.shape[0]
    tm = ROW_TILE
    tn = PROJ_COL_TILE if tn is None else tn
    npt = mp // tm
    assert all(c0 % tn == 0 and w % tn == 0 for c0, w in col_sections)
    sections = tuple((c0 // tn, w // tn) for c0, w in col_sections)
    ncols = sum(w for _, w in col_sections)
    nj = ncols // tn
    hpt = tn // LANES
    n_chunks = npt
    assert kdim % n_chunks == 0 and n_chunks >= 2
    kc = kdim // n_chunks

    def row_i(j, i):
        return _prompt_tile(j, i, npt)

    in_specs = [
        pl.BlockSpec((tm, kdim), lambda j, i: (row_i(j, i), 0)),
        pl.BlockSpec((ms, kdim), lambda j, i: (0, 0)),
        pl.BlockSpec(memory_space=pl.ANY),
    ]
    for a in aux:
        in_specs.append(pl.BlockSpec((a.shape[0], tn), lambda j, i: (0, j)))
    out_specs, out_shapes = [], []
    for pdt, sdt, hm in outs:
        if hm:
            out_specs.append(pl.BlockSpec((hpt, tm, LANES), lambda j, i: (j, row_i(j, i), 0)))
            out_shapes.append(jax.ShapeDtypeStruct((ncols // LANES, mp, LANES), pdt))
            out_specs.append(pl.BlockSpec((hpt, ms, LANES), lambda j, i: (j, 0, 0)))
            out_shapes.append(jax.ShapeDtypeStruct((ncols // LANES, ms, LANES), sdt))
        else:
            out_specs.append(pl.BlockSpec((tm, tn), lambda j, i: (row_i(j, i), j)))
            out_shapes.append(jax.ShapeDtypeStruct((mp, ncols), pdt))
            out_specs.append(pl.BlockSpec((ms, tn), lambda j, i: (0, j)))
            out_shapes.append(jax.ShapeDtypeStruct((ms, ncols), sdt))
    kern = functools.partial(
        _proj_kernel, epilogues=tuple(epilogues),
        out_head_major=tuple(hm for _, _, hm in outs),
        n_aux=len(aux), n_prompt_tiles=npt, sections=sections, n_chunks=n_chunks)
    return pl.pallas_call(
        kern,
        grid=(nj, npt + 1),
        in_specs=in_specs,
        out_specs=out_specs,
        out_shape=out_shapes,
        scratch_shapes=[pltpu.VMEM((2, kdim, tn), BF16),
                        pltpu.VMEM((2, kc, tn), F32),
                        pltpu.SemaphoreType.DMA((2,))],
        compiler_params=_params(("arbitrary", "arbitrary")),
        name=name,
    )(xp, xs, w3, *aux)


def _ep_gelu(acc):
    return (jax.nn.gelu(acc, approximate=True),)


def _ep_silu(acc):
    return (acc * jax.nn.sigmoid(acc),)


def _ep_sigmoid(acc):
    return (jax.nn.sigmoid(acc),)


def _ep_identity(acc):
    return (acc,)


def _ep_forget(acc, lbl):
    m = jnp.max(lbl, axis=0, keepdims=True)
    e = jnp.exp(lbl - m)
    lb = e[0:1, :] / jnp.sum(e, axis=0, keepdims=True)
    sig = jax.nn.sigmoid(acc)
    log2_f = jnp.log(lb + (1.0 - lb) * sig) * LOG2_E
    k = (1.0 - lb) * (1.0 - sig)
    return log2_f, k


def _layernorm(gv, g, b):
    mu = jnp.mean(gv, axis=-1, keepdims=True)
    xc = gv - mu
    var = jnp.mean(xc * xc, axis=-1, keepdims=True)
    return xc * lax.rsqrt(var + EPS) * g + b


def _gate_kernel(u_ref, gv_ref, sza_ref, lng_ref, lnb_ref, ws_ref, bs_ref,
                 ya_ref, vr_ref, wm_ref, *, n_steps, n_groups):
    n = pl.program_id(1)

    @pl.when((pl.program_id(0) == 0) & (n == 0))
    def _():
        r = lax.broadcasted_iota(jnp.int32, (CHUNK_A, CHUNK_A), 0)
        c = lax.broadcasted_iota(jnp.int32, (CHUNK_A, CHUNK_A), 1)
        for g in range(n_groups):
            wm_ref[g] = jnp.where(c <= r, ws_ref[g], 0.0).astype(BF16)

    step_chunks = gv_ref.shape[0] // CHUNK_A
    for cc in range(step_chunks):
        rs = slice(cc * CHUNK_A, (cc + 1) * CHUNK_A)
        vn = _layernorm(gv_ref[rs, :].astype(F32), lng_ref[...], lnb_ref[...])

        if cc == step_chunks - 1:
            @pl.when(n == n_steps - 1)
            def _():
                vr_ref[...] = vn

        vnb = vn.astype(BF16)
        for g in range(n_groups):
            sl = slice(g * LANES, (g + 1) * LANES)
            mixed = jnp.dot(wm_ref[g], vnb[:, sl], preferred_element_type=F32) + bs_ref[:, sl]
            ya_ref[rs, sl] = (u_ref[rs, sl].astype(F32) * mixed
                              * sza_ref[rs, sl].astype(F32)).astype(ya_ref.dtype)


def _gate_prompt(uv, sza, ln_g, ln_b, w_s, bs_rows, batch, seq, e_a):
    rows = GATE_STEP_CHUNKS * CHUNK_A
    assert seq % rows == 0
    n_steps = seq // rows
    n_groups = w_s.shape[0]
    blk = lambda b, n: (b * n_steps + n, 0)
    kern = functools.partial(_gate_kernel, n_steps=n_steps, n_groups=n_groups)
    return pl.pallas_call(
        kern,
        grid=(batch, n_steps),
        in_specs=[
            pl.BlockSpec((rows, e_a), blk),
            pl.BlockSpec((rows, e_a), lambda b, n: (b * n_steps + n, 1)),
            pl.BlockSpec((rows, e_a), blk),
            pl.BlockSpec((1, e_a), lambda b, n: (0, 0)),
            pl.BlockSpec((1, e_a), lambda b, n: (0, 0)),
            pl.BlockSpec(w_s.shape, lambda b, n: (0, 0, 0)),
            pl.BlockSpec((CHUNK_A, e_a), lambda b, n: (0, 0)),
        ],
        out_specs=[
            pl.BlockSpec((rows, e_a), blk),
            pl.BlockSpec((None, None, CHUNK_A, e_a), lambda b, n: (0, b, 0, 0)),
        ],
        out_shape=[
            jax.ShapeDtypeStruct((batch * seq, e_a), BF16),
            jax.ShapeDtypeStruct((1, batch, CHUNK_A, e_a), F32),
        ],
        scratch_shapes=[pltpu.VMEM((n_groups, CHUNK_A, CHUNK_A), BF16)],
        compiler_params=_params(("arbitrary", "arbitrary")),
        name="spatial_gate_prompt",
    )(uv, uv, sza, ln_g, ln_b, w_s, bs_rows)


def _gate_sample_kernel(u_ref, gv_ref, sza_ref, lng_ref, lnb_ref, wd_ref, b0_ref,
                        ya_ref, vr_ref):
    vn = _layernorm(gv_ref[...].astype(F32), lng_ref[...], lnb_ref[...])
    vr_ref[...] = vn
    mixed = wd_ref[...] * vn + b0_ref[...]
    ya_ref[...] = (u_ref[...].astype(F32) * mixed * sza_ref[...].astype(F32)).astype(ya_ref.dtype)


def _gate_sample(uv, sza, ln_g, ln_b, wd_row, b0_row, e_a):
    m = sza.shape[0]
    full = lambda i: (0, 0)
    return pl.pallas_call(
        _gate_sample_kernel,
        grid=(1,),
        in_specs=[
            pl.BlockSpec((m, e_a), full),
            pl.BlockSpec((m, e_a), lambda i: (0, 1)),
            pl.BlockSpec((m, e_a), full),
            pl.BlockSpec((1, e_a), full),
            pl.BlockSpec((1, e_a), full),
            pl.BlockSpec((1, e_a), full),
            pl.BlockSpec((1, e_a), full),
        ],
        out_specs=[pl.BlockSpec((m, e_a), full), pl.BlockSpec((m, e_a), full)],
        out_shape=[jax.ShapeDtypeStruct((m, e_a), BF16),
                   jax.ShapeDtypeStruct((m, e_a), F32)],
        compiler_params=_params(("arbitrary",)),
        name="spatial_gate_sample",
    )(uv, uv, sza, ln_g, ln_b, wd_row, b0_row)


def _dot_nt(a, b):
    return lax.dot_general(a, b, (((1,), (1,)), ((), ())), preferred_element_type=F32)


def _dot_tn(a, b):
    return lax.dot_general(a, b, (((0,), (0,)), ((), ())), preferred_element_type=F32)


def _hgrn_kernel(q_ref, lf_ref, k_ref, i_ref, szb_ref, gon_ref, yb_ref, st_ref,
                 s_scr, o_scr, *att_scrs, n_steps, n_heads):
    n = pl.program_id(1)
    c, hd, sb = CHUNK_B, HEAD_DIM, SUBLANES
    nv = c // sb

    @pl.when((pl.program_id(0) == 0) & (n == 0))
    def _():
        for att_scr in att_scrs:
            att_scr[...] = jnp.zeros_like(att_scr)

    @pl.when(n == 0)
    def _():
        s_scr[...] = jnp.zeros_like(s_scr)

    r8 = lax.broadcasted_iota(jnp.int32, (1, sb, hd), 1)
    t_idx = lax.broadcasted_iota(jnp.int32, (c, c), 0)
    s_idx = lax.broadcasted_iota(jnp.int32, (c, c), 1)
    fine = [h for h in (1, 2, 4) if 2 * h <= sb]
    coarse = [1 << l for l in range(sb.bit_length() - 1, (c // 2).bit_length())]
    second8 = {h: (r8 & (2 * h - 1)) >= h for h in fine}
    fine_mask = {}
    for h in fine:
        same = (t_idx >> h.bit_length()) == (s_idx >> h.bit_length())
        if h == 1:
            fine_mask[h] = same & (s_idx <= t_idx)
        else:
            fine_mask[h] = same & ((t_idx & (2 * h - 1)) >= h) & ((s_idx & (2 * h - 1)) < h)

    def tiles(x):
        return x.reshape(nv, sb, hd)

    def fine_boundary(p3, h):
        acc = None
        for blk in range(sb // (2 * h)):
            r = blk * 2 * h + h - 1
            row = jnp.broadcast_to(p3[:, r:r + 1, :], (nv, sb, hd))
            acc = row if acc is None else jnp.where(r8 >= blk * 2 * h, row, acc)
        return acc

    def one_head(h, rows, att_ref):
        lf = lf_ref[h, rows, :]
        q = q_ref[h, rows, :]
        k = k_ref[h, rows, :].astype(F32)
        ib = i_ref[h, rows, :].astype(BF16)
        q3, k3, lf3 = tiles(q), tiles(k), tiles(lf)

        odd = second8[1]
        qf = jnp.where(odd, q3 * jnp.exp2(lf3), q3).reshape(c, hd).astype(BF16)
        kf = jnp.where(odd, k3 * jnp.exp2(-lf3), k3).reshape(c, hd).astype(BF16)
        att_f = jnp.where(fine_mask[1], _dot_nt(qf, kf), 0.0)
        p3 = lf3 + jnp.where(odd, tiles(pltpu.roll(lf, 1, 0)), 0.0)
        yield
        for hh in fine[1:]:
            bnd = fine_boundary(p3, hh)
            sec = second8[hh]
            g = jnp.where(sec, p3, bnd - p3)
            x = (jnp.where(sec, q3, k3) * jnp.exp2(g)).reshape(c, hd).astype(BF16)
            att_f = jnp.where(fine_mask[hh], _dot_nt(x, x), att_f)
            p3 = p3 + jnp.where(sec, bnd, 0.0)
            yield
        for v in range(nv):
            blk = slice(v * sb, (v + 1) * sb)
            att_ref[blk, blk] = att_f[blk, blk]

        p = p3.reshape(c, hd)
        for hh in coarse:
            xs, ps = [], []
            for blk in range(c // (2 * hh)):
                lo, mid, hi = blk * 2 * hh, blk * 2 * hh + hh, (blk + 1) * 2 * hh
                bnd = jnp.broadcast_to(p[mid - 1:mid, :], (hh, hd))
                p_lo, p_hi = p[lo:mid], p[mid:hi]
                xs.append(k[lo:mid] * jnp.exp2(bnd - p_lo))
                xs.append(q[mid:hi] * jnp.exp2(p_hi))
                ps.append(p_lo)
                ps.append(p_hi + bnd)
            x = jnp.concatenate(xs, axis=0).astype(BF16)
            a = _dot_nt(x, x)
            for blk in range(c // (2 * hh)):
                lo, mid, hi = blk * 2 * hh, blk * 2 * hh + hh, (blk + 1) * 2 * hh
                att_ref[mid:hi, lo:mid] = a[mid:hi, lo:mid]
            p = jnp.concatenate(ps, axis=0)
            yield

        b_last = p[c - 1:c, :]
        qt = (q * jnp.exp2(p)).astype(BF16)
        kt = (k * jnp.exp2(jnp.broadcast_to(b_last, (c, hd)) - p)).astype(BF16)
        dec = jnp.exp2(b_last)
        s_old = s_scr[h]
        lhs = jnp.concatenate([att_ref[...].astype(BF16), qt], axis=1)
        rhs = jnp.concatenate([ib, s_old.astype(BF16)], axis=0)
        o_scr[h] = jnp.dot(lhs, rhs, preferred_element_type=F32)
        dec_col = jnp.transpose(jnp.broadcast_to(dec, (hd, hd)))
        s_scr[h] = dec_col * s_old + _dot_tn(kt, ib)

    def chunk_body(cc, chunk_carry):
        rows = pl.ds(pl.multiple_of(cc * c, c), c)

        def head_group(grp, carry):
            heads = [one_head(grp * HEAD_UNROLL + u, rows, att_scrs[u])
                     for u in range(HEAD_UNROLL)]
            for _ in itertools.zip_longest(*heads):
                pass
            return carry

        lax.fori_loop(0, n_heads // HEAD_UNROLL, head_group, 0)

        for h in range(n_heads):
            sl = slice(h * hd, (h + 1) * hd)
            o = o_scr[h]
            ms = jnp.mean(o * o, axis=-1, keepdims=True)
            yb_ref[rows, sl] = (o * lax.rsqrt(ms + EPS) * gon_ref[:, sl]
                                * szb_ref[rows, sl].astype(F32)).astype(yb_ref.dtype)
        return chunk_carry

    lax.fori_loop(0, q_ref.shape[1] // c, chunk_body, 0)

    @pl.when(n == n_steps - 1)
    def _():
        st_ref[...] = s_scr[...]


def _hgrn_prompt(qi, lf, k, zz, g_onorm, batch, seq):
    n_heads = lf.shape[0]
    e_b = n_heads * HEAD_DIM
    rows = HGRN_STEP_CHUNKS * CHUNK_B
    assert seq % rows == 0
    n_steps = seq // rows
    hm = pl.BlockSpec((n_heads, rows, HEAD_DIM), lambda b, n: (0, b * n_steps + n, 0))
    hm_i = pl.BlockSpec((n_heads, rows, HEAD_DIM), lambda b, n: (1, b * n_steps + n, 0))
    rm = pl.BlockSpec((rows, e_b), lambda b, n: (b * n_steps + n, 0))
    rm_zb = pl.BlockSpec((rows, e_b), lambda b, n: (b * n_steps + n, 1))
    kern = functools.partial(_hgrn_kernel, n_steps=n_steps, n_heads=n_heads)
    q, iv, szb = qi, qi, zz
    return pl.pallas_call(
        kern,
        grid=(batch, n_steps),
        in_specs=[hm, hm, hm, hm_i, rm_zb, pl.BlockSpec((1, e_b), lambda b, n: (0, 0))],
        out_specs=[
            rm,
            pl.BlockSpec((None, None, n_heads, HEAD_DIM, HEAD_DIM),
                         lambda b, n: (0, b, 0, 0, 0)),
        ],
        out_shape=[
            jax.ShapeDtypeStruct((batch * seq, e_b), BF16),
            jax.ShapeDtypeStruct((1, batch, n_heads, HEAD_DIM, HEAD_DIM), F32),
        ],
        scratch_shapes=[
            pltpu.VMEM((n_heads, HEAD_DIM, HEAD_DIM), F32),
            pltpu.VMEM((n_heads, CHUNK_B, HEAD_DIM), F32),
        ] + [pltpu.VMEM((CHUNK_B, CHUNK_B), F32) for _ in range(HEAD_UNROLL)],
        compiler_params=_params(("arbitrary", "arbitrary")),
        name="hgrn_prompt",
    )(q, lf, k, iv, szb, g_onorm)


def _hgrn_sample_kernel(q_ref, lf_ref, k_ref, i_ref, st_ref, szb_ref, gon_ref,
                        sto_ref, yb_ref, ft_scr, kt_scr, o_scr, *, n_heads, nb):
    s = pl.program_id(0)
    hd = HEAD_DIM

    @pl.when(s == 0)
    def _():
        def tb(h, carry):
            ft_scr[h] = jnp.transpose(jnp.exp2(lf_ref[h]))
            kt_scr[h] = jnp.transpose(k_ref[h])
            return carry
        lax.fori_loop(0, n_heads, tb, 0)

    n_seq = lf_ref.shape[1]
    shift = jnp.where(s == 0, 0, n_seq - s * nb)

    def one_head(h):
        fr = pltpu.roll(ft_scr[h], shift, 1)
        kr = pltpu.roll(kt_scr[h], shift, 1)
        for j in range(nb):
            fb = jnp.broadcast_to(fr[:, j:j + 1], (hd, hd))
            kb = jnp.broadcast_to(kr[:, j:j + 1], (hd, hd))
            irow = i_ref[h, pl.ds(s * nb + j, 1), :]
            s_new = fb * st_ref[j, h] + kb * irow
            sto_ref[j, h] = s_new
            qrow = jnp.broadcast_to(q_ref[h, pl.ds(s * nb + j, 1), :], (SUBLANES, hd))
            o = jnp.dot(qrow.astype(BF16), s_new.astype(BF16), preferred_element_type=F32)
            o_scr[h, pl.ds(j, 1), :] = o[0:1, :]
            yield

    def head_group(grp, carry):
        heads = [one_head(grp * SAMPLE_HEAD_UNROLL + u) for u in range(SAMPLE_HEAD_UNROLL)]
        for _ in itertools.zip_longest(*heads):
            pass
        return carry

    lax.fori_loop(0, n_heads // SAMPLE_HEAD_UNROLL, head_group, 0)

    for h in range(n_heads):
        sl = slice(h * hd, (h + 1) * hd)
        o = o_scr[h]
        ms = jnp.mean(o * o, axis=-1, keepdims=True)
        yb_ref[:, sl] = (o * lax.rsqrt(ms + EPS) * gon_ref[:, sl] * szb_ref[:, sl])


def _hgrn_sample(qi, lf, k, state, zz, g_onorm):
    n_heads, n_seq, hd = lf.shape
    nb = SAMPLE_SEQ_BLOCK
    e_b = n_heads * hd
    full3 = pl.BlockSpec((n_heads, n_seq, hd), lambda s: (0, 0, 0))
    full3_i = pl.BlockSpec((n_heads, n_seq, hd), lambda s: (1, 0, 0))
    st_spec = pl.BlockSpec((None, nb, n_heads, hd, hd), lambda s: (0, s, 0, 0, 0))
    kern = functools.partial(_hgrn_sample_kernel, n_heads=n_heads, nb=nb)
    q, iv, szb = qi, qi, zz
    return pl.pallas_call(
        kern,
        grid=(n_seq // nb,),
        in_specs=[full3, full3, full3, full3_i, st_spec,
                  pl.BlockSpec((nb, e_b), lambda s: (s, 1)),
                  pl.BlockSpec((1, e_b), lambda s: (0, 0))],
        out_specs=[st_spec, pl.BlockSpec((nb, e_b), lambda s: (s, 0))],
        out_shape=[jax.ShapeDtypeStruct(state.shape, F32),
                   jax.ShapeDtypeStruct((n_seq, e_b), F32)],
        scratch_shapes=[
            pltpu.VMEM((n_heads, hd, n_seq), F32),
            pltpu.VMEM((n_heads, hd, n_seq), F32),
            pltpu.VMEM((n_heads, nb, hd), F32),
        ],
        compiler_params=_params(("arbitrary",)),
        name="hgrn_sample",
    )(q, lf, k, iv, state, szb, g_onorm)


def _merge_kernel(yap_ref, ybp_ref, gap_ref, gbp_ref, yas_ref, ybs_ref, gas_ref, gbs_ref,
                  wpa_hbm, wpb_hbm, mp_ref, ms_ref, wa_bf, wb_bf, stage_ref, sem,
                  *, n_prompt_tiles, n_col_tiles, n_chunks):
    j = pl.program_id(0)
    i = pl.program_id(1)
    cur = j % 2
    _, _, kc, tn = stage_ref.shape
    weights = ((wpa_hbm, wa_bf), (wpb_hbm, wb_bf))

    def chunk_copy(widx, jj, c, slot):
        return pltpu.make_async_copy(
            weights[widx][0].at[0, pl.ds(c * kc, kc), pl.ds(jj * tn, tn)],
            stage_ref.at[slot, widx], sem.at[2 * slot + widx])

    @pl.when((j == 0) & (i == 0))
    def _():
        for widx in range(2):
            chunk_copy(widx, 0, 0, 0).start()
        for c in range(n_chunks):
            for widx in range(2):
                if c + 1 < n_chunks:
                    chunk_copy(widx, 0, c + 1, (c + 1) % 2).start()
                chunk_copy(widx, 0, c, c % 2).wait()
                weights[widx][1][0, c * kc:(c + 1) * kc, :] = (
                    stage_ref[c % 2, widx].astype(BF16))

    has_next = j + 1 < n_col_tiles

    @pl.when(has_next & (i >= 1))
    def _():
        for widx in range(2):
            chunk_copy(widx, j + 1, i - 1, (i - 1) % 2).wait()

    @pl.when(has_next & (i < n_chunks))
    def _():
        for widx in range(2):
            chunk_copy(widx, j + 1, i, i % 2).start()

    def cast_previous_chunk():
        prev = (i + n_chunks - 1) % n_chunks
        for widx in range(2):
            weights[widx][1][1 - cur, pl.ds(prev * kc, kc), :] = (
                stage_ref[(i + 1) % 2, widx].astype(BF16))

    def run(ya, yb, ga, gb, out):
        cast_previous_chunk()
        rows = ya.shape[0]
        sub = min(rows, SUB_ROWS)
        for m in range(rows // sub):
            rs = slice(m * sub, (m + 1) * sub)
            a = jnp.dot(ya[rs, :].astype(BF16), wa_bf[cur], preferred_element_type=F32)
            b = jnp.dot(yb[rs, :].astype(BF16), wb_bf[cur], preferred_element_type=F32)
            out[rs, :] = (ga[rs, :].astype(F32) * a + gb[rs, :].astype(F32) * b).astype(out.dtype)

    @pl.when(i != _sample_step(n_prompt_tiles))
    def _():
        run(yap_ref, ybp_ref, gap_ref, gbp_ref, mp_ref)

    @pl.when(i == _sample_step(n_prompt_tiles))
    def _():
        run(yas_ref, ybs_ref, gas_ref, gbs_ref, ms_ref)


def _merge(ya_p, yb_p, g_p, ya_s, yb_s, g_s, w_pa, w_pb, gate_col0):
    mp, kdim = ya_p.shape
    ms = ya_s.shape[0]
    d = w_pa.shape[-1]
    tm, tn = ROW_TILE, PROJ_COL_TILE
    npt = mp // tm
    nj = d // tn
    assert gate_col0 % tn == 0 and d % tn == 0
    g0 = gate_col0 // tn
    n_chunks = npt
    assert kdim % n_chunks == 0 and n_chunks >= 2
    kc = kdim // n_chunks

    def row_i(j, i):
        return _prompt_tile(j, i, npt)

    yp = pl.BlockSpec((tm, kdim), lambda j, i: (row_i(j, i), 0))
    ys = pl.BlockSpec((ms, kdim), lambda j, i: (0, 0))
    w = pl.BlockSpec(memory_space=pl.ANY)
    kern = functools.partial(_merge_kernel, n_prompt_tiles=npt, n_col_tiles=nj,
                             n_chunks=n_chunks)
    return pl.pallas_call(
        kern,
        grid=(nj, npt + 1),
        in_specs=[
            yp, yp,
            pl.BlockSpec((tm, tn), lambda j, i: (row_i(j, i), g0 + j)),
            pl.BlockSpec((tm, tn), lambda j, i: (row_i(j, i), g0 + nj + j)),
            ys, ys,
            pl.BlockSpec((ms, tn), lambda j, i: (0, g0 + j)),
            pl.BlockSpec((ms, tn), lambda j, i: (0, g0 + nj + j)),
            w, w,
        ],
        out_specs=[pl.BlockSpec((tm, tn), lambda j, i: (row_i(j, i), j)),
                   pl.BlockSpec((ms, tn), lambda j, i: (0, j))],
        out_shape=[jax.ShapeDtypeStruct((mp, d), BF16),
                   jax.ShapeDtypeStruct((ms, d), BF16)],
        scratch_shapes=[pltpu.VMEM((2, kdim, tn), BF16), pltpu.VMEM((2, kdim, tn), BF16),
                        pltpu.VMEM((2, 2, kc, tn), F32),
                        pltpu.SemaphoreType.DMA((4,))],
        compiler_params=_params(("arbitrary", "arbitrary")),
        name="gated_merge",
    )(ya_p, yb_p, g_p, g_p, ya_s, yb_s, g_s, g_s, w_pa, w_pb)


def _out_post_kernel(mp_ref, xp_ref, ms_ref, xs_ref, w_hbm, g_ref, op_ref, os_ref, z_scr,
                     w_ref, stage_ref, sem, *, n_prompt_tiles):
    s = pl.program_id(0)
    n_slots, kc, _ = stage_ref.shape
    n_w_chunks = w_ref.shape[0] // kc

    def chunk_copy(c, slot):
        return pltpu.make_async_copy(w_hbm.at[0, pl.ds(c * kc, kc), :],
                                     stage_ref.at[slot], sem.at[slot])

    def matmul(m_ref):
        z_scr[...] = jnp.dot(m_ref[...], w_ref[...], preferred_element_type=F32)

    def finish(x_ref, o_ref):
        z = z_scr[...]
        ms = jnp.mean(z * z, axis=-1, keepdims=True)
        o_ref[...] = x_ref[...] + z * lax.rsqrt(ms + EPS) * g_ref[...]

    @pl.when(s == 0)
    def _():
        for c in range(n_slots - 1):
            chunk_copy(c, c % n_slots).start()
        for c in range(n_w_chunks):
            ahead = c + n_slots - 1
            if ahead < n_w_chunks:
                chunk_copy(ahead, ahead % n_slots).start()
            chunk_copy(c, c % n_slots).wait()
            w_ref[c * kc:(c + 1) * kc, :] = stage_ref[c % n_slots].astype(BF16)
        matmul(mp_ref)

    @pl.when((s >= 1) & (s < n_prompt_tiles))
    def _():
        finish(xp_ref, op_ref)
        matmul(mp_ref)

    @pl.when(s == n_prompt_tiles)
    def _():
        finish(xp_ref, op_ref)
        matmul(ms_ref)

    @pl.when(s == n_prompt_tiles + 1)
    def _():
        finish(xs_ref, os_ref)


def _out_post(m_p, x_p, m_s, x_s, w3, g):
    mp, d = x_p.shape
    ms = x_s.shape[0]
    kdim = w3.shape[1]
    tm = OUT_ROWS
    assert ms == tm
    assert kdim % OUT_W_CHUNK == 0 and kdim // OUT_W_CHUNK >= OUT_W_SLOTS
    npt = mp // tm
    mm_row = pl.BlockSpec((tm, d), lambda s: (jnp.minimum(s, npt - 1), 0))
    fin_row = pl.BlockSpec((tm, d), lambda s: (jnp.clip(s - 1, 0, npt - 1), 0))
    full_s = pl.BlockSpec((ms, d), lambda s: (0, 0))
    kern = functools.partial(_out_post_kernel, n_prompt_tiles=npt)
    return pl.pallas_call(
        kern,
        grid=(npt + 2,),
        in_specs=[mm_row, fin_row, full_s, full_s,
                  pl.BlockSpec(memory_space=pl.ANY),
                  pl.BlockSpec((1, d), lambda s: (0, 0))],
        out_specs=[fin_row, full_s],
        out_shape=[jax.ShapeDtypeStruct((mp, d), F32), jax.ShapeDtypeStruct((ms, d), F32)],
        scratch_shapes=[pltpu.VMEM((tm, d), F32),
                        pltpu.VMEM((kdim, d), BF16),
                        pltpu.VMEM((OUT_W_SLOTS, OUT_W_CHUNK, d), F32),
                        pltpu.SemaphoreType.DMA((OUT_W_SLOTS,))],
        compiler_params=_params(("arbitrary",)),
        name="out_proj_post_norm",
    )(m_p, x_p, m_s, x_s, w3, g)


def kernel(x_prompt, x_sample, state_hgrn, lb_logits, g_pre, w_in, ln_g, ln_b, w_s, b_s,
           g_onorm, w_pa, w_pb, w_o, g_post):
    batch, seq, d = x_prompt.shape
    n_seq, dec_seq, _ = x_sample.shape
    depth = w_in.shape[0]
    assert depth == 1 and dec_seq == 1
    assert seq % CHUNK_A == 0 and seq % CHUNK_B == 0
    e = w_pa.shape[1]
    n_groups = w_s.shape[1]
    assert n_groups * CHUNK_A == e

    xp = x_prompt.reshape(batch * seq, d)
    xs = x_sample.reshape(n_seq, d)
    xn_p = _rmsnorm(xp, g_pre, NORM_ROWS)
    xn_s = _rmsnorm(xs, g_pre, n_seq)

    col = lambda idx: idx * e
    uvg_p, uvg_s = _proj(xn_p, xn_s, w_in, [(col(0), 2 * e), (col(7), 2 * d)],
                         [_ep_gelu, _ep_sigmoid],
                         [(BF16, BF16, False)], name="proj_uv_gates")
    zz_p, zz_s = _proj(xn_p, xn_s, w_in, [(col(2), e), (col(6), e)], [_ep_silu, _ep_silu],
                       [(BF16, F32, False)], name="proj_za_zb")
    qi_p, qi_s = _proj(xn_p, xn_s, w_in, [(col(3), e), (col(5), e)],
                       [_ep_silu, _ep_identity],
                       [(F32, F32, True)], name="proj_q_i")
    lf_p, lf_s, k_p, k_s = _proj(xn_p, xn_s, w_in, [(col(4), e)], [_ep_forget],
                                 [(F32, F32, True), (BF16, F32, True)],
                                 aux=(lb_logits,), name="proj_f")

    bs_rows = jnp.repeat(b_s[0].T, CHUNK_A, axis=1)
    ya_p, vrows_p = _gate_prompt(uvg_p, zz_p, ln_g, ln_b, w_s[0], bs_rows, batch, seq, e)
    wd_row = jnp.repeat(w_s[0, :, 0, 0], CHUNK_A)[None, :]
    b0_row = jnp.repeat(b_s[0, :, 0], CHUNK_A)[None, :]
    ya_s, vrows_s = _gate_sample(uvg_s, zz_s, ln_g, ln_b, wd_row, b0_row, e)

    yb_p, state_p = _hgrn_prompt(qi_p, lf_p, k_p, zz_p, g_onorm, batch, seq)
    state_s, yb_s = _hgrn_sample(qi_s, lf_s, k_s, state_hgrn, zz_s, g_onorm)

    m_p, m_s = _merge(ya_p, yb_p, uvg_p, ya_s, yb_s, uvg_s, w_pa, w_pb, 2 * e)
    y_p, y_s = _out_post(m_p, xp, m_s, xs, w_o, g_post)

    return (y_p.reshape(batch, seq, d), y_s.reshape(n_seq, 1, d),
            state_p, state_s,
            vrows_p, vrows_s.reshape(1, n_seq, 1, e))
```

```python
import functools
import itertools

import jax
import jax.numpy as jnp
from jax import lax
from jax.experimental import pallas as pl
from jax.experimental.pallas import tpu as pltpu

F32 = jnp.float32
BF16 = jnp.bfloat16

EPS = 1e-6
LOG2_E = 1.4426950408889634
LANES = 128
SUBLANES = 8
CHUNK_A = 128
GATE_STEP_CHUNKS = 8
HEAD_DIM = 128
CHUNK_B = 128
HGRN_STEP_CHUNKS = 4
HEAD_UNROLL = 4
VMEM_LIMIT_BYTES = 56 * 1024 * 1024
ROW_TILE = 1024
PROJ_COL_TILE = 1024
SUB_ROWS = 256
NORM_ROWS = 512
OUT_ROWS = 128
OUT_W_CHUNK = 32
OUT_W_SLOTS = 6
SAMPLE_SEQ_BLOCK = 8
SAMPLE_HEAD_UNROLL = 8


def _params(sem):
    return pltpu.CompilerParams(dimension_semantics=sem,
                                vmem_limit_bytes=VMEM_LIMIT_BYTES)


def _rmsnorm_kernel(x_ref, g_ref, o_ref):
    x = x_ref[...]
    ms = jnp.mean(x * x, axis=-1, keepdims=True)
    o_ref[...] = (x * lax.rsqrt(ms + EPS) * g_ref[...]).astype(o_ref.dtype)


def _rmsnorm(x, g, rows):
    m, d = x.shape
    return pl.pallas_call(
        _rmsnorm_kernel,
        grid=(m // rows,),
        in_specs=[pl.BlockSpec((rows, d), lambda i: (i, 0)),
                  pl.BlockSpec((1, d), lambda i: (0, 0))],
        out_specs=pl.BlockSpec((rows, d), lambda i: (i, 0)),
        out_shape=jax.ShapeDtypeStruct((m, d), BF16),
        compiler_params=_params(("arbitrary",)),
        name="pre_rmsnorm",
    )(x, g)


def _sample_step(n_prompt_tiles):
    return n_prompt_tiles // 2


def _prompt_tile(j, i, n_prompt_tiles):
    fwd = i - (i > _sample_step(n_prompt_tiles)).astype(jnp.int32)
    return jnp.where(j % 2 == 0, fwd, n_prompt_tiles - 1 - fwd)


def _proj_kernel(*refs, epilogues, out_head_major, n_aux, n_prompt_tiles, sections,
                 n_chunks):
    n_out = len(out_head_major)
    xp_ref, xs_ref, w_hbm = refs[:3]
    aux_refs = refs[3:3 + n_aux]
    out_refs = refs[3 + n_aux:3 + n_aux + 2 * n_out]
    wbf_ref, stage_ref, sem = refs[-3:]
    j = pl.program_id(0)
    i = pl.program_id(1)
    n_col_tiles = sum(count for _, count in sections)
    cur = j % 2
    _, kc, tn = stage_ref.shape

    def weight_col_tile(jj):
        tile, start = None, 0
        for first, count in sections:
            t = first + (jj - start)
            tile = t if tile is None else jnp.where(jj >= start, t, tile)
            start += count
        return tile

    def chunk_copy(jj, c, slot):
        return pltpu.make_async_copy(
            w_hbm.at[0, pl.ds(c * kc, kc), pl.ds(weight_col_tile(jj) * tn, tn)],
            stage_ref.at[slot], sem.at[slot])

    @pl.when((j == 0) & (i == 0))
    def _():
        chunk_copy(0, 0, 0).start()
        for c in range(n_chunks):
            if c + 1 < n_chunks:
                chunk_copy(0, c + 1, (c + 1) % 2).start()
            chunk_copy(0, c, c % 2).wait()
            wbf_ref[0, c * kc:(c + 1) * kc, :] = stage_ref[c % 2].astype(BF16)

    has_next = j + 1 < n_col_tiles

    @pl.when(has_next & (i >= 1))
    def _():
        chunk_copy(j + 1, i - 1, (i - 1) % 2).wait()

    @pl.when(has_next & (i < n_chunks))
    def _():
        chunk_copy(j + 1, i, i % 2).start()

    def cast_previous_chunk():
        prev = (i + n_chunks - 1) % n_chunks
        wbf_ref[1 - cur, pl.ds(prev * kc, kc), :] = stage_ref[(i + 1) % 2].astype(BF16)

    def run(x_ref, outs, epilogue):
        cast_previous_chunk()
        rows = x_ref.shape[0]
        sub = min(rows, SUB_ROWS)
        for m in range(rows // sub):
            rs = slice(m * sub, (m + 1) * sub)
            acc = jnp.dot(x_ref[rs, :].astype(BF16), wbf_ref[cur],
                          preferred_element_type=F32)
            res = epilogue(acc, *[a[...] for a in aux_refs])
            for r, o_ref, hm in zip(res, outs, out_head_major):
                if hm:
                    for hh in range(o_ref.shape[0]):
                        o_ref[hh, rs, :] = r[:, hh * LANES:(hh + 1) * LANES].astype(o_ref.dtype)
                else:
                    o_ref[rs, :] = r.astype(o_ref.dtype)

    is_sample = i == _sample_step(n_prompt_tiles)
    start = 0
    for (_, count), epilogue in zip(sections, epilogues):
        in_section = (j >= start) & (j < start + count)
        start += count

        @pl.when(in_section & jnp.logical_not(is_sample))
        def _(epilogue=epilogue):
            run(xp_ref, out_refs[0::2], epilogue)

        @pl.when(in_section & is_sample)
        def _(epilogue=epilogue):
            run(xs_ref, out_refs[1::2], epilogue)


def _proj(xp, xs, w3, col_sections, epilogues, outs, aux=(), tn=None, name="proj"):
    mp, kdim = xp.shape
    ms = xs.shape[0]
    tm = ROW_TILE
    tn = PROJ_COL_TILE if tn is None else tn
    npt = mp // tm
    assert all(c0 % tn == 0 and w % tn == 0 for c0, w in col_sections)
    sections = tuple((c0 // tn, w // tn) for c0, w in col_sections)
    ncols = sum(w for _, w in col_sections)
    nj = ncols // tn
    hpt = tn // LANES
    n_chunks = npt
    assert kdim % n_chunks == 0 and n_chunks >= 2
    kc = kdim // n_chunks

    def row_i(j, i):
        return _prompt_tile(j, i, npt)

    in_specs = [
        pl.BlockSpec((tm, kdim), lambda j, i: (row_i(j, i), 0)),
        pl.BlockSpec((ms, kdim), lambda j, i: (0, 0)),
        pl.BlockSpec(memory_space=pl.ANY),
    ]
    for a in aux:
        in_specs.append(pl.BlockSpec((a.shape[0], tn), lambda j, i: (0, j)))
    out_specs, out_shapes = [], []
    for pdt, sdt, hm in outs:
        if hm:
            out_specs.append(pl.BlockSpec((hpt, tm, LANES), lambda j, i: (j, row_i(j, i), 0)))
            out_shapes.append(jax.ShapeDtypeStruct((ncols // LANES, mp, LANES), pdt))
            out_specs.append(pl.BlockSpec((hpt, ms, LANES), lambda j, i: (j, 0, 0)))
            out_shapes.append(jax.ShapeDtypeStruct((ncols // LANES, ms, LANES), sdt))
        else:
            out_specs.append(pl.BlockSpec((tm, tn), lambda j, i: (row_i(j, i), j)))
            out_shapes.append(jax.ShapeDtypeStruct((mp, ncols), pdt))
            out_specs.append(pl.BlockSpec((ms, tn), lambda j, i: (0, j)))
            out_shapes.append(jax.ShapeDtypeStruct((ms, ncols), sdt))
    kern = functools.partial(
        _proj_kernel, epilogues=tuple(epilogues),
        out_head_major=tuple(hm for _, _, hm in outs),
        n_aux=len(aux), n_prompt_tiles=npt, sections=sections, n_chunks=n_chunks)
    return pl.pallas_call(
        kern,
        grid=(nj, npt + 1),
        in_specs=in_specs,
        out_specs=out_specs,
        out_shape=out_shapes,
        scratch_shapes=[pltpu.VMEM((2, kdim, tn), BF16),
                        pltpu.VMEM((2, kc, tn), F32),
                        pltpu.SemaphoreType.DMA((2,))],
        compiler_params=_params(("arbitrary", "arbitrary")),
        name=name,
    )(xp, xs, w3, *aux)


def _ep_gelu(acc):
    return (jax.nn.gelu(acc, approximate=True),)


def _ep_silu(acc):
    return (acc * jax.nn.sigmoid(acc),)


def _ep_sigmoid(acc):
    return (jax.nn.sigmoid(acc),)


def _ep_identity(acc):
    return (acc,)


def _ep_forget(acc, lbl):
    m = jnp.max(lbl, axis=0, keepdims=True)
    e = jnp.exp(lbl - m)
    lb = e[0:1, :] / jnp.sum(e, axis=0, keepdims=True)
    sig = jax.nn.sigmoid(acc)
    log2_f = jnp.log(lb + (1.0 - lb) * sig) * LOG2_E
    k = (1.0 - lb) * (1.0 - sig)
    return log2_f, k


def _layernorm(gv, g, b):
    mu = jnp.mean(gv, axis=-1, keepdims=True)
    xc = gv - mu
    var = jnp.mean(xc * xc, axis=-1, keepdims=True)
    return xc * lax.rsqrt(var + EPS) * g + b


def _gate_kernel(u_ref, gv_ref, sza_ref, lng_ref, lnb_ref, ws_ref, bs_ref,
                 ya_ref, vr_ref, wm_ref, *, n_steps, n_groups):
    n = pl.program_id(1)

    @pl.when((pl.program_id(0) == 0) & (n == 0))
    def _():
        r = lax.broadcasted_iota(jnp.int32, (CHUNK_A, CHUNK_A), 0)
        c = lax.broadcasted_iota(jnp.int32, (CHUNK_A, CHUNK_A), 1)
        for g in range(n_groups):
            wm_ref[g] = jnp.where(c <= r, ws_ref[g], 0.0).astype(BF16)

    step_chunks = gv_ref.shape[0] // CHUNK_A
    for cc in range(step_chunks):
        rs = slice(cc * CHUNK_A, (cc + 1) * CHUNK_A)
        vn = _layernorm(gv_ref[rs, :].astype(F32), lng_ref[...], lnb_ref[...])

        if cc == step_chunks - 1:
            @pl.when(n == n_steps - 1)
            def _():
                vr_ref[...] = vn

        vnb = vn.astype(BF16)
        for g in range(n_groups):
            sl = slice(g * LANES, (g + 1) * LANES)
            mixed = jnp.dot(wm_ref[g], vnb[:, sl], preferred_element_type=F32) + bs_ref[:, sl]
            ya_ref[rs, sl] = (u_ref[rs, sl].astype(F32) * mixed
                              * sza_ref[rs, sl].astype(F32)).astype(ya_ref.dtype)


def _gate_prompt(uv, sza, ln_g, ln_b, w_s, bs_rows, batch, seq, e_a):
    rows = GATE_STEP_CHUNKS * CHUNK_A
    assert seq % rows == 0
    n_steps = seq // rows
    n_groups = w_s.shape[0]
    blk = lambda b, n: (b * n_steps + n, 0)
    kern = functools.partial(_gate_kernel, n_steps=n_steps, n_groups=n_groups)
    return pl.pallas_call(
        kern,
        grid=(batch, n_steps),
        in_specs=[
            pl.BlockSpec((rows, e_a), blk),
            pl.BlockSpec((rows, e_a), lambda b, n: (b * n_steps + n, 1)),
            pl.BlockSpec((rows, e_a), blk),
            pl.BlockSpec((1, e_a), lambda b, n: (0, 0)),
            pl.BlockSpec((1, e_a), lambda b, n: (0, 0)),
            pl.BlockSpec(w_s.shape, lambda b, n: (0, 0, 0)),
            pl.BlockSpec((CHUNK_A, e_a), lambda b, n: (0, 0)),
        ],
        out_specs=[
            pl.BlockSpec((rows, e_a), blk),
            pl.BlockSpec((None, None, CHUNK_A, e_a), lambda b, n: (0, b, 0, 0)),
        ],
        out_shape=[
            jax.ShapeDtypeStruct((batch * seq, e_a), BF16),
            jax.ShapeDtypeStruct((1, batch, CHUNK_A, e_a), F32),
        ],
        scratch_shapes=[pltpu.VMEM((n_groups, CHUNK_A, CHUNK_A), BF16)],
        compiler_params=_params(("arbitrary", "arbitrary")),
        name="spatial_gate_prompt",
    )(uv, uv, sza, ln_g, ln_b, w_s, bs_rows)


def _gate_sample_kernel(u_ref, gv_ref, sza_ref, lng_ref, lnb_ref, wd_ref, b0_ref,
                        ya_ref, vr_ref):
    vn = _layernorm(gv_ref[...].astype(F32), lng_ref[...], lnb_ref[...])
    vr_ref[...] = vn
    mixed = wd_ref[...] * vn + b0_ref[...]
    ya_ref[...] = (u_ref[...].astype(F32) * mixed * sza_ref[...].astype(F32)).astype(ya_ref.dtype)


def _gate_sample(uv, sza, ln_g, ln_b, wd_row, b0_row, e_a):
    m = sza.shape[0]
    full = lambda i: (0, 0)
    return pl.pallas_call(
        _gate_sample_kernel,
        grid=(1,),
        in_specs=[
            pl.BlockSpec((m, e_a), full),
            pl.BlockSpec((m, e_a), lambda i: (0, 1)),
            pl.BlockSpec((m, e_a), full),
            pl.BlockSpec((1, e_a), full),
            pl.BlockSpec((1, e_a), full),
            pl.BlockSpec((1, e_a), full),
            pl.BlockSpec((1, e_a), full),
        ],
        out_specs=[pl.BlockSpec((m, e_a), full), pl.BlockSpec((m, e_a), full)],
        out_shape=[jax.ShapeDtypeStruct((m, e_a), BF16),
                   jax.ShapeDtypeStruct((m, e_a), F32)],
        compiler_params=_params(("arbitrary",)),
        name="spatial_gate_sample",
    )(uv, uv, sza, ln_g, ln_b, wd_row, b0_row)


def _dot_nt(a, b):
    return lax.dot_general(a, b, (((1,), (1,)), ((), ())), preferred_element_type=F32)


def _dot_tn(a, b):
    return lax.dot_general(a, b, (((0,), (0,)), ((), ())), preferred_element_type=F32)


def _hgrn_kernel(q_ref, lf_ref, k_ref, i_ref, szb_ref, gon_ref, yb_ref, st_ref,
                 s_scr, o_scr, *att_scrs, n_steps, n_heads):
    n = pl.program_id(1)
    c, hd, sb = CHUNK_B, HEAD_DIM, SUBLANES
    nv = c // sb

    @pl.when((pl.program_id(0) == 0) & (n == 0))
    def _():
        for att_scr in att_scrs:
            att_scr[...] = jnp.zeros_like(att_scr)

    @pl.when(n == 0)
    def _():
        s_scr[...] = jnp.zeros_like(s_scr)

    r8 = lax.broadcasted_iota(jnp.int32, (1, sb, hd), 1)
    t_idx = lax.broadcasted_iota(jnp.int32, (c, c), 0)
    s_idx = lax.broadcasted_iota(jnp.int32, (c, c), 1)
    fine = [h for h in (1, 2, 4) if 2 * h <= sb]
    coarse = [1 << l for l in range(sb.bit_length() - 1, (c // 2).bit_length())]
    second8 = {h: (r8 & (2 * h - 1)) >= h for h in fine}
    fine_mask = {}
    for h in fine:
        same = (t_idx >> h.bit_length()) == (s_idx >> h.bit_length())
        if h == 1:
            fine_mask[h] = same & (s_idx <= t_idx)
        else:
            fine_mask[h] = same & ((t_idx & (2 * h - 1)) >= h) & ((s_idx & (2 * h - 1)) < h)

    def tiles(x):
        return x.reshape(nv, sb, hd)

    def fine_boundary(p3, h):
        acc = None
        for blk in range(sb // (2 * h)):
            r = blk * 2 * h + h - 1
            row = jnp.broadcast_to(p3[:, r:r + 1, :], (nv, sb, hd))
            acc = row if acc is None else jnp.where(r8 >= blk * 2 * h, row, acc)
        return acc

    def one_head(h, rows, att_ref):
        lf = lf_ref[h, rows, :]
        q = q_ref[h, rows, :]
        k = k_ref[h, rows, :].astype(F32)
        ib = i_ref[h, rows, :].astype(BF16)
        q3, k3, lf3 = tiles(q), tiles(k), tiles(lf)

        odd = second8[1]
        qf = jnp.where(odd, q3 * jnp.exp2(lf3), q3).reshape(c, hd).astype(BF16)
        kf = jnp.where(odd, k3 * jnp.exp2(-lf3), k3).reshape(c, hd).astype(BF16)
        att_f = jnp.where(fine_mask[1], _dot_nt(qf, kf), 0.0)
        p3 = lf3 + jnp.where(odd, tiles(pltpu.roll(lf, 1, 0)), 0.0)
        yield
        for hh in fine[1:]:
            bnd = fine_boundary(p3, hh)
            sec = second8[hh]
            g = jnp.where(sec, p3, bnd - p3)
            x = (jnp.where(sec, q3, k3) * jnp.exp2(g)).reshape(c, hd).astype(BF16)
            att_f = jnp.where(fine_mask[hh], _dot_nt(x, x), att_f)
            p3 = p3 + jnp.where(sec, bnd, 0.0)
            yield
        for v in range(nv):
            blk = slice(v * sb, (v + 1) * sb)
            att_ref[blk, blk] = att_f[blk, blk]

        p = p3.reshape(c, hd)
        for hh in coarse:
            xs, ps = [], []
            for blk in range(c // (2 * hh)):
                lo, mid, hi = blk * 2 * hh, blk * 2 * hh + hh, (blk + 1) * 2 * hh
                bnd = jnp.broadcast_to(p[mid - 1:mid, :], (hh, hd))
                p_lo, p_hi = p[lo:mid], p[mid:hi]
                xs.append(k[lo:mid] * jnp.exp2(bnd - p_lo))
                xs.append(q[mid:hi] * jnp.exp2(p_hi))
                ps.append(p_lo)
                ps.append(p_hi + bnd)
            x = jnp.concatenate(xs, axis=0).astype(BF16)
            a = _dot_nt(x, x)
            for blk in range(c // (2 * hh)):
                lo, mid, hi = blk * 2 * hh, blk * 2 * hh + hh, (blk + 1) * 2 * hh
                att_ref[mid:hi, lo:mid] = a[mid:hi, lo:mid]
            p = jnp.concatenate(ps, axis=0)
            yield

        b_last = p[c - 1:c, :]
        qt = (q * jnp.exp2(p)).astype(BF16)
        kt = (k * jnp.exp2(jnp.broadcast_to(b_last, (c, hd)) - p)).astype(BF16)
        dec = jnp.exp2(b_last)
        s_old = s_scr[h]
        lhs = jnp.concatenate([att_ref[...].astype(BF16), qt], axis=1)
        rhs = jnp.concatenate([ib, s_old.astype(BF16)], axis=0)
        o_scr[h] = jnp.dot(lhs, rhs, preferred_element_type=F32)
        dec_col = jnp.transpose(jnp.broadcast_to(dec, (hd, hd)))
        s_scr[h] = dec_col * s_old + _dot_tn(kt, ib)

    def chunk_body(cc, chunk_carry):
        rows = pl.ds(pl.multiple_of(cc * c, c), c)

        def head_group(grp, carry):
            heads = [one_head(grp * HEAD_UNROLL + u, rows, att_scrs[u])
                     for u in range(HEAD_UNROLL)]
            for _ in itertools.zip_longest(*heads):
                pass
            return carry

        lax.fori_loop(0, n_heads // HEAD_UNROLL, head_group, 0)

        for h in range(n_heads):
            sl = slice(h * hd, (h + 1) * hd)
            o = o_scr[h]
            ms = jnp.mean(o * o, axis=-1, keepdims=True)
            yb_ref[rows, sl] = (o * lax.rsqrt(ms + EPS) * gon_ref[:, sl]
                                * szb_ref[rows, sl].astype(F32)).astype(yb_ref.dtype)
        return chunk_carry

    lax.fori_loop(0, q_ref.shape[1] // c, chunk_body, 0)

    @pl.when(n == n_steps - 1)
    def _():
        st_ref[...] = s_scr[...]


def _hgrn_prompt(qi, lf, k, zz, g_onorm, batch, seq):
    n_heads = lf.shape[0]
    e_b = n_heads * HEAD_DIM
    rows = HGRN_STEP_CHUNKS * CHUNK_B
    assert seq % rows == 0
    n_steps = seq // rows
    hm = pl.BlockSpec((n_heads, rows, HEAD_DIM), lambda b, n: (0, b * n_steps + n, 0))
    hm_i = pl.BlockSpec((n_heads, rows, HEAD_DIM), lambda b, n: (1, b * n_steps + n, 0))
    rm = pl.BlockSpec((rows, e_b), lambda b, n: (b * n_steps + n, 0))
    rm_zb = pl.BlockSpec((rows, e_b), lambda b, n: (b * n_steps + n, 1))
    kern = functools.partial(_hgrn_kernel, n_steps=n_steps, n_heads=n_heads)
    q, iv, szb = qi, qi, zz
    return pl.pallas_call(
        kern,
        grid=(batch, n_steps),
        in_specs=[hm, hm, hm, hm_i, rm_zb, pl.BlockSpec((1, e_b), lambda b, n: (0, 0))],
        out_specs=[
            rm,
            pl.BlockSpec((None, None, n_heads, HEAD_DIM, HEAD_DIM),
                         lambda b, n: (0, b, 0, 0, 0)),
        ],
        out_shape=[
            jax.ShapeDtypeStruct((batch * seq, e_b), BF16),
            jax.ShapeDtypeStruct((1, batch, n_heads, HEAD_DIM, HEAD_DIM), F32),
        ],
        scratch_shapes=[
            pltpu.VMEM((n_heads, HEAD_DIM, HEAD_DIM), F32),
            pltpu.VMEM((n_heads, CHUNK_B, HEAD_DIM), F32),
        ] + [pltpu.VMEM((CHUNK_B, CHUNK_B), F32) for _ in range(HEAD_UNROLL)],
        compiler_params=_params(("arbitrary", "arbitrary")),
        name="hgrn_prompt",
    )(q, lf, k, iv, szb, g_onorm)


def _hgrn_sample_kernel(q_ref, lf_ref, k_ref, i_ref, st_ref, szb_ref, gon_ref,
                        sto_ref, yb_ref, ft_scr, kt_scr, o_scr, *, n_heads, nb):
    s = pl.program_id(0)
    hd = HEAD_DIM

    @pl.when(s == 0)
    def _():
        def tb(h, carry):
            ft_scr[h] = jnp.transpose(jnp.exp2(lf_ref[h]))
            kt_scr[h] = jnp.transpose(k_ref[h])
            return carry
        lax.fori_loop(0, n_heads, tb, 0)

    n_seq = lf_ref.shape[1]
    shift = jnp.where(s == 0, 0, n_seq - s * nb)

    def one_head(h):
        fr = pltpu.roll(ft_scr[h], shift, 1)
        kr = pltpu.roll(kt_scr[h], shift, 1)
        for j in range(nb):
            fb = jnp.broadcast_to(fr[:, j:j + 1], (hd, hd))
            kb = jnp.broadcast_to(kr[:, j:j + 1], (hd, hd))
            irow = i_ref[h, pl.ds(s * nb + j, 1), :]
            s_new = fb * st_ref[j, h] + kb * irow
            sto_ref[j, h] = s_new
            qrow = jnp.broadcast_to(q_ref[h, pl.ds(s * nb + j, 1), :], (SUBLANES, hd))
            o = jnp.dot(qrow.astype(BF16), s_new.astype(BF16), preferred_element_type=F32)
            o_scr[h, pl.ds(j, 1), :] = o[0:1, :]
            yield

    def head_group(grp, carry):
        heads = [one_head(grp * SAMPLE_HEAD_UNROLL + u) for u in range(SAMPLE_HEAD_UNROLL)]
        for _ in itertools.zip_longest(*heads):
            pass
        return carry

    lax.fori_loop(0, n_heads // SAMPLE_HEAD_UNROLL, head_group, 0)

    for h in range(n_heads):
        sl = slice(h * hd, (h + 1) * hd)
        o = o_scr[h]
        ms = jnp.mean(o * o, axis=-1, keepdims=True)
        yb_ref[:, sl] = (o * lax.rsqrt(ms + EPS) * gon_ref[:, sl] * szb_ref[:, sl])


def _hgrn_sample(qi, lf, k, state, zz, g_onorm):
    n_heads, n_seq, hd = lf.shape
    nb = SAMPLE_SEQ_BLOCK
    e_b = n_heads * hd
    full3 = pl.BlockSpec((n_heads, n_seq, hd), lambda s: (0, 0, 0))
    full3_i = pl.BlockSpec((n_heads, n_seq, hd), lambda s: (1, 0, 0))
    st_spec = pl.BlockSpec((None, nb, n_heads, hd, hd), lambda s: (0, s, 0, 0, 0))
    kern = functools.partial(_hgrn_sample_kernel, n_heads=n_heads, nb=nb)
    q, iv, szb = qi, qi, zz
    return pl.pallas_call(
        kern,
        grid=(n_seq // nb,),
        in_specs=[full3, full3, full3, full3_i, st_spec,
                  pl.BlockSpec((nb, e_b), lambda s: (s, 1)),
                  pl.BlockSpec((1, e_b), lambda s: (0, 0))],
        out_specs=[st_spec, pl.BlockSpec((nb, e_b), lambda s: (s, 0))],
        out_shape=[jax.ShapeDtypeStruct(state.shape, F32),
                   jax.ShapeDtypeStruct((n_seq, e_b), F32)],
        scratch_shapes=[
            pltpu.VMEM((n_heads, hd, n_seq), F32),
            pltpu.VMEM((n_heads, hd, n_seq), F32),
            pltpu.VMEM((n_heads, nb, hd), F32),
        ],
        compiler_params=_params(("arbitrary",)),
        name="hgrn_sample",
    )(q, lf, k, iv, state, szb, g_onorm)


def _merge_kernel(yap_ref, ybp_ref, gap_ref, gbp_ref, yas_ref, ybs_ref, gas_ref, gbs_ref,
                  wpa_hbm, wpb_hbm, mp_ref, ms_ref, wa_bf, wb_bf, stage_ref, sem,
                  *, n_prompt_tiles, n_col_tiles, n_chunks):
    j = pl.program_id(0)
    i = pl.program_id(1)
    cur = j % 2
    _, _, kc, tn = stage_ref.shape
    weights = ((wpa_hbm, wa_bf), (wpb_hbm, wb_bf))

    def chunk_copy(widx, jj, c, slot):
        return pltpu.make_async_copy(
            weights[widx][0].at[0, pl.ds(c * kc, kc), pl.ds(jj * tn, tn)],
            stage_ref.at[slot, widx], sem.at[2 * slot + widx])

    @pl.when((j == 0) & (i == 0))
    def _():
        for widx in range(2):
            chunk_copy(widx, 0, 0, 0).start()
        for c in range(n_chunks):
            for widx in range(2):
                if c + 1 < n_chunks:
                    chunk_copy(widx, 0, c + 1, (c + 1) % 2).start()
                chunk_copy(widx, 0, c, c % 2).wait()
                weights[widx][1][0, c * kc:(c + 1) * kc, :] = (
                    stage_ref[c % 2, widx].astype(BF16))

    has_next = j + 1 < n_col_tiles

    @pl.when(has_next & (i >= 1))
    def _():
        for widx in range(2):
            chunk_copy(widx, j + 1, i - 1, (i - 1) % 2).wait()

    @pl.when(has_next & (i < n_chunks))
    def _():
        for widx in range(2):
            chunk_copy(widx, j + 1, i, i % 2).start()

    def cast_previous_chunk():
        prev = (i + n_chunks - 1) % n_chunks
        for widx in range(2):
            weights[widx][1][1 - cur, pl.ds(prev * kc, kc), :] = (
                stage_ref[(i + 1) % 2, widx].astype(BF16))

    def run(ya, yb, ga, gb, out):
        cast_previous_chunk()
        rows = ya.shape[0]
        sub = min(rows, SUB_ROWS)
        for m in range(rows // sub):
            rs = slice(m * sub, (m + 1) * sub)
            a = jnp.dot(ya[rs, :].astype(BF16), wa_bf[cur], preferred_element_type=F32)
            b = jnp.dot(yb[rs, :].astype(BF16), wb_bf[cur], preferred_element_type=F32)
            out[rs, :] = (ga[rs, :].astype(F32) * a + gb[rs, :].astype(F32) * b).astype(out.dtype)

    @pl.when(i != _sample_step(n_prompt_tiles))
    def _():
        run(yap_ref, ybp_ref, gap_ref, gbp_ref, mp_ref)

    @pl.when(i == _sample_step(n_prompt_tiles))
    def _():
        run(yas_ref, ybs_ref, gas_ref, gbs_ref, ms_ref)


def _merge(ya_p, yb_p, g_p, ya_s, yb_s, g_s, w_pa, w_pb, gate_col0):
    mp, kdim = ya_p.shape
    ms = ya_s.shape[0]
    d = w_pa.shape[-1]
    tm, tn = ROW_TILE, PROJ_COL_TILE
    npt = mp // tm
    nj = d // tn
    assert gate_col0 % tn == 0 and d % tn == 0
    g0 = gate_col0 // tn
    n_chunks = npt
    assert kdim % n_chunks == 0 and n_chunks >= 2
    kc = kdim // n_chunks

    def row_i(j, i):
        return _prompt_tile(j, i, npt)

    yp = pl.BlockSpec((tm, kdim), lambda j, i: (row_i(j, i), 0))
    ys = pl.BlockSpec((ms, kdim), lambda j, i: (0, 0))
    w = pl.BlockSpec(memory_space=pl.ANY)
    kern = functools.partial(_merge_kernel, n_prompt_tiles=npt, n_col_tiles=nj,
                             n_chunks=n_chunks)
    return pl.pallas_call(
        kern,
        grid=(nj, npt + 1),
        in_specs=[
            yp, yp,
            pl.BlockSpec((tm, tn), lambda j, i: (row_i(j, i), g0 + j)),
            pl.BlockSpec((tm, tn), lambda j, i: (row_i(j, i), g0 + nj + j)),
            ys, ys,
            pl.BlockSpec((ms, tn), lambda j, i: (0, g0 + j)),
            pl.BlockSpec((ms, tn), lambda j, i: (0, g0 + nj + j)),
            w, w,
        ],
        out_specs=[pl.BlockSpec((tm, tn), lambda j, i: (row_i(j, i), j)),
                   pl.BlockSpec((ms, tn), lambda j, i: (0, j))],
        out_shape=[jax.ShapeDtypeStruct((mp, d), BF16),
                   jax.ShapeDtypeStruct((ms, d), BF16)],
        scratch_shapes=[pltpu.VMEM((2, kdim, tn), BF16), pltpu.VMEM((2, kdim, tn), BF16),
                        pltpu.VMEM((2, 2, kc, tn), F32),
                        pltpu.SemaphoreType.DMA((4,))],
        compiler_params=_params(("arbitrary", "arbitrary")),
        name="gated_merge",
    )(ya_p, yb_p, g_p, g_p, ya_s, yb_s, g_s, g_s, w_pa, w_pb)


def _out_post_kernel(mp_ref, xp_ref, ms_ref, xs_ref, w_hbm, g_ref, op_ref, os_ref, z_scr,
                     w_ref, stage_ref, sem, *, n_prompt_tiles):
    s = pl.program_id(0)
    n_slots, kc, _ = stage_ref.shape
    n_w_chunks = w_ref.shape[0] // kc

    def chunk_copy(c, slot):
        return pltpu.make_async_copy(w_hbm.at[0, pl.ds(c * kc, kc), :],
                                     stage_ref.at[slot], sem.at[slot])

    def matmul(m_ref):
        z_scr[...] = jnp.dot(m_ref[...], w_ref[...], preferred_element_type=F32)

    def finish(x_ref, o_ref):
        z = z_scr[...]
        ms = jnp.mean(z * z, axis=-1, keepdims=True)
        o_ref[...] = x_ref[...] + z * lax.rsqrt(ms + EPS) * g_ref[...]

    @pl.when(s == 0)
    def _():
        for c in range(n_slots - 1):
            chunk_copy(c, c % n_slots).start()
        for c in range(n_w_chunks):
            ahead = c + n_slots - 1
            if ahead < n_w_chunks:
                chunk_copy(ahead, ahead % n_slots).start()
            chunk_copy(c, c % n_slots).wait()
            w_ref[c * kc:(c + 1) * kc, :] = stage_ref[c % n_slots].astype(BF16)
        matmul(mp_ref)

    @pl.when((s >= 1) & (s < n_prompt_tiles))
    def _():
        finish(xp_ref, op_ref)
        matmul(mp_ref)

    @pl.when(s == n_prompt_tiles)
    def _():
        finish(xp_ref, op_ref)
        matmul(ms_ref)

    @pl.when(s == n_prompt_tiles + 1)
    def _():
        finish(xs_ref, os_ref)


def _out_post(m_p, x_p, m_s, x_s, w3, g):
    mp, d = x_p.shape
    ms = x_s.shape[0]
    kdim = w3.shape[1]
    tm = OUT_ROWS
    assert ms == tm
    assert kdim % OUT_W_CHUNK == 0 and kdim // OUT_W_CHUNK >= OUT_W_SLOTS
    npt = mp // tm
    mm_row = pl.BlockSpec((tm, d), lambda s: (jnp.minimum(s, npt - 1), 0))
    fin_row = pl.BlockSpec((tm, d), lambda s: (jnp.clip(s - 1, 0, npt - 1), 0))
    full_s = pl.BlockSpec((ms, d), lambda s: (0, 0))
    kern = functools.partial(_out_post_kernel, n_prompt_tiles=npt)
    return pl.pallas_call(
        kern,
        grid=(npt + 2,),
        in_specs=[mm_row, fin_row, full_s, full_s,
                  pl.BlockSpec(memory_space=pl.ANY),
                  pl.BlockSpec((1, d), lambda s: (0, 0))],
        out_specs=[fin_row, full_s],
        out_shape=[jax.ShapeDtypeStruct((mp, d), F32), jax.ShapeDtypeStruct((ms, d), F32)],
        scratch_shapes=[pltpu.VMEM((tm, d), F32),
                        pltpu.VMEM((kdim, d), BF16),
                        pltpu.VMEM((OUT_W_SLOTS, OUT_W_CHUNK, d), F32),
                        pltpu.SemaphoreType.DMA((OUT_W_SLOTS,))],
        compiler_params=_params(("arbitrary",)),
        name="out_proj_post_norm",
    )(m_p, x_p, m_s, x_s, w3, g)


def kernel(x_prompt, x_sample, state_hgrn, lb_logits, g_pre, w_in, ln_g, ln_b, w_s, b_s,
           g_onorm, w_pa, w_pb, w_o, g_post):
    batch, seq, d = x_prompt.shape
    n_seq, dec_seq, _ = x_sample.shape
    depth = w_in.shape[0]
    assert depth == 1 and dec_seq == 1
    assert seq % CHUNK_A == 0 and seq % CHUNK_B == 0
    e = w_pa.shape[1]
    n_groups = w_s.shape[1]
    assert n_groups * CHUNK_A == e

    xp = x_prompt.reshape(batch * seq, d)
    xs = x_sample.reshape(n_seq, d)
    xn_p = _rmsnorm(xp, g_pre, NORM_ROWS)
    xn_s = _rmsnorm(xs, g_pre, n_seq)

    col = lambda idx: idx * e
    uvg_p, uvg_s = _proj(xn_p, xn_s, w_in, [(col(0), 2 * e), (col(7), 2 * d)],
                         [_ep_gelu, _ep_sigmoid],
                         [(BF16, BF16, False)], name="proj_uv_gates")
    zz_p, zz_s = _proj(xn_p, xn_s, w_in, [(col(2), e), (col(6), e)], [_ep_silu, _ep_silu],
                       [(BF16, F32, False)], name="proj_za_zb")
    qi_p, qi_s = _proj(xn_p, xn_s, w_in, [(col(3), e), (col(5), e)],
                       [_ep_silu, _ep_identity],
                       [(F32, F32, True)], name="proj_q_i")
    lf_p, lf_s, k_p, k_s = _proj(xn_p, xn_s, w_in, [(col(4), e)], [_ep_forget],
                                 [(F32, F32, True), (BF16, F32, True)],
                                 aux=(lb_logits,), name="proj_f")

    bs_rows = jnp.repeat(b_s[0].T, CHUNK_A, axis=1)
    ya_p, vrows_p = _gate_prompt(uvg_p, zz_p, ln_g, ln_b, w_s[0], bs_rows, batch, seq, e)
    wd_row = jnp.repeat(w_s[0, :, 0, 0], CHUNK_A)[None, :]
    b0_row = jnp.repeat(b_s[0, :, 0], CHUNK_A)[None, :]
    ya_s, vrows_s = _gate_sample(uvg_s, zz_s, ln_g, ln_b, wd_row, b0_row, e)

    yb_p, state_p = _hgrn_prompt(qi_p, lf_p, k_p, zz_p, g_onorm, batch, seq)
    state_s, yb_s = _hgrn_sample(qi_s, lf_s, k_s, state_hgrn, zz_s, g_onorm)

    m_p, m_s = _merge(ya_p, yb_p, uvg_p, ya_s, yb_s, uvg_s, w_pa, w_pb, 2 * e)
    y_p, y_s = _out_post(m_p, xp, m_s, xs, w_o, g_post)

    return (y_p.reshape(batch, seq, d), y_s.reshape(n_seq, 1, d),
            state_p, state_s,
            vrows_p, vrows_s.reshape(1, n_seq, 1, e))
```

```python
import functools
import itertools

import jax
import jax.numpy as jnp
from jax import lax
from jax.experimental import pallas as pl
from jax.experimental.pallas import tpu as pltpu

F32 = jnp.float32
BF16 = jnp.bfloat16

EPS = 1e-6
LOG2_E = 1.4426950408889634
LANES = 128
SUBLANES = 8
CHUNK_A = 128
GATE_STEP_CHUNKS = 8
HEAD_DIM = 128
CHUNK_B = 128
HGRN_STEP_CHUNKS = 4
HEAD_UNROLL = 8
VMEM_LIMIT_BYTES = 56 * 1024 * 1024
ROW_TILE = 1024
PROJ_COL_TILE = 1024
SUB_ROWS = 256
NORM_ROWS = 512
OUT_ROWS = 128
OUT_W_CHUNK = 32
OUT_W_SLOTS = 6
SAMPLE_SEQ_BLOCK = 8
SAMPLE_HEAD_UNROLL = 8


def _params(sem):
    return pltpu.CompilerParams(dimension_semantics=sem,
                                vmem_limit_bytes=VMEM_LIMIT_BYTES)


def _rmsnorm_kernel(x_ref, g_ref, o_ref):
    x = x_ref[...]
    ms = jnp.mean(x * x, axis=-1, keepdims=True)
    o_ref[...] = (x * lax.rsqrt(ms + EPS) * g_ref[...]).astype(o_ref.dtype)


def _rmsnorm(x, g, rows):
    m, d = x.shape
    return pl.pallas_call(
        _rmsnorm_kernel,
        grid=(m // rows,),
        in_specs=[pl.BlockSpec((rows, d), lambda i: (i, 0)),
                  pl.BlockSpec((1, d), lambda i: (0, 0))],
        out_specs=pl.BlockSpec((rows, d), lambda i: (i, 0)),
        out_shape=jax.ShapeDtypeStruct((m, d), BF16),
        compiler_params=_params(("arbitrary",)),
        name="pre_rmsnorm",
    )(x, g)


def _sample_step(n_prompt_tiles):
    return n_prompt_tiles // 2


def _prompt_tile(j, i, n_prompt_tiles):
    fwd = i - (i > _sample_step(n_prompt_tiles)).astype(jnp.int32)
    return jnp.where(j % 2 == 0, fwd, n_prompt_tiles - 1 - fwd)


def _proj_kernel(*refs, epilogues, out_head_major, n_aux, n_prompt_tiles, sections,
                 n_chunks):
    n_out = len(out_head_major)
    xp_ref, xs_ref, w_hbm = refs[:3]
    aux_refs = refs[3:3 + n_aux]
    out_refs = refs[3 + n_aux:3 + n_aux + 2 * n_out]
    wbf_ref, stage_ref, sem = refs[-3:]
    j = pl.program_id(0)
    i = pl.program_id(1)
    n_col_tiles = sum(count for _, count in sections)
    cur = j % 2
    _, kc, tn = stage_ref.shape

    def weight_col_tile(jj):
        tile, start = None, 0
        for first, count in sections:
            t = first + (jj - start)
            tile = t if tile is None else jnp.where(jj >= start, t, tile)
            start += count
        return tile

    def chunk_copy(jj, c, slot):
        return pltpu.make_async_copy(
            w_hbm.at[0, pl.ds(c * kc, kc), pl.ds(weight_col_tile(jj) * tn, tn)],
            stage_ref.at[slot], sem.at[slot])

    @pl.when((j == 0) & (i == 0))
    def _():
        chunk_copy(0, 0, 0).start()
        for c in range(n_chunks):
            if c + 1 < n_chunks:
                chunk_copy(0, c + 1, (c + 1) % 2).start()
            chunk_copy(0, c, c % 2).wait()
            wbf_ref[0, c * kc:(c + 1) * kc, :] = stage_ref[c % 2].astype(BF16)

    has_next = j + 1 < n_col_tiles

    @pl.when(has_next & (i >= 1))
    def _():
        chunk_copy(j + 1, i - 1, (i - 1) % 2).wait()

    @pl.when(has_next & (i < n_chunks))
    def _():
        chunk_copy(j + 1, i, i % 2).start()

    def cast_previous_chunk():
        prev = (i + n_chunks - 1) % n_chunks
        wbf_ref[1 - cur, pl.ds(prev * kc, kc), :] = stage_ref[(i + 1) % 2].astype(BF16)

    def run(x_ref, outs, epilogue):
        cast_previous_chunk()
        rows = x_ref.shape[0]
        sub = min(rows, SUB_ROWS)
        for m in range(rows // sub):
            rs = slice(m * sub, (m + 1) * sub)
            acc = jnp.dot(x_ref[rs, :].astype(BF16), wbf_ref[cur],
                          preferred_element_type=F32)
            res = epilogue(acc, *[a[...] for a in aux_refs])
            for r, o_ref, hm in zip(res, outs, out_head_major):
                if hm:
                    for hh in range(o_ref.shape[0]):
                        o_ref[hh, rs, :] = r[:, hh * LANES:(hh + 1) * LANES].astype(o_ref.dtype)
                else:
                    o_ref[rs, :] = r.astype(o_ref.dtype)

    is_sample = i == _sample_step(n_prompt_tiles)
    start = 0
    for (_, count), epilogue in zip(sections, epilogues):
        in_section = (j >= start) & (j < start + count)
        start += count

        @pl.when(in_section & jnp.logical_not(is_sample))
        def _(epilogue=epilogue):
            run(xp_ref, out_refs[0::2], epilogue)

        @pl.when(in_section & is_sample)
        def _(epilogue=epilogue):
            run(xs_ref, out_refs[1::2], epilogue)


def _proj(xp, xs, w3, col_sections, epilogues, outs, aux=(), tn=None, name="proj"):
    mp, kdim = xp.shape
    ms = xs.shape[0]
    tm = ROW_TILE
    tn = PROJ_COL_TILE if tn is None else tn
    npt = mp // tm
    assert all(c0 % tn == 0 and w % tn == 0 for c0, w in col_sections)
    sections = tuple((c0 // tn, w // tn) for c0, w in col_sections)
    ncols = sum(w for _, w in col_sections)
    nj = ncols // tn
    hpt = tn // LANES
    n_chunks = npt
    assert kdim % n_chunks == 0 and n_chunks >= 2
    kc = kdim // n_chunks

    def row_i(j, i):
        return _prompt_tile(j, i, npt)

    in_specs = [
        pl.BlockSpec((tm, kdim), lambda j, i: (row_i(j, i), 0)),
        pl.BlockSpec((ms, kdim), lambda j, i: (0, 0)),
        pl.BlockSpec(memory_space=pl.ANY),
    ]
    for a in aux:
        in_specs.append(pl.BlockSpec((a.shape[0], tn), lambda j, i: (0, j)))
    out_specs, out_shapes = [], []
    for pdt, sdt, hm in outs:
        if hm:
            out_specs.append(pl.BlockSpec((hpt, tm, LANES), lambda j, i: (j, row_i(j, i), 0)))
            out_shapes.append(jax.ShapeDtypeStruct((ncols // LANES, mp, LANES), pdt))
            out_specs.append(pl.BlockSpec((hpt, ms, LANES), lambda j, i: (j, 0, 0)))
            out_shapes.append(jax.ShapeDtypeStruct((ncols // LANES, ms, LANES), sdt))
        else:
            out_specs.append(pl.BlockSpec((tm, tn), lambda j, i: (row_i(j, i), j)))
            out_shapes.append(jax.ShapeDtypeStruct((mp, ncols), pdt))
            out_specs.append(pl.BlockSpec((ms, tn), lambda j, i: (0, j)))
            out_shapes.append(jax.ShapeDtypeStruct((ms, ncols), sdt))
    kern = functools.partial(
        _proj_kernel, epilogues=tuple(epilogues),
        out_head_major=tuple(hm for _, _, hm in outs),
        n_aux=len(aux), n_prompt_tiles=npt, sections=sections, n_chunks=n_chunks)
    return pl.pallas_call(
        kern,
        grid=(nj, npt + 1),
        in_specs=in_specs,
        out_specs=out_specs,
        out_shape=out_shapes,
        scratch_shapes=[pltpu.VMEM((2, kdim, tn), BF16),
                        pltpu.VMEM((2, kc, tn), F32),
                        pltpu.SemaphoreType.DMA((2,))],
        compiler_params=_params(("arbitrary", "arbitrary")),
        name=name,
    )(xp, xs, w3, *aux)


def _ep_gelu(acc):
    return (jax.nn.gelu(acc, approximate=True),)


def _ep_silu(acc):
    return (acc * jax.nn.sigmoid(acc),)


def _ep_sigmoid(acc):
    return (jax.nn.sigmoid(acc),)


def _ep_identity(acc):
    return (acc,)


def _ep_forget(acc, lbl):
    m = jnp.max(lbl, axis=0, keepdims=True)
    e = jnp.exp(lbl - m)
    lb = e[0:1, :] / jnp.sum(e, axis=0, keepdims=True)
    sig = jax.nn.sigmoid(acc)
    log2_f = jnp.log(lb + (1.0 - lb) * sig) * LOG2_E
    k = (1.0 - lb) * (1.0 - sig)
    return log2_f, k


def _layernorm(gv, g, b):
    mu = jnp.mean(gv, axis=-1, keepdims=True)
    xc = gv - mu
    var = jnp.mean(xc * xc, axis=-1, keepdims=True)
    return xc * lax.rsqrt(var + EPS) * g + b


def _gate_kernel(u_ref, gv_ref, sza_ref, lng_ref, lnb_ref, ws_ref, bs_ref,
                 ya_ref, vr_ref, wm_ref, *, n_steps, n_groups):
    n = pl.program_id(1)

    @pl.when((pl.program_id(0) == 0) & (n == 0))
    def _():
        r = lax.broadcasted_iota(jnp.int32, (CHUNK_A, CHUNK_A), 0)
        c = lax.broadcasted_iota(jnp.int32, (CHUNK_A, CHUNK_A), 1)
        for g in range(n_groups):
            wm_ref[g] = jnp.where(c <= r, ws_ref[g], 0.0).astype(BF16)

    step_chunks = gv_ref.shape[0] // CHUNK_A
    for cc in range(step_chunks):
        rs = slice(cc * CHUNK_A, (cc + 1) * CHUNK_A)
        vn = _layernorm(gv_ref[rs, :].astype(F32), lng_ref[...], lnb_ref[...])

        if cc == step_chunks - 1:
            @pl.when(n == n_steps - 1)
            def _():
                vr_ref[...] = vn

        vnb = vn.astype(BF16)
        for g in range(n_groups):
            sl = slice(g * LANES, (g + 1) * LANES)
            mixed = jnp.dot(wm_ref[g], vnb[:, sl], preferred_element_type=F32) + bs_ref[:, sl]
            ya_ref[rs, sl] = (u_ref[rs, sl].astype(F32) * mixed
                              * sza_ref[rs, sl].astype(F32)).astype(ya_ref.dtype)


def _gate_prompt(uv, sza, ln_g, ln_b, w_s, bs_rows, batch, seq, e_a):
    rows = GATE_STEP_CHUNKS * CHUNK_A
    assert seq % rows == 0
    n_steps = seq // rows
    n_groups = w_s.shape[0]
    blk = lambda b, n: (b * n_steps + n, 0)
    kern = functools.partial(_gate_kernel, n_steps=n_steps, n_groups=n_groups)
    return pl.pallas_call(
        kern,
        grid=(batch, n_steps),
        in_specs=[
            pl.BlockSpec((rows, e_a), blk),
            pl.BlockSpec((rows, e_a), lambda b, n: (b * n_steps + n, 1)),
            pl.BlockSpec((rows, e_a), blk),
            pl.BlockSpec((1, e_a), lambda b, n: (0, 0)),
            pl.BlockSpec((1, e_a), lambda b, n: (0, 0)),
            pl.BlockSpec(w_s.shape, lambda b, n: (0, 0, 0)),
            pl.BlockSpec((CHUNK_A, e_a), lambda b, n: (0, 0)),
        ],
        out_specs=[
            pl.BlockSpec((rows, e_a), blk),
            pl.BlockSpec((None, None, CHUNK_A, e_a), lambda b, n: (0, b, 0, 0)),
        ],
        out_shape=[
            jax.ShapeDtypeStruct((batch * seq, e_a), BF16),
            jax.ShapeDtypeStruct((1, batch, CHUNK_A, e_a), F32),
        ],
        scratch_shapes=[pltpu.VMEM((n_groups, CHUNK_A, CHUNK_A), BF16)],
        compiler_params=_params(("arbitrary", "arbitrary")),
        name="spatial_gate_prompt",
    )(uv, uv, sza, ln_g, ln_b, w_s, bs_rows)


def _gate_sample_kernel(u_ref, gv_ref, sza_ref, lng_ref, lnb_ref, wd_ref, b0_ref,
                        ya_ref, vr_ref):
    vn = _layernorm(gv_ref[...].astype(F32), lng_ref[...], lnb_ref[...])
    vr_ref[...] = vn
    mixed = wd_ref[...] * vn + b0_ref[...]
    ya_ref[...] = (u_ref[...].astype(F32) * mixed * sza_ref[...].astype(F32)).astype(ya_ref.dtype)


def _gate_sample(uv, sza, ln_g, ln_b, wd_row, b0_row, e_a):
    m = sza.shape[0]
    full = lambda i: (0, 0)
    return pl.pallas_call(
        _gate_sample_kernel,
        grid=(1,),
        in_specs=[
            pl.BlockSpec((m, e_a), full),
            pl.BlockSpec((m, e_a), lambda i: (0, 1)),
            pl.BlockSpec((m, e_a), full),
            pl.BlockSpec((1, e_a), full),
            pl.BlockSpec((1, e_a), full),
            pl.BlockSpec((1, e_a), full),
            pl.BlockSpec((1, e_a), full),
        ],
        out_specs=[pl.BlockSpec((m, e_a), full), pl.BlockSpec((m, e_a), full)],
        out_shape=[jax.ShapeDtypeStruct((m, e_a), BF16),
                   jax.ShapeDtypeStruct((m, e_a), F32)],
        compiler_params=_params(("arbitrary",)),
        name="spatial_gate_sample",
    )(uv, uv, sza, ln_g, ln_b, wd_row, b0_row)


def _dot_nt(a, b):
    return lax.dot_general(a, b, (((1,), (1,)), ((), ())), preferred_element_type=F32)


def _dot_tn(a, b):
    return lax.dot_general(a, b, (((0,), (0,)), ((), ())), preferred_element_type=F32)


def _hgrn_kernel(q_ref, lf_ref, k_ref, i_ref, szb_ref, gon_ref, yb_ref, st_ref,
                 s_scr, o_scr, *, n_steps, n_heads):
    n = pl.program_id(1)
    c, hd, sb = CHUNK_B, HEAD_DIM, SUBLANES
    nv = c // sb

    @pl.when(n == 0)
    def _():
        s_scr[...] = jnp.zeros_like(s_scr)

    r8 = lax.broadcasted_iota(jnp.int32, (1, sb, hd), 1)
    t_idx = lax.broadcasted_iota(jnp.int32, (c, c), 0)
    s_idx = lax.broadcasted_iota(jnp.int32, (c, c), 1)
    fine = [h for h in (1, 2, 4) if 2 * h <= sb]
    coarse = [1 << l for l in range(sb.bit_length() - 1, (c // 2).bit_length())]
    second8 = {h: (r8 & (2 * h - 1)) >= h for h in fine}
    fine_mask = {}
    for h in fine + coarse:
        same = (t_idx >> h.bit_length()) == (s_idx >> h.bit_length())
        if h == 1:
            fine_mask[h] = same & (s_idx <= t_idx)
        else:
            fine_mask[h] = same & ((t_idx & (2 * h - 1)) >= h) & ((s_idx & (2 * h - 1)) < h)

    def tiles(x):
        return x.reshape(nv, sb, hd)

    def fine_boundary(p3, h):
        acc = None
        for blk in range(sb // (2 * h)):
            r = blk * 2 * h + h - 1
            row = jnp.broadcast_to(p3[:, r:r + 1, :], (nv, sb, hd))
            acc = row if acc is None else jnp.where(r8 >= blk * 2 * h, row, acc)
        return acc

    def one_head(h, rows):
        lf = lf_ref[h, rows, :]
        q = q_ref[h, rows, :]
        k = k_ref[h, rows, :].astype(F32)
        ib = i_ref[h, rows, :].astype(BF16)
        q3, k3, lf3 = tiles(q), tiles(k), tiles(lf)

        odd = second8[1]
        qf = jnp.where(odd, q3 * jnp.exp2(lf3), q3).reshape(c, hd).astype(BF16)
        kf = jnp.where(odd, k3 * jnp.exp2(-lf3), k3).reshape(c, hd).astype(BF16)
        att_f = jnp.where(fine_mask[1], _dot_nt(qf, kf), 0.0)
        p3 = lf3 + jnp.where(odd, tiles(pltpu.roll(lf, 1, 0)), 0.0)
        yield
        for hh in fine[1:]:
            bnd = fine_boundary(p3, hh)
            sec = second8[hh]
            g = jnp.where(sec, p3, bnd - p3)
            x = (jnp.where(sec, q3, k3) * jnp.exp2(g)).reshape(c, hd).astype(BF16)
            att_f = jnp.where(fine_mask[hh], _dot_nt(x, x), att_f)
            p3 = p3 + jnp.where(sec, bnd, 0.0)
            yield
        att = att_f

        p = p3.reshape(c, hd)
        for hh in coarse:
            xs, ps = [], []
            for blk in range(c // (2 * hh)):
                lo, mid, hi = blk * 2 * hh, blk * 2 * hh + hh, (blk + 1) * 2 * hh
                bnd = jnp.broadcast_to(p[mid - 1:mid, :], (hh, hd))
                p_lo, p_hi = p[lo:mid], p[mid:hi]
                xs.append(k[lo:mid] * jnp.exp2(bnd - p_lo))
                xs.append(q[mid:hi] * jnp.exp2(p_hi))
                ps.append(p_lo)
                ps.append(p_hi + bnd)
            x = jnp.concatenate(xs, axis=0).astype(BF16)
            att = jnp.where(fine_mask[hh], _dot_nt(x, x), att)
            p = jnp.concatenate(ps, axis=0)
            yield

        b_last = p[c - 1:c, :]
        qt = (q * jnp.exp2(p)).astype(BF16)
        kt = (k * jnp.exp2(jnp.broadcast_to(b_last, (c, hd)) - p)).astype(BF16)
        dec = jnp.exp2(b_last)
        s_old = s_scr[h]
        lhs = jnp.concatenate([att.astype(BF16), qt], axis=1)
        rhs = jnp.concatenate([ib, s_old.astype(BF16)], axis=0)
        o_scr[h] = jnp.dot(lhs, rhs, preferred_element_type=F32)
        dec_col = jnp.transpose(jnp.broadcast_to(dec, (hd, hd)))
        s_scr[h] = dec_col * s_old + _dot_tn(kt, ib)

    def chunk_body(cc, chunk_carry):
        rows = pl.ds(pl.multiple_of(cc * c, c), c)

        def head_group(grp, carry):
            heads = [one_head(grp * HEAD_UNROLL + u, rows)
                     for u in range(HEAD_UNROLL)]
            for _ in itertools.zip_longest(*heads):
                pass
            return carry

        lax.fori_loop(0, n_heads // HEAD_UNROLL, head_group, 0)

        for h in range(n_heads):
            sl = slice(h * hd, (h + 1) * hd)
            o = o_scr[h]
            ms = jnp.mean(o * o, axis=-1, keepdims=True)
            yb_ref[rows, sl] = (o * lax.rsqrt(ms + EPS) * gon_ref[:, sl]
                                * szb_ref[rows, sl].astype(F32)).astype(yb_ref.dtype)
        return chunk_carry

    lax.fori_loop(0, q_ref.shape[1] // c, chunk_body, 0)

    @pl.when(n == n_steps - 1)
    def _():
        st_ref[...] = s_scr[...]


def _hgrn_prompt(qi, lf, k, zz, g_onorm, batch, seq):
    n_heads = lf.shape[0]
    e_b = n_heads * HEAD_DIM
    rows = HGRN_STEP_CHUNKS * CHUNK_B
    assert seq % rows == 0
    n_steps = seq // rows
    hm = pl.BlockSpec((n_heads, rows, HEAD_DIM), lambda b, n: (0, b * n_steps + n, 0))
    hm_i = pl.BlockSpec((n_heads, rows, HEAD_DIM), lambda b, n: (1, b * n_steps + n, 0))
    rm = pl.BlockSpec((rows, e_b), lambda b, n: (b * n_steps + n, 0))
    rm_zb = pl.BlockSpec((rows, e_b), lambda b, n: (b * n_steps + n, 1))
    kern = functools.partial(_hgrn_kernel, n_steps=n_steps, n_heads=n_heads)
    q, iv, szb = qi, qi, zz
    return pl.pallas_call(
        kern,
        grid=(batch, n_steps),
        in_specs=[hm, hm, hm, hm_i, rm_zb, pl.BlockSpec((1, e_b), lambda b, n: (0, 0))],
        out_specs=[
            rm,
            pl.BlockSpec((None, None, n_heads, HEAD_DIM, HEAD_DIM),
                         lambda b, n: (0, b, 0, 0, 0)),
        ],
        out_shape=[
            jax.ShapeDtypeStruct((batch * seq, e_b), BF16),
            jax.ShapeDtypeStruct((1, batch, n_heads, HEAD_DIM, HEAD_DIM), F32),
        ],
        scratch_shapes=[
            pltpu.VMEM((n_heads, HEAD_DIM, HEAD_DIM), F32),
            pltpu.VMEM((n_heads, CHUNK_B, HEAD_DIM), F32),
        ],
        compiler_params=_params(("arbitrary", "arbitrary")),
        name="hgrn_prompt",
    )(q, lf, k, iv, szb, g_onorm)


def _hgrn_sample_kernel(q_ref, lf_ref, k_ref, i_ref, st_ref, szb_ref, gon_ref,
                        sto_ref, yb_ref, ft_scr, kt_scr, o_scr, *, n_heads, nb):
    s = pl.program_id(0)
    hd = HEAD_DIM

    @pl.when(s == 0)
    def _():
        def tb(h, carry):
            ft_scr[h] = jnp.transpose(jnp.exp2(lf_ref[h]))
            kt_scr[h] = jnp.transpose(k_ref[h])
            return carry
        lax.fori_loop(0, n_heads, tb, 0)

    n_seq = lf_ref.shape[1]
    shift = jnp.where(s == 0, 0, n_seq - s * nb)

    def one_head(h):
        fr = pltpu.roll(ft_scr[h], shift, 1)
        kr = pltpu.roll(kt_scr[h], shift, 1)
        for j in range(nb):
            fb = jnp.broadcast_to(fr[:, j:j + 1], (hd, hd))
            kb = jnp.broadcast_to(kr[:, j:j + 1], (hd, hd))
            irow = i_ref[h, pl.ds(s * nb + j, 1), :]
            s_new = fb * st_ref[j, h] + kb * irow
            sto_ref[j, h] = s_new
            qrow = jnp.broadcast_to(q_ref[h, pl.ds(s * nb + j, 1), :], (SUBLANES, hd))
            o = jnp.dot(qrow.astype(BF16), s_new.astype(BF16), preferred_element_type=F32)
            o_scr[h, pl.ds(j, 1), :] = o[0:1, :]
            yield

    def head_group(grp, carry):
        heads = [one_head(grp * SAMPLE_HEAD_UNROLL + u) for u in range(SAMPLE_HEAD_UNROLL)]
        for _ in itertools.zip_longest(*heads):
            pass
        return carry

    lax.fori_loop(0, n_heads // SAMPLE_HEAD_UNROLL, head_group, 0)

    for h in range(n_heads):
        sl = slice(h * hd, (h + 1) * hd)
        o = o_scr[h]
        ms = jnp.mean(o * o, axis=-1, keepdims=True)
        yb_ref[:, sl] = (o * lax.rsqrt(ms + EPS) * gon_ref[:, sl] * szb_ref[:, sl])


def _hgrn_sample(qi, lf, k, state, zz, g_onorm):
    n_heads, n_seq, hd = lf.shape
    nb = SAMPLE_SEQ_BLOCK
    e_b = n_heads * hd
    full3 = pl.BlockSpec((n_heads, n_seq, hd), lambda s: (0, 0, 0))
    full3_i = pl.BlockSpec((n_heads, n_seq, hd), lambda s: (1, 0, 0))
    st_spec = pl.BlockSpec((None, nb, n_heads, hd, hd), lambda s: (0, s, 0, 0, 0))
    kern = functools.partial(_hgrn_sample_kernel, n_heads=n_heads, nb=nb)
    q, iv, szb = qi, qi, zz
    return pl.pallas_call(
        kern,
        grid=(n_seq // nb,),
        in_specs=[full3, full3, full3, full3_i, st_spec,
                  pl.BlockSpec((nb, e_b), lambda s: (s, 1)),
                  pl.BlockSpec((1, e_b), lambda s: (0, 0))],
        out_specs=[st_spec, pl.BlockSpec((nb, e_b), lambda s: (s, 0))],
        out_shape=[jax.ShapeDtypeStruct(state.shape, F32),
                   jax.ShapeDtypeStruct((n_seq, e_b), F32)],
        scratch_shapes=[
            pltpu.VMEM((n_heads, hd, n_seq), F32),
            pltpu.VMEM((n_heads, hd, n_seq), F32),
            pltpu.VMEM((n_heads, nb, hd), F32),
        ],
        compiler_params=_params(("arbitrary",)),
        name="hgrn_sample",
    )(q, lf, k, iv, state, szb, g_onorm)


def _merge_kernel(yap_ref, ybp_ref, gap_ref, gbp_ref, yas_ref, ybs_ref, gas_ref, gbs_ref,
                  wpa_hbm, wpb_hbm, mp_ref, ms_ref, wa_bf, wb_bf, stage_ref, sem,
                  *, n_prompt_tiles, n_col_tiles, n_chunks):
    j = pl.program_id(0)
    i = pl.program_id(1)
    cur = j % 2
    _, _, kc, tn = stage_ref.shape
    weights = ((wpa_hbm, wa_bf), (wpb_hbm, wb_bf))

    def chunk_copy(widx, jj, c, slot):
        return pltpu.make_async_copy(
            weights[widx][0].at[0, pl.ds(c * kc, kc), pl.ds(jj * tn, tn)],
            stage_ref.at[slot, widx], sem.at[2 * slot + widx])

    @pl.when((j == 0) & (i == 0))
    def _():
        for widx in range(2):
            chunk_copy(widx, 0, 0, 0).start()
        for c in range(n_chunks):
            for widx in range(2):
                if c + 1 < n_chunks:
                    chunk_copy(widx, 0, c + 1, (c + 1) % 2).start()
                chunk_copy(widx, 0, c, c % 2).wait()
                weights[widx][1][0, c * kc:(c + 1) * kc, :] = (
                    stage_ref[c % 2, widx].astype(BF16))

    has_next = j + 1 < n_col_tiles

    @pl.when(has_next & (i >= 1))
    def _():
        for widx in range(2):
            chunk_copy(widx, j + 1, i - 1, (i - 1) % 2).wait()

    @pl.when(has_next & (i < n_chunks))
    def _():
        for widx in range(2):
            chunk_copy(widx, j + 1, i, i % 2).start()

    def cast_previous_chunk():
        prev = (i + n_chunks - 1) % n_chunks
        for widx in range(2):
            weights[widx][1][1 - cur, pl.ds(prev * kc, kc), :] = (
                stage_ref[(i + 1) % 2, widx].astype(BF16))

    def run(ya, yb, ga, gb, out):
        cast_previous_chunk()
        rows = ya.shape[0]
        sub = min(rows, SUB_ROWS)
        for m in range(rows // sub):
            rs = slice(m * sub, (m + 1) * sub)
            a = jnp.dot(ya[rs, :].astype(BF16), wa_bf[cur], preferred_element_type=F32)
            b = jnp.dot(yb[rs, :].astype(BF16), wb_bf[cur], preferred_element_type=F32)
            out[rs, :] = (ga[rs, :].astype(F32) * a + gb[rs, :].astype(F32) * b).astype(out.dtype)

    @pl.when(i != _sample_step(n_prompt_tiles))
    def _():
        run(yap_ref, ybp_ref, gap_ref, gbp_ref, mp_ref)

    @pl.when(i == _sample_step(n_prompt_tiles))
    def _():
        run(yas_ref, ybs_ref, gas_ref, gbs_ref, ms_ref)


def _merge(ya_p, yb_p, g_p, ya_s, yb_s, g_s, w_pa, w_pb, gate_col0):
    mp, kdim = ya_p.shape
    ms = ya_s.shape[0]
    d = w_pa.shape[-1]
    tm, tn = ROW_TILE, PROJ_COL_TILE
    npt = mp // tm
    nj = d // tn
    assert gate_col0 % tn == 0 and d % tn == 0
    g0 = gate_col0 // tn
    n_chunks = npt
    assert kdim % n_chunks == 0 and n_chunks >= 2
    kc = kdim // n_chunks

    def row_i(j, i):
        return _prompt_tile(j, i, npt)

    yp = pl.BlockSpec((tm, kdim), lambda j, i: (row_i(j, i), 0))
    ys = pl.BlockSpec((ms, kdim), lambda j, i: (0, 0))
    w = pl.BlockSpec(memory_space=pl.ANY)
    kern = functools.partial(_merge_kernel, n_prompt_tiles=npt, n_col_tiles=nj,
                             n_chunks=n_chunks)
    return pl.pallas_call(
        kern,
        grid=(nj, npt + 1),
        in_specs=[
            yp, yp,
            pl.BlockSpec((tm, tn), lambda j, i: (row_i(j, i), g0 + j)),
            pl.BlockSpec((tm, tn), lambda j, i: (row_i(j, i), g0 + nj + j)),
            ys, ys,
            pl.BlockSpec((ms, tn), lambda j, i: (0, g0 + j)),
            pl.BlockSpec((ms, tn), lambda j, i: (0, g0 + nj + j)),
            w, w,
        ],
        out_specs=[pl.BlockSpec((tm, tn), lambda j, i: (row_i(j, i), j)),
                   pl.BlockSpec((ms, tn), lambda j, i: (0, j))],
        out_shape=[jax.ShapeDtypeStruct((mp, d), BF16),
                   jax.ShapeDtypeStruct((ms, d), BF16)],
        scratch_shapes=[pltpu.VMEM((2, kdim, tn), BF16), pltpu.VMEM((2, kdim, tn), BF16),
                        pltpu.VMEM((2, 2, kc, tn), F32),
                        pltpu.SemaphoreType.DMA((4,))],
        compiler_params=_params(("arbitrary", "arbitrary")),
        name="gated_merge",
    )(ya_p, yb_p, g_p, g_p, ya_s, yb_s, g_s, g_s, w_pa, w_pb)


def _out_post_kernel(mp_ref, xp_ref, ms_ref, xs_ref, w_hbm, g_ref, op_ref, os_ref, z_scr,
                     w_ref, stage_ref, sem, *, n_prompt_tiles):
    s = pl.program_id(0)
    n_slots, kc, _ = stage_ref.shape
    n_w_chunks = w_ref.shape[0] // kc

    def chunk_copy(c, slot):
        return pltpu.make_async_copy(w_hbm.at[0, pl.ds(c * kc, kc), :],
                                     stage_ref.at[slot], sem.at[slot])

    def matmul(m_ref):
        z_scr[...] = jnp.dot(m_ref[...], w_ref[...], preferred_element_type=F32)

    def finish(x_ref, o_ref):
        z = z_scr[...]
        ms = jnp.mean(z * z, axis=-1, keepdims=True)
        o_ref[...] = x_ref[...] + z * lax.rsqrt(ms + EPS) * g_ref[...]

    @pl.when(s == 0)
    def _():
        for c in range(n_slots - 1):
            chunk_copy(c, c % n_slots).start()
        for c in range(n_w_chunks):
            ahead = c + n_slots - 1
            if ahead < n_w_chunks:
                chunk_copy(ahead, ahead % n_slots).start()
            chunk_copy(c, c % n_slots).wait()
            w_ref[c * kc:(c + 1) * kc, :] = stage_ref[c % n_slots].astype(BF16)
        matmul(mp_ref)

    @pl.when((s >= 1) & (s < n_prompt_tiles))
    def _():
        finish(xp_ref, op_ref)
        matmul(mp_ref)

    @pl.when(s == n_prompt_tiles)
    def _():
        finish(xp_ref, op_ref)
        matmul(ms_ref)

    @pl.when(s == n_prompt_tiles + 1)
    def _():
        finish(xs_ref, os_ref)


def _out_post(m_p, x_p, m_s, x_s, w3, g):
    mp, d = x_p.shape
    ms = x_s.shape[0]
    kdim = w3.shape[1]
    tm = OUT_ROWS
    assert ms == tm
    assert kdim % OUT_W_CHUNK == 0 and kdim // OUT_W_CHUNK >= OUT_W_SLOTS
    npt = mp // tm
    mm_row = pl.BlockSpec((tm, d), lambda s: (jnp.minimum(s, npt - 1), 0))
    fin_row = pl.BlockSpec((tm, d), lambda s: (jnp.clip(s - 1, 0, npt - 1), 0))
    full_s = pl.BlockSpec((ms, d), lambda s: (0, 0))
    kern = functools.partial(_out_post_kernel, n_prompt_tiles=npt)
    return pl.pallas_call(
        kern,
        grid=(npt + 2,),
        in_specs=[mm_row, fin_row, full_s, full_s,
                  pl.BlockSpec(memory_space=pl.ANY),
                  pl.BlockSpec((1, d), lambda s: (0, 0))],
        out_specs=[fin_row, full_s],
        out_shape=[jax.ShapeDtypeStruct((mp, d), F32), jax.ShapeDtypeStruct((ms, d), F32)],
        scratch_shapes=[pltpu.VMEM((tm, d), F32),
                        pltpu.VMEM((kdim, d), BF16),
                        pltpu.VMEM((OUT_W_SLOTS, OUT_W_CHUNK, d), F32),
                        pltpu.SemaphoreType.DMA((OUT_W_SLOTS,))],
        compiler_params=_params(("arbitrary",)),
        name="out_proj_post_norm",
    )(m_p, x_p, m_s, x_s, w3, g)


def kernel(x_prompt, x_sample, state_hgrn, lb_logits, g_pre, w_in, ln_g, ln_b, w_s, b_s,
           g_onorm, w_pa, w_pb, w_o, g_post):
    batch, seq, d = x_prompt.shape
    n_seq, dec_seq, _ = x_sample.shape
    depth = w_in.shape[0]
    assert depth == 1 and dec_seq == 1
    assert seq % CHUNK_A == 0 and seq % CHUNK_B == 0
    e = w_pa.shape[1]
    n_groups = w_s.shape[1]
    assert n_groups * CHUNK_A == e

    xp = x_prompt.reshape(batch * seq, d)
    xs = x_sample.reshape(n_seq, d)
    xn_p = _rmsnorm(xp, g_pre, NORM_ROWS)
    xn_s = _rmsnorm(xs, g_pre, n_seq)

    col = lambda idx: idx * e
    uvg_p, uvg_s = _proj(xn_p, xn_s, w_in, [(col(0), 2 * e), (col(7), 2 * d)],
                         [_ep_gelu, _ep_sigmoid],
                         [(BF16, BF16, False)], name="proj_uv_gates")
    zz_p, zz_s = _proj(xn_p, xn_s, w_in, [(col(2), e), (col(6), e)], [_ep_silu, _ep_silu],
                       [(BF16, F32, False)], name="proj_za_zb")
    qi_p, qi_s = _proj(xn_p, xn_s, w_in, [(col(3), e), (col(5), e)],
                       [_ep_silu, _ep_identity],
                       [(F32, F32, True)], name="proj_q_i")
    lf_p, lf_s, k_p, k_s = _proj(xn_p, xn_s, w_in, [(col(4), e)], [_ep_forget],
                                 [(F32, F32, True), (BF16, F32, True)],
                                 aux=(lb_logits,), name="proj_f")

    bs_rows = jnp.repeat(b_s[0].T, CHUNK_A, axis=1)
    ya_p, vrows_p = _gate_prompt(uvg_p, zz_p, ln_g, ln_b, w_s[0], bs_rows, batch, seq, e)
    wd_row = jnp.repeat(w_s[0, :, 0, 0], CHUNK_A)[None, :]
    b0_row = jnp.repeat(b_s[0, :, 0], CHUNK_A)[None, :]
    ya_s, vrows_s = _gate_sample(uvg_s, zz_s, ln_g, ln_b, wd_row, b0_row, e)

    yb_p, state_p = _hgrn_prompt(qi_p, lf_p, k_p, zz_p, g_onorm, batch, seq)
    state_s, yb_s = _hgrn_sample(qi_s, lf_s, k_s, state_hgrn, zz_s, g_onorm)

    m_p, m_s = _merge(ya_p, yb_p, uvg_p, ya_s, yb_s, uvg_s, w_pa, w_pb, 2 * e)
    y_p, y_s = _out_post(m_p, xp, m_s, xs, w_o, g_post)

    return (y_p.reshape(batch, seq, d), y_s.reshape(n_seq, 1, d),
            state_p, state_s,
            vrows_p, vrows_s.reshape(1, n_seq, 1, e))
```
